```python
import math
import jax, jax.numpy as jnp
from jax import lax
import numpy as np

D_MODEL = 1024
BATCH = 8
SEQ = 2048
DEPTH = 1

MLA_HEADS = D_MODEL // 128
QK_NOPE_DIM = 64
QK_ROPE_DIM = 32
V_HEAD_DIM = 64
Q_LORA_RANK = 3 * D_MODEL // 8
KV_LORA_RANK = D_MODEL // 4
MLA_WIDTH = MLA_HEADS * V_HEAD_DIM
CONV_CHANNELS = D_MODEL - MLA_WIDTH
CONV_WIDTH = 31
ROPE_THETA = 10000.0
Q_BLOCK = 128
N_EXPERTS = 16
CAPACITY_FACTOR = 2
D_FF_EXPERT = 2 * D_MODEL
IN_COLS = Q_LORA_RANK + KV_LORA_RANK + QK_ROPE_DIM + 2 * CONV_CHANNELS
DEEPNORM_ALPHA = (2.0 * DEPTH) ** 0.25
DEEPNORM_BETA = (8.0 * DEPTH) ** -0.25
LN_EPS = 1e-5
RMS_EPS = 1e-6

kernel_name = "hybrid_mla_conformer_ecmoe_deepnorm"


def layer_norm(x, g, b):
    xf = x.astype(jnp.float32)
    mu = jnp.mean(xf, axis=-1, keepdims=True)
    xc = xf - mu
    var = jnp.mean(xc * xc, axis=-1, keepdims=True)
    y = xc * lax.rsqrt(var + LN_EPS) * g.astype(jnp.float32) + b.astype(jnp.float32)
    return y.astype(x.dtype)


def rms_norm(x, g):
    xf = x.astype(jnp.float32)
    y = xf * lax.rsqrt(jnp.mean(xf * xf, axis=-1, keepdims=True) + RMS_EPS)
    return (y * g.astype(jnp.float32)).astype(x.dtype)


def rope(x, positions):
    half = x.shape[-1] // 2
    inv_freq = ROPE_THETA ** (-jnp.arange(half, dtype=jnp.float32) / half)
    ang = positions.astype(jnp.float32)[..., None] * inv_freq
    cos = jnp.cos(ang)[:, :, None, :]
    sin = jnp.sin(ang)[:, :, None, :]
    xf = x.astype(jnp.float32)
    x1, x2 = xf[..., :half], xf[..., half:]
    out = jnp.concatenate([x1 * cos - x2 * sin, x2 * cos + x1 * sin], axis=-1)
    return out.astype(x.dtype)


def mla_attention(cq, ckv, kr, positions, q_norm_g, w_qb, kv_norm_g, w_kvb):
    B, S, _ = cq.shape
    H = MLA_HEADS
    q = jnp.einsum('bsr,rf->bsf', rms_norm(cq, q_norm_g), w_qb).reshape(B, S, H, QK_NOPE_DIM + QK_ROPE_DIM)
    kv = jnp.einsum('bsr,rf->bsf', rms_norm(ckv, kv_norm_g), w_kvb).reshape(B, S, H, QK_NOPE_DIM + V_HEAD_DIM)
    q = jnp.concatenate([q[..., :QK_NOPE_DIM], rope(q[..., QK_NOPE_DIM:], positions)], axis=-1)
    k_pe = jnp.broadcast_to(rope(kr[:, :, None, :], positions), (B, S, H, QK_ROPE_DIM))
    k = jnp.concatenate([kv[..., :QK_NOPE_DIM], k_pe], axis=-1)
    v = kv[..., QK_NOPE_DIM:]
    scale = (QK_NOPE_DIM + QK_ROPE_DIM) ** -0.5
    n_blocks = S // Q_BLOCK
    q_blocks = q.reshape(B, n_blocks, Q_BLOCK, H, -1).transpose(1, 0, 2, 3, 4)

    def attend(qb):
        s = jnp.einsum('bqhd,bkhd->bhqk', qb, k).astype(jnp.float32) * scale
        p = jax.nn.softmax(s, axis=-1).astype(v.dtype)
        return jnp.einsum('bhqk,bkhd->bqhd', p, v)

    o = lax.map(attend, q_blocks)
    return o.transpose(1, 0, 2, 3, 4).reshape(B, S, H * V_HEAD_DIM)


def conformer_conv(u, conv_w, conv_b, conv_ln_g, conv_ln_b):
    a, g = u[..., :CONV_CHANNELS], u[..., CONV_CHANNELS:]
    h = a * jax.nn.sigmoid(g)
    h = lax.conv_general_dilated(
        h, conv_w[:, None, :].astype(h.dtype), window_strides=(1,), padding='SAME',
        dimension_numbers=('NWC', 'WIO', 'NWC'), feature_group_count=CONV_CHANNELS)
    h = h + conv_b
    h = layer_norm(h, conv_ln_g, conv_ln_b)
    return jax.nn.silu(h)


def expert_choice_moe(h, w_router, w_gate, w_up, w_down):
    B, S, D = h.shape
    capacity = CAPACITY_FACTOR * S // N_EXPERTS
    logits = jnp.einsum('bsd,de->bse', h, w_router).astype(jnp.float32)
    affinity = jax.nn.softmax(logits, axis=-1)
    gates, idx = lax.top_k(affinity.transpose(0, 2, 1), capacity)
    xg = jax.vmap(lambda hb, ib: hb[ib])(h, idx)
    a = jnp.einsum('becd,edf->becf', xg, w_gate)
    u = jnp.einsum('becd,edf->becf', xg, w_up)
    y = jnp.einsum('becf,efd->becd', jax.nn.silu(a) * u, w_down)
    y = y * gates[..., None].astype(y.dtype)

    def scatter(yb, ib):
        return jnp.zeros((S, D), yb.dtype).at[ib.reshape(-1)].add(yb.reshape(-1, D))

    return jax.vmap(scatter)(y, idx)


def setup_inputs(seed: int = 0) -> dict:
    key = jax.random.key(seed)
    ks = jax.random.split(key, 24)
    f32 = jnp.float32
    L = DEPTH

    def nrm(k, shape, scale):
        return jax.random.normal(k, shape, f32) * scale

    def gain(k, shape):
        return 1.0 + 0.02 * jax.random.normal(k, shape, f32)

    x = jax.random.normal(ks[0], (BATCH, SEQ, D_MODEL), f32)
    offsets = jax.random.randint(ks[1], (BATCH, 1), 0, 64, dtype=jnp.int32)
    positions = jnp.arange(SEQ, dtype=jnp.int32)[None, :] + offsets
    return {
        "x": x,
        "positions": positions,
        "emb_ln_g": gain(ks[2], (D_MODEL,)),
        "emb_ln_b": nrm(ks[3], (D_MODEL,), 0.02),
        "w_in": nrm(ks[4], (L, D_MODEL, IN_COLS), D_MODEL ** -0.5),
        "q_norm_g": gain(ks[5], (L, Q_LORA_RANK)),
        "w_qb": nrm(ks[6], (L, Q_LORA_RANK, MLA_HEADS * (QK_NOPE_DIM + QK_ROPE_DIM)), Q_LORA_RANK ** -0.5),
        "kv_norm_g": gain(ks[7], (L, KV_LORA_RANK)),
        "w_kvb": nrm(ks[8], (L, KV_LORA_RANK, MLA_HEADS * (QK_NOPE_DIM + V_HEAD_DIM)), KV_LORA_RANK ** -0.5),
        "conv_w": nrm(ks[9], (L, CONV_WIDTH, CONV_CHANNELS), CONV_WIDTH ** -0.5),
        "conv_b": nrm(ks[10], (L, CONV_CHANNELS), 0.02),
        "conv_ln_g": gain(ks[11], (L, CONV_CHANNELS)),
        "conv_ln_b": nrm(ks[12], (L, CONV_CHANNELS), 0.02),
        "w_o": nrm(ks[13], (L, MLA_WIDTH + CONV_CHANNELS, D_MODEL), D_MODEL ** -0.5 * DEEPNORM_BETA),
        "ln1_g": gain(ks[14], (L, D_MODEL)),
        "ln1_b": nrm(ks[15], (L, D_MODEL), 0.02),
        "w_router": nrm(ks[16], (L, D_MODEL, N_EXPERTS), D_MODEL ** -0.5),
        "w_gate": nrm(ks[17], (L, N_EXPERTS, D_MODEL, D_FF_EXPERT), D_MODEL ** -0.5),
        "w_up": nrm(ks[18], (L, N_EXPERTS, D_MODEL, D_FF_EXPERT), D_MODEL ** -0.5),
        "w_down": nrm(ks[19], (L, N_EXPERTS, D_FF_EXPERT, D_MODEL), D_FF_EXPERT ** -0.5 * DEEPNORM_BETA),
        "ln2_g": gain(ks[20], (L, D_MODEL)),
        "ln2_b": nrm(ks[21], (L, D_MODEL), 0.02),
    }


def reference(x, positions, emb_ln_g, emb_ln_b, w_in, q_norm_g, w_qb, kv_norm_g, w_kvb,
              conv_w, conv_b, conv_ln_g, conv_ln_b, w_o, ln1_g, ln1_b,
              w_router, w_gate, w_up, w_down, ln2_g, ln2_b):
    h = layer_norm(x, emb_ln_g, emb_ln_b)
    c1 = Q_LORA_RANK
    c2 = c1 + KV_LORA_RANK
    c3 = c2 + QK_ROPE_DIM
    for l in range(DEPTH):
        proj = jnp.einsum('bsd,df->bsf', h, w_in[l])
        attn_out = mla_attention(proj[..., :c1], proj[..., c1:c2], proj[..., c2:c3], positions,
                                 q_norm_g[l], w_qb[l], kv_norm_g[l], w_kvb[l])
        conv_out = conformer_conv(proj[..., c3:], conv_w[l], conv_b[l], conv_ln_g[l], conv_ln_b[l])
        groups = jnp.concatenate([attn_out, conv_out], axis=-1)
        mix = jnp.einsum('bsf,fd->bsd', groups, w_o[l])
        h = layer_norm(DEEPNORM_ALPHA * h + mix, ln1_g[l], ln1_b[l])
        moe = expert_choice_moe(h, w_router[l], w_gate[l], w_up[l], w_down[l])
        h = layer_norm(DEEPNORM_ALPHA * h + moe, ln2_g[l], ln2_b[l])
    return h
```

```python
import functools

import jax
import jax.numpy as jnp
from jax import lax
from jax.experimental import pallas as pl
from jax.experimental.pallas import tpu as pltpu

F32 = jnp.float32
BF16 = jnp.bfloat16

MLA_HEADS = 8
QK_NOPE_DIM = 64
QK_ROPE_DIM = 32
V_HEAD_DIM = 64
CONV_WIDTH = 31
ROPE_THETA = 10000.0
N_EXPERTS = 16
CAPACITY_FACTOR = 2
DEPTH = 1
DEEPNORM_ALPHA = (2.0 * DEPTH) ** 0.25
LN_EPS = 1e-5
RMS_EPS = 1e-6

LANES = 128
SUBLANES = 8
HEAD_PAD = LANES
VMEM_LIMIT_BYTES = 56 * 1024 * 1024

TOPK_SEARCH_STEPS = 36
TOPK_SEARCH_FLOOR = 1e-30


def _layer_norm(x, g, b):
    mu = jnp.mean(x, axis=-1, keepdims=True)
    xc = x - mu
    var = jnp.mean(xc * xc, axis=-1, keepdims=True)
    return xc * lax.rsqrt(var + LN_EPS) * g + b


def _rms_norm(x, g):
    return x * lax.rsqrt(jnp.mean(x * x, axis=-1, keepdims=True) + RMS_EPS) * g


def _rope_tile(x, cos_t, sin_t, lane):
    half = QK_ROPE_DIM // 2
    fwd = pltpu.roll(x, LANES - half, 1)
    bwd = pltpu.roll(x, half, 1)
    partner = jnp.where(lane < QK_NOPE_DIM + half, fwd, bwd)
    return x * cos_t + partner * sin_t


def _in_proj_kernel(x_ref, pos_ref, invf_ref, lng_ref, lnb_ref, win_ref, qg_ref, wqb_ref,
                    kvg_ref, wkv_ref, q_ref, k_ref, v_ref, hc_ref, *, q_rank, kv_rank, conv_ch):
    tm = x_ref.shape[0]
    h = _layer_norm(x_ref[...], lng_ref[...], lnb_ref[...])
    proj = jnp.dot(h.astype(BF16), win_ref[...], preferred_element_type=F32)
    c1 = q_rank
    c2 = c1 + kv_rank
    c3 = c2 + conv_ch
    c4 = c3 + conv_ch
    cq = proj[:, :c1]
    ckv = proj[:, c1:c2]
    a = proj[:, c2:c3]
    g = proj[:, c3:c4]
    kr = proj[:, c4:c4 + LANES]

    half = QK_ROPE_DIM // 2
    ang = pos_ref[...].astype(F32) * invf_ref[...]
    cos = jnp.cos(ang)
    sin = jnp.sin(ang)
    ones = jnp.ones((QK_NOPE_DIM, tm), F32)
    zeros = jnp.zeros((QK_NOPE_DIM, tm), F32)
    tail = LANES - QK_NOPE_DIM - QK_ROPE_DIM
    cos_t = jnp.concatenate([ones, cos, cos, ones[:tail]], axis=0).T
    sin_t = jnp.concatenate([zeros, -sin, sin, zeros[:tail]], axis=0).T
    lane = lax.broadcasted_iota(jnp.int32, (tm, LANES), 1)

    scale = (QK_NOPE_DIM + QK_ROPE_DIM) ** -0.5
    cqn = _rms_norm(cq, qg_ref[...])
    q = jnp.dot(cqn.astype(BF16), wqb_ref[...], preferred_element_type=F32) * scale
    ckvn = _rms_norm(ckv, kvg_ref[...])
    kv = jnp.dot(ckvn.astype(BF16), wkv_ref[...], preferred_element_type=F32)
    k_pe = _rope_tile(kr, cos_t, sin_t, lane)
    nk = MLA_HEADS * HEAD_PAD
    for hd in range(MLA_HEADS):
        sl = slice(hd * HEAD_PAD, (hd + 1) * HEAD_PAD)
        q_ref[:, sl] = _rope_tile(q[:, sl], cos_t, sin_t, lane).astype(BF16)
        k_ref[:, sl] = (kv[:, sl] + k_pe).astype(BF16)
    v_ref[...] = kv[:, nk:].astype(BF16)
    hc_ref[...] = a * (1.0 / (1.0 + jnp.exp(-g)))


def _in_proj(x2d, pos_row, invf_col, lng, lnb, w_in_r, qg, w_qb_pad, kvg, w_kv_r, *, tm):
    T, D = x2d.shape
    q_rank = w_qb_pad.shape[0]
    kv_rank = w_kv_r.shape[0]
    conv_ch = (w_in_r.shape[1] - q_rank - kv_rank - LANES) // 2
    nk = MLA_HEADS * HEAD_PAD
    nv = MLA_HEADS * V_HEAD_DIM
    full = lambda a: pl.BlockSpec(a.shape, lambda i: (0,) * a.ndim)
    return pl.pallas_call(
        functools.partial(_in_proj_kernel, q_rank=q_rank, kv_rank=kv_rank, conv_ch=conv_ch),
        grid=(T // tm,),
        in_specs=[
            pl.BlockSpec((tm, D), lambda i: (i, 0)),
            pl.BlockSpec((1, tm), lambda i: (0, i)),
            full(invf_col), full(lng), full(lnb), full(w_in_r), full(qg), full(w_qb_pad),
            full(kvg), full(w_kv_r),
        ],
        out_specs=[
            pl.BlockSpec((tm, nk), lambda i: (i, 0)),
            pl.BlockSpec((tm, nk), lambda i: (i, 0)),
            pl.BlockSpec((tm, nv), lambda i: (i, 0)),
            pl.BlockSpec((tm, conv_ch), lambda i: (i, 0)),
        ],
        out_shape=[
            jax.ShapeDtypeStruct((T, nk), BF16),
            jax.ShapeDtypeStruct((T, nk), BF16),
            jax.ShapeDtypeStruct((T, nv), BF16),
            jax.ShapeDtypeStruct((T, conv_ch), F32),
        ],
        compiler_params=pltpu.CompilerParams(
            dimension_semantics=("arbitrary",), vmem_limit_bytes=VMEM_LIMIT_BYTES),
        name="in_proj",
    )(x2d, pos_row, invf_col, lng, lnb, w_in_r, qg, w_qb_pad, kvg, w_kv_r)


CONV_PAD_ROWS = 16
CONV_ROW_CHUNK = 128
CONV_WINDOW = CONV_ROW_CHUNK + 2 * CONV_PAD_ROWS


def _conv_kernel(hc_ref, cw_ref, cb_ref, g_ref, b_ref, o_ref, pad_ref):
    S, C = hc_ref.shape[1], hc_ref.shape[2]
    pad_ref[0:CONV_PAD_ROWS, :] = jnp.zeros((CONV_PAD_ROWS, C), F32)
    pad_ref[CONV_PAD_ROWS + S:, :] = jnp.zeros((CONV_PAD_ROWS, C), F32)
    pad_ref[CONV_PAD_ROWS:CONV_PAD_ROWS + S, :] = hc_ref[0]
    first = CONV_PAD_ROWS - CONV_WIDTH // 2

    def chunk(r, carry):
        base = pl.multiple_of(r * CONV_ROW_CHUNK, CONV_ROW_CHUNK)
        cols = []
        for c0 in range(0, C, LANES):
            win = pad_ref[pl.ds(base, CONV_WINDOW), c0:c0 + LANES]
            acc = jnp.zeros((CONV_ROW_CHUNK, LANES), F32)
            for res in range(SUBLANES):
                shifted = win if res == 0 else pltpu.roll(win, CONV_WINDOW - res, 0)
                for j in range(CONV_WIDTH):
                    off = first + j
                    if off % SUBLANES == res:
                        a0 = off - res
                        acc = acc + shifted[a0:a0 + CONV_ROW_CHUNK, :] * cw_ref[j:j + 1, c0:c0 + LANES]
            cols.append(acc)
        y = _layer_norm(jnp.concatenate(cols, axis=1) + cb_ref[...], g_ref[...], b_ref[...])
        o_ref[0, pl.ds(base, CONV_ROW_CHUNK), :] = (y * (1.0 / (1.0 + jnp.exp(-y)))).astype(BF16)
        return carry

    lax.fori_loop(0, S // CONV_ROW_CHUNK, chunk, 0)


def _conv(hc, cw, cb, g, b):
    B, S, C = hc.shape
    full = lambda a: pl.BlockSpec(a.shape, lambda i: (0,) * a.ndim)
    return pl.pallas_call(
        _conv_kernel,
        grid=(B,),
        in_specs=[pl.BlockSpec((1, S, C), lambda i: (i, 0, 0)), full(cw), full(cb), full(g), full(b)],
        out_specs=pl.BlockSpec((1, S, C), lambda i: (i, 0, 0)),
        out_shape=jax.ShapeDtypeStruct((B, S, C), BF16),
        scratch_shapes=[pltpu.VMEM((S + 2 * CONV_PAD_ROWS, C), F32)],
        compiler_params=pltpu.CompilerParams(
            dimension_semantics=("arbitrary",), vmem_limit_bytes=VMEM_LIMIT_BYTES),
        name="conv",
    )(hc, cw, cb, g, b)


HEADS_PER_STEP = LANES // V_HEAD_DIM


def _attn_kernel(q_ref, k_ref, v_ref, o_ref, *, tq):
    S = q_ref.shape[1]
    lane = lax.broadcasted_iota(jnp.int32, (tq, LANES), 1)
    v = v_ref[0]
    for qi in range(S // tq):
        rows = slice(qi * tq, (qi + 1) * tq)
        outs = []
        for hh in range(HEADS_PER_STEP):
            cols = slice(hh * HEAD_PAD, (hh + 1) * HEAD_PAD)
            s = lax.dot_general(q_ref[0, rows, cols], k_ref[0, :, cols],
                                (((1,), (1,)), ((), ())), preferred_element_type=F32)
            p = jnp.exp(s - jnp.max(s, axis=-1, keepdims=True))
            l = jnp.sum(p, axis=-1, keepdims=True)
            o = jnp.dot(p.astype(BF16), v, preferred_element_type=F32)
            outs.append(o * (1.0 / l))
        o_ref[0, rows, :] = jnp.where(lane < V_HEAD_DIM, outs[0], outs[1]).astype(BF16)


def _attn(q, k, v, *, tq):
    B, S, _ = q.shape
    qw = HEADS_PER_STEP * HEAD_PAD
    return pl.pallas_call(
        functools.partial(_attn_kernel, tq=tq),
        grid=(B, MLA_HEADS // HEADS_PER_STEP),
        in_specs=[
            pl.BlockSpec((1, S, qw), lambda b, h: (b, 0, h)),
            pl.BlockSpec((1, S, qw), lambda b, h: (b, 0, h)),
            pl.BlockSpec((1, S, LANES), lambda b, h: (b, 0, h)),
        ],
        out_specs=pl.BlockSpec((1, S, LANES), lambda b, h: (b, 0, h)),
        out_shape=jax.ShapeDtypeStruct((B, S, MLA_HEADS * V_HEAD_DIM), BF16),
        compiler_params=pltpu.CompilerParams(
            dimension_semantics=("arbitrary", "arbitrary"), vmem_limit_bytes=VMEM_LIMIT_BYTES),
        name="attn",
    )(q, k, v)


def _out_proj_kernel(x_ref, att_ref, cv_ref, lng_ref, lnb_ref, wo_ref, g1_ref, b1_ref, wr_ref,
                     h1_ref, h1b_ref, lt_ref):
    na = att_ref.shape[1]
    h0 = _layer_norm(x_ref[...], lng_ref[...], lnb_ref[...])
    mix = jnp.dot(att_ref[...], wo_ref[:na, :], preferred_element_type=F32)
    mix = mix + jnp.dot(cv_ref[...], wo_ref[na:, :], preferred_element_type=F32)
    h1 = _layer_norm(DEEPNORM_ALPHA * h0 + mix, g1_ref[...], b1_ref[...])
    h1_ref[...] = h1
    h1b_ref[...] = h1.astype(BF16)
    lt_ref[0] = lax.dot_general(wr_ref[...], h1, (((1,), (1,)), ((), ())),
                                preferred_element_type=F32, precision=lax.Precision.HIGHEST)


def _out_proj(x2d, att, cv, lng, lnb, w_o, g1, b1, w_rt, *, tm, seq):
    T, D = x2d.shape
    E = w_rt.shape[0]
    B = T // seq
    per = seq // tm
    full = lambda a: pl.BlockSpec(a.shape, lambda i: (0,) * a.ndim)
    return pl.pallas_call(
        _out_proj_kernel,
        grid=(T // tm,),
        in_specs=[
            pl.BlockSpec((tm, D), lambda i: (i, 0)),
            pl.BlockSpec((tm, att.shape[1]), lambda i: (i, 0)),
            pl.BlockSpec((tm, cv.shape[1]), lambda i: (i, 0)),
            full(lng), full(lnb), full(w_o), full(g1), full(b1), full(w_rt),
        ],
        out_specs=[
            pl.BlockSpec((tm, D), lambda i: (i, 0)),
            pl.BlockSpec((tm, D), lambda i: (i, 0)),
            pl.BlockSpec((1, E, tm), lambda i: (i // per, 0, i % per)),
        ],
        out_shape=[
            jax.ShapeDtypeStruct((T, D), F32),
            jax.ShapeDtypeStruct((T, D), BF16),
            jax.ShapeDtypeStruct((B, E, seq), F32),
        ],
        compiler_params=pltpu.CompilerParams(
            dimension_semantics=("arbitrary",), vmem_limit_bytes=VMEM_LIMIT_BYTES),
        name="out_proj",
    )(x2d, att, cv, lng, lnb, w_o, g1, b1, w_rt)


def _route_kernel(lt_ref, rank_ref, gate_ref, *, cap):
    B, E, S = lt_ref.shape
    lt = lt_ref[...]
    ex = jnp.exp(lt - jnp.max(lt, axis=1, keepdims=True))
    aff = ex / jnp.sum(ex, axis=1, keepdims=True)
    v = aff.reshape(B * E, S)
    rows = B * E
    kf = float(cap)

    def step(_, carry):
        lo, hi = carry
        mid = jnp.sqrt(jnp.maximum(lo, TOPK_SEARCH_FLOOR)) * jnp.sqrt(hi)
        cnt = jnp.sum(jnp.where(v >= mid, 1.0, 0.0), axis=1, keepdims=True)
        ge = cnt >= kf
        return jnp.where(ge, mid, lo), jnp.where(ge, hi, mid)

    lo, hi = lax.fori_loop(0, TOPK_SEARCH_STEPS, step,
                           (jnp.zeros((rows, 1), F32), jnp.full((rows, 1), 2.0, F32)))
    above = v >= hi
    tie = jnp.logical_and(v >= lo, jnp.logical_not(above))
    stacked = jnp.concatenate([jnp.where(above, 1.0, 0.0), jnp.where(tie, 1.0, 0.0)], axis=0)
    upper = jnp.where(lax.broadcasted_iota(jnp.int32, (S, S), 0) <= lax.broadcasted_iota(jnp.int32, (S, S), 1),
                      1.0, 0.0).astype(BF16)
    pc = jnp.dot(stacked.astype(BF16), upper, preferred_element_type=F32)
    pa = pc[:rows]
    pt = pc[rows:]
    need = kf - jnp.sum(jnp.where(above, 1.0, 0.0), axis=1, keepdims=True)
    sel = jnp.logical_or(above, jnp.logical_and(tie, pt <= need))
    rank = pa + jnp.minimum(pt, need) - 1.0
    rank_ref[...] = jnp.where(sel, rank, -1.0).astype(jnp.int32).reshape(B, E, S)
    gate_ref[...] = jnp.where(sel, v, 0.0).reshape(B, E, S)


def _route(lt, *, cap):
    B, E, S = lt.shape
    spec = pl.BlockSpec((B, E, S), lambda i: (0, 0, 0))
    return pl.pallas_call(
        functools.partial(_route_kernel, cap=cap),
        grid=(1,),
        in_specs=[spec],
        out_specs=[spec, spec],
        out_shape=[jax.ShapeDtypeStruct((B, E, S), jnp.int32), jax.ShapeDtypeStruct((B, E, S), F32)],
        compiler_params=pltpu.CompilerParams(
            dimension_semantics=("arbitrary",), vmem_limit_bytes=VMEM_LIMIT_BYTES),
        name="route",
    )(lt)


def _dispatch_kernel(rank_ref, h_ref, xg_ref, *, cap):
    E, S = rank_ref.shape[1], rank_ref.shape[2]
    slot = lax.broadcasted_iota(jnp.int32, (cap, S), 0)
    hb = h_ref[0]
    for e in range(E):
        onehot = jnp.where(rank_ref[0, e:e + 1, :] == slot, 1.0, 0.0).astype(BF16)
        xg_ref[e, 0] = jnp.dot(onehot, hb, preferred_element_type=F32).astype(BF16)


def _dispatch(rank, h1b, *, cap):
    B, E, S = rank.shape
    D = h1b.shape[-1]
    return pl.pallas_call(
        functools.partial(_dispatch_kernel, cap=cap),
        grid=(B,),
        in_specs=[pl.BlockSpec((1, E, S), lambda b: (b, 0, 0)), pl.BlockSpec((1, S, D), lambda b: (b, 0, 0))],
        out_specs=pl.BlockSpec((E, 1, cap, D), lambda b: (0, b, 0, 0)),
        out_shape=jax.ShapeDtypeStruct((E, B, cap, D), BF16),
        compiler_params=pltpu.CompilerParams(
            dimension_semantics=("arbitrary",), vmem_limit_bytes=VMEM_LIMIT_BYTES),
        name="dispatch",
    )(rank, h1b)


def _expert_kernel(x_ref, wg_ref, wu_ref, wd_ref, y_ref, acc_ref):
    f = pl.program_id(1)
    B, cap, D = x_ref.shape[1], x_ref.shape[2], x_ref.shape[3]
    x = x_ref[0].reshape(B * cap, D)
    a = jnp.dot(x, wg_ref[0].astype(BF16), preferred_element_type=F32)
    u = jnp.dot(x, wu_ref[0].astype(BF16), preferred_element_type=F32)
    hmid = (a * (1.0 / (1.0 + jnp.exp(-a))) * u).astype(BF16)
    part = jnp.dot(hmid, wd_ref[0].astype(BF16), preferred_element_type=F32)

    @pl.when(f == 0)
    def _():
        acc_ref[...] = part

    @pl.when(f > 0)
    def _():
        acc_ref[...] += part

    @pl.when(f == pl.num_programs(1) - 1)
    def _():
        y_ref[0] = acc_ref[...].reshape(B, cap, D).astype(BF16)


def _experts(xg, w_gate, w_up, w_down, *, tf):
    E, B, cap, D = xg.shape
    F = w_gate.shape[-1]
    return pl.pallas_call(
        _expert_kernel,
        grid=(E, F // tf),
        in_specs=[
            pl.BlockSpec((1, B, cap, D), lambda e, f: (e, 0, 0, 0)),
            pl.BlockSpec((1, D, tf), lambda e, f: (e, 0, f)),
            pl.BlockSpec((1, D, tf), lambda e, f: (e, 0, f)),
            pl.BlockSpec((1, tf, D), lambda e, f: (e, f, 0)),
        ],
        out_specs=pl.BlockSpec((1, B, cap, D), lambda e, f: (e, 0, 0, 0)),
        out_shape=jax.ShapeDtypeStruct((E, B, cap, D), BF16),
        scratch_shapes=[pltpu.VMEM((B * cap, D), F32)],
        compiler_params=pltpu.CompilerParams(
            dimension_semantics=("arbitrary", "arbitrary"), vmem_limit_bytes=VMEM_LIMIT_BYTES),
        name="experts",
    )(xg, w_gate, w_up, w_down)


def _combine_kernel(rank_ref, gate_ref, y_ref, h_ref, g2_ref, b2_ref, o_ref, *, cap):
    ts, E = rank_ref.shape[1], rank_ref.shape[2]
    slot = lax.broadcasted_iota(jnp.int32, (ts, cap), 1)
    acc = DEEPNORM_ALPHA * h_ref[0]
    for e in range(E):
        w = jnp.where(rank_ref[0, :, e:e + 1] == slot, gate_ref[0, :, e:e + 1], 0.0).astype(BF16)
        acc = acc + jnp.dot(w, y_ref[e, 0], preferred_element_type=F32)
    o_ref[0] = _layer_norm(acc, g2_ref[...], b2_ref[...])


def _combine(rank_t, gate_t, y, h1, g2, b2, *, ts):
    B, S, E = rank_t.shape
    cap, D = y.shape[2], y.shape[3]
    full = lambda a: pl.BlockSpec(a.shape, lambda b, s: (0,) * a.ndim)
    return pl.pallas_call(
        functools.partial(_combine_kernel, cap=cap),
        grid=(B, S // ts),
        in_specs=[
            pl.BlockSpec((1, ts, E), lambda b, s: (b, s, 0)),
            pl.BlockSpec((1, ts, E), lambda b, s: (b, s, 0)),
            pl.BlockSpec((E, 1, cap, D), lambda b, s: (0, b, 0, 0)),
            pl.BlockSpec((1, ts, D), lambda b, s: (b, s, 0)),
            full(g2), full(b2),
        ],
        out_specs=pl.BlockSpec((1, ts, D), lambda b, s: (b, s, 0)),
        out_shape=jax.ShapeDtypeStruct((B, S, D), F32),
        compiler_params=pltpu.CompilerParams(
            dimension_semantics=("arbitrary", "arbitrary"), vmem_limit_bytes=VMEM_LIMIT_BYTES),
        name="combine",
    )(rank_t, gate_t, y, h1, g2, b2)


def _tile(n, target):
    t = min(n, target)
    assert n % t == 0, (n, t)
    return t


def kernel(x, positions, emb_ln_g, emb_ln_b, w_in, q_norm_g, w_qb, kv_norm_g, w_kvb, conv_w, conv_b,
           conv_ln_g, conv_ln_b, w_o, ln1_g, ln1_b, w_router, w_gate, w_up, w_down, ln2_g, ln2_b):
    B, S, D = x.shape
    T = B * S
    H = MLA_HEADS
    q_rank = q_norm_g.shape[-1]
    kv_rank = kv_norm_g.shape[-1]
    conv_ch = conv_w.shape[-1]
    qk_dim = QK_NOPE_DIM + QK_ROPE_DIM
    cap = CAPACITY_FACTOR * S // N_EXPERTS
    assert w_in.shape[0] == DEPTH == 1
    row = lambda a: a.reshape(1, -1)

    wi = w_in[0]
    c1, c2, c3 = q_rank, q_rank + kv_rank, q_rank + kv_rank + QK_ROPE_DIM
    tail = LANES - QK_NOPE_DIM - QK_ROPE_DIM
    kr_cols = jnp.pad(wi[:, c2:c3], ((0, 0), (QK_NOPE_DIM, tail)))
    w_in_r = jnp.concatenate(
        [wi[:, :c2], wi[:, c3:c3 + conv_ch], wi[:, c3 + conv_ch:], kr_cols], axis=1).astype(BF16)
    w_qb_pad = jnp.pad(w_qb[0].reshape(q_rank, H, qk_dim),
                       ((0, 0), (0, 0), (0, HEAD_PAD - qk_dim))).reshape(q_rank, H * HEAD_PAD).astype(BF16)
    wkv = w_kvb[0].reshape(kv_rank, H, QK_NOPE_DIM + V_HEAD_DIM)
    wk_pad = jnp.pad(wkv[:, :, :QK_NOPE_DIM], ((0, 0), (0, 0), (0, HEAD_PAD - QK_NOPE_DIM)))
    w_kv_r = jnp.concatenate([wk_pad.reshape(kv_rank, H * HEAD_PAD),
                              wkv[:, :, QK_NOPE_DIM:].reshape(kv_rank, H * V_HEAD_DIM)], axis=1).astype(BF16)
    half = QK_ROPE_DIM // 2
    inv_freq = (ROPE_THETA ** (-jnp.arange(half, dtype=F32) / half)).reshape(half, 1)

    x2d = x.reshape(T, D)
    tm = _tile(S, 512)
    q, k, v, hc = _in_proj(x2d, positions.reshape(1, T), inv_freq, row(emb_ln_g), row(emb_ln_b), w_in_r,
                           row(q_norm_g[0]), w_qb_pad, row(kv_norm_g[0]), w_kv_r, tm=tm)
    cv = _conv(hc.reshape(B, S, conv_ch), conv_w[0], row(conv_b[0]), row(conv_ln_g[0]), row(conv_ln_b[0]))
    att = _attn(q.reshape(B, S, H * HEAD_PAD), k.reshape(B, S, H * HEAD_PAD),
                v.reshape(B, S, H * V_HEAD_DIM), tq=_tile(S, 512))
    h1, h1b, lt = _out_proj(x2d, att.reshape(T, H * V_HEAD_DIM), cv.reshape(T, conv_ch), row(emb_ln_g),
                            row(emb_ln_b), w_o[0].astype(BF16), row(ln1_g[0]), row(ln1_b[0]),
                            w_router[0].T, tm=tm, seq=S)
    rank, gate = _route(lt, cap=cap)
    xg = _dispatch(rank, h1b.reshape(B, S, D), cap=cap)
    y = _experts(xg, w_gate[0], w_up[0], w_down[0], tf=_tile(w_gate.shape[-1], 512))
    out = _combine(jnp.swapaxes(rank, 1, 2), jnp.swapaxes(gate, 1, 2), y, h1.reshape(B, S, D),
                   row(ln2_g[0]), row(ln2_b[0]), ts=_tile(S, 1024))
    return out
```

```python
import functools

import jax
import jax.numpy as jnp
from jax import lax
from jax.experimental import pallas as pl
from jax.experimental.pallas import tpu as pltpu

F32 = jnp.float32
BF16 = jnp.bfloat16

MLA_HEADS = 8
QK_NOPE_DIM = 64
QK_ROPE_DIM = 32
V_HEAD_DIM = 64
CONV_WIDTH = 31
ROPE_THETA = 10000.0
N_EXPERTS = 16
CAPACITY_FACTOR = 2
DEPTH = 1
DEEPNORM_ALPHA = (2.0 * DEPTH) ** 0.25
LN_EPS = 1e-5
RMS_EPS = 1e-6

LANES = 128
SUBLANES = 8
HEAD_PAD = LANES
VMEM_LIMIT_BYTES = 56 * 1024 * 1024
TOKEN_SUB_TILE = 256

TOPK_SEARCH_STEPS = 36
TOPK_SEARCH_FLOOR = 1e-30


def _layer_norm(x, g, b):
    mu = jnp.mean(x, axis=-1, keepdims=True)
    xc = x - mu
    var = jnp.mean(xc * xc, axis=-1, keepdims=True)
    return xc * lax.rsqrt(var + LN_EPS) * g + b


def _rms_norm(x, g):
    return x * lax.rsqrt(jnp.mean(x * x, axis=-1, keepdims=True) + RMS_EPS) * g


def _rope_tile(x, cos_t, sin_t, lane):
    half = QK_ROPE_DIM // 2
    fwd = pltpu.roll(x, LANES - half, 1)
    bwd = pltpu.roll(x, half, 1)
    partner = jnp.where(lane < QK_NOPE_DIM + half, fwd, bwd)
    return x * cos_t + partner * sin_t


def _in_proj_kernel(x_ref, pos_ref, invf_ref, lng_ref, lnb_ref, win_ref, qg_ref, wqb_ref,
                    kvg_ref, wkv_ref, q_ref, k_ref, v_ref, hc_ref, *, q_rank, kv_rank, conv_ch):
    tm = x_ref.shape[0]
    sub = min(tm, TOKEN_SUB_TILE)
    c1 = q_rank
    c2 = c1 + kv_rank
    c3 = c2 + conv_ch
    c4 = c3 + conv_ch
    tail = LANES - QK_NOPE_DIM - QK_ROPE_DIM
    nk = MLA_HEADS * HEAD_PAD
    scale = (QK_NOPE_DIM + QK_ROPE_DIM) ** -0.5
    ones = jnp.ones((QK_NOPE_DIM, sub), F32)
    zeros = jnp.zeros((QK_NOPE_DIM, sub), F32)
    lane = lax.broadcasted_iota(jnp.int32, (sub, LANES), 1)
    for r0 in range(0, tm, sub):
        rows = slice(r0, r0 + sub)
        h = _layer_norm(x_ref[rows, :], lng_ref[...], lnb_ref[...])
        proj = jnp.dot(h.astype(BF16), win_ref[...], preferred_element_type=F32)
        cq = proj[:, :c1]
        ckv = proj[:, c1:c2]
        a = proj[:, c2:c3]
        g = proj[:, c3:c4]
        kr = proj[:, c4:c4 + LANES]

        ang = pos_ref[:, rows].astype(F32) * invf_ref[...]
        cos = jnp.cos(ang)
        sin = jnp.sin(ang)
        cos_t = jnp.concatenate([ones, cos, cos, ones[:tail]], axis=0).T
        sin_t = jnp.concatenate([zeros, -sin, sin, zeros[:tail]], axis=0).T

        cqn = _rms_norm(cq, qg_ref[...])
        q = jnp.dot(cqn.astype(BF16), wqb_ref[...], preferred_element_type=F32) * scale
        ckvn = _rms_norm(ckv, kvg_ref[...])
        kv = jnp.dot(ckvn.astype(BF16), wkv_ref[...], preferred_element_type=F32)
        k_pe = _rope_tile(kr, cos_t, sin_t, lane)
        for hd in range(MLA_HEADS):
            sl = slice(hd * HEAD_PAD, (hd + 1) * HEAD_PAD)
            q_ref[rows, sl] = _rope_tile(q[:, sl], cos_t, sin_t, lane).astype(BF16)
            k_ref[rows, sl] = (kv[:, sl] + k_pe).astype(BF16)
        v_ref[rows, :] = kv[:, nk:].astype(BF16)
        hc_ref[rows, :] = a * (1.0 / (1.0 + jnp.exp(-g)))


def _in_proj(x2d, pos_row, invf_col, lng, lnb, w_in_r, qg, w_qb_pad, kvg, w_kv_r, *, tm):
    T, D = x2d.shape
    q_rank = w_qb_pad.shape[0]
    kv_rank = w_kv_r.shape[0]
    conv_ch = (w_in_r.shape[1] - q_rank - kv_rank - LANES) // 2
    nk = MLA_HEADS * HEAD_PAD
    nv = MLA_HEADS * V_HEAD_DIM
    full = lambda a: pl.BlockSpec(a.shape, lambda i: (0,) * a.ndim)
    return pl.pallas_call(
        functools.partial(_in_proj_kernel, q_rank=q_rank, kv_rank=kv_rank, conv_ch=conv_ch),
        grid=(T // tm,),
        in_specs=[
            pl.BlockSpec((tm, D), lambda i: (i, 0)),
            pl.BlockSpec((1, tm), lambda i: (0, i)),
            full(invf_col), full(lng), full(lnb), full(w_in_r), full(qg), full(w_qb_pad),
            full(kvg), full(w_kv_r),
        ],
        out_specs=[
            pl.BlockSpec((tm, nk), lambda i: (i, 0)),
            pl.BlockSpec((tm, nk), lambda i: (i, 0)),
            pl.BlockSpec((tm, nv), lambda i: (i, 0)),
            pl.BlockSpec((tm, conv_ch), lambda i: (i, 0)),
        ],
        out_shape=[
            jax.ShapeDtypeStruct((T, nk), BF16),
            jax.ShapeDtypeStruct((T, nk), BF16),
            jax.ShapeDtypeStruct((T, nv), BF16),
            jax.ShapeDtypeStruct((T, conv_ch), F32),
        ],
        compiler_params=pltpu.CompilerParams(
            dimension_semantics=("arbitrary",), vmem_limit_bytes=VMEM_LIMIT_BYTES),
        name="in_proj",
    )(x2d, pos_row, invf_col, lng, lnb, w_in_r, qg, w_qb_pad, kvg, w_kv_r)


CONV_PAD_ROWS = 16
CONV_ROW_CHUNK = 128
CONV_WINDOW = CONV_ROW_CHUNK + 2 * CONV_PAD_ROWS


def _conv_kernel(hc_ref, cw_ref, cb_ref, g_ref, b_ref, o_ref, pad_ref):
    S, C = hc_ref.shape[1], hc_ref.shape[2]
    pad_ref[0:CONV_PAD_ROWS, :] = jnp.zeros((CONV_PAD_ROWS, C), F32)
    pad_ref[CONV_PAD_ROWS + S:, :] = jnp.zeros((CONV_PAD_ROWS, C), F32)
    pad_ref[CONV_PAD_ROWS:CONV_PAD_ROWS + S, :] = hc_ref[0]
    first = CONV_PAD_ROWS - CONV_WIDTH // 2

    def chunk(r, carry):
        base = pl.multiple_of(r * CONV_ROW_CHUNK, CONV_ROW_CHUNK)
        cols = []
        for c0 in range(0, C, LANES):
            win = pad_ref[pl.ds(base, CONV_WINDOW), c0:c0 + LANES]
            acc = jnp.zeros((CONV_ROW_CHUNK, LANES), F32)
            for res in range(SUBLANES):
                shifted = win if res == 0 else pltpu.roll(win, CONV_WINDOW - res, 0)
                for j in range(CONV_WIDTH):
                    off = first + j
                    if off % SUBLANES == res:
                        a0 = off - res
                        acc = acc + shifted[a0:a0 + CONV_ROW_CHUNK, :] * cw_ref[j:j + 1, c0:c0 + LANES]
            cols.append(acc)
        y = _layer_norm(jnp.concatenate(cols, axis=1) + cb_ref[...], g_ref[...], b_ref[...])
        o_ref[0, pl.ds(base, CONV_ROW_CHUNK), :] = (y * (1.0 / (1.0 + jnp.exp(-y)))).astype(BF16)
        return carry

    lax.fori_loop(0, S // CONV_ROW_CHUNK, chunk, 0)


def _conv(hc, cw, cb, g, b):
    B, S, C = hc.shape
    full = lambda a: pl.BlockSpec(a.shape, lambda i: (0,) * a.ndim)
    return pl.pallas_call(
        _conv_kernel,
        grid=(B,),
        in_specs=[pl.BlockSpec((1, S, C), lambda i: (i, 0, 0)), full(cw), full(cb), full(g), full(b)],
        out_specs=pl.BlockSpec((1, S, C), lambda i: (i, 0, 0)),
        out_shape=jax.ShapeDtypeStruct((B, S, C), BF16),
        scratch_shapes=[pltpu.VMEM((S + 2 * CONV_PAD_ROWS, C), F32)],
        compiler_params=pltpu.CompilerParams(
            dimension_semantics=("arbitrary",), vmem_limit_bytes=VMEM_LIMIT_BYTES),
        name="conv",
    )(hc, cw, cb, g, b)


HEADS_PER_STEP = LANES // V_HEAD_DIM


def _attn_kernel(q_ref, k_ref, v_ref, o_ref, *, tq):
    S = q_ref.shape[1]
    lane = lax.broadcasted_iota(jnp.int32, (tq, LANES), 1)
    v = v_ref[0]
    for qi in range(S // tq):
        rows = slice(qi * tq, (qi + 1) * tq)
        outs = []
        for hh in range(HEADS_PER_STEP):
            cols = slice(hh * HEAD_PAD, (hh + 1) * HEAD_PAD)
            s = lax.dot_general(q_ref[0, rows, cols], k_ref[0, :, cols],
                                (((1,), (1,)), ((), ())), preferred_element_type=F32)
            p = jnp.exp(s - jnp.max(s, axis=-1, keepdims=True))
            l = jnp.sum(p, axis=-1, keepdims=True)
            o = jnp.dot(p.astype(BF16), v, preferred_element_type=F32)
            outs.append(o * (1.0 / l))
        o_ref[0, rows, :] = jnp.where(lane < V_HEAD_DIM, outs[0], outs[1]).astype(BF16)


def _attn(q, k, v, *, tq):
    B, S, _ = q.shape
    qw = HEADS_PER_STEP * HEAD_PAD
    return pl.pallas_call(
        functools.partial(_attn_kernel, tq=tq),
        grid=(B, MLA_HEADS // HEADS_PER_STEP),
        in_specs=[
            pl.BlockSpec((1, S, qw), lambda b, h: (b, 0, h)),
            pl.BlockSpec((1, S, qw), lambda b, h: (b, 0, h)),
            pl.BlockSpec((1, S, LANES), lambda b, h: (b, 0, h)),
        ],
        out_specs=pl.BlockSpec((1, S, LANES), lambda b, h: (b, 0, h)),
        out_shape=jax.ShapeDtypeStruct((B, S, MLA_HEADS * V_HEAD_DIM), BF16),
        compiler_params=pltpu.CompilerParams(
            dimension_semantics=("arbitrary", "arbitrary"), vmem_limit_bytes=VMEM_LIMIT_BYTES),
        name="attn",
    )(q, k, v)


def _out_proj_kernel(x_ref, att_ref, cv_ref, lng_ref, lnb_ref, wo_ref, g1_ref, b1_ref, wr_ref,
                     h1_ref, h1b_ref, lg_ref):
    na = att_ref.shape[1]
    tm = x_ref.shape[0]
    sub = min(tm, TOKEN_SUB_TILE)
    for r0 in range(0, tm, sub):
        rows = slice(r0, r0 + sub)
        h0 = _layer_norm(x_ref[rows, :], lng_ref[...], lnb_ref[...])
        mix = jnp.dot(att_ref[rows, :], wo_ref[:na, :], preferred_element_type=F32)
        mix = mix + jnp.dot(cv_ref[rows, :], wo_ref[na:, :], preferred_element_type=F32)
        h1 = _layer_norm(DEEPNORM_ALPHA * h0 + mix, g1_ref[...], b1_ref[...])
        h1_ref[rows, :] = h1
        h1b = h1.astype(BF16)
        h1b_ref[rows, :] = h1b
        lg_ref[rows, :] = jnp.dot(h1b, wr_ref[...], preferred_element_type=F32)


def _out_proj(x2d, att, cv, lng, lnb, w_o, g1, b1, w_r2, *, tm):
    T, D = x2d.shape
    full = lambda a: pl.BlockSpec(a.shape, lambda i: (0,) * a.ndim)
    return pl.pallas_call(
        _out_proj_kernel,
        grid=(T // tm,),
        in_specs=[
            pl.BlockSpec((tm, D), lambda i: (i, 0)),
            pl.BlockSpec((tm, att.shape[1]), lambda i: (i, 0)),
            pl.BlockSpec((tm, cv.shape[1]), lambda i: (i, 0)),
            full(lng), full(lnb), full(w_o), full(g1), full(b1), full(w_r2),
        ],
        out_specs=[
            pl.BlockSpec((tm, D), lambda i: (i, 0)),
            pl.BlockSpec((tm, D), lambda i: (i, 0)),
            pl.BlockSpec((tm, w_r2.shape[1]), lambda i: (i, 0)),
        ],
        out_shape=[
            jax.ShapeDtypeStruct((T, D), F32),
            jax.ShapeDtypeStruct((T, D), BF16),
            jax.ShapeDtypeStruct((T, w_r2.shape[1]), F32),
        ],
        compiler_params=pltpu.CompilerParams(
            dimension_semantics=("arbitrary",), vmem_limit_bytes=VMEM_LIMIT_BYTES),
        name="out_proj",
    )(x2d, att, cv, lng, lnb, w_o, g1, b1, w_r2)


def _route_kernel(lt_ref, rank_ref, gate_ref, *, cap):
    B, E, S = rank_ref.shape
    lt = lt_ref[:, :E, :] + lt_ref[:, E:, :]
    ex = jnp.exp(lt - jnp.max(lt, axis=1, keepdims=True))
    aff = ex / jnp.sum(ex, axis=1, keepdims=True)
    v = aff.reshape(B * E, S)
    rows = B * E
    kf = float(cap)

    def step(_, carry):
        lo, hi = carry
        mid = jnp.sqrt(jnp.maximum(lo, TOPK_SEARCH_FLOOR)) * jnp.sqrt(hi)
        cnt = jnp.sum(jnp.where(v >= mid, 1.0, 0.0), axis=1, keepdims=True)
        ge = cnt >= kf
        return jnp.where(ge, mid, lo), jnp.where(ge, hi, mid)

    lo, hi = lax.fori_loop(0, TOPK_SEARCH_STEPS, step,
                           (jnp.zeros((rows, 1), F32), jnp.full((rows, 1), 2.0, F32)))
    above = v >= hi
    tie = jnp.logical_and(v >= lo, jnp.logical_not(above))
    stacked = jnp.concatenate([jnp.where(above, 1.0, 0.0), jnp.where(tie, 1.0, 0.0)], axis=0)
    upper = jnp.where(lax.broadcasted_iota(jnp.int32, (S, S), 0) <= lax.broadcasted_iota(jnp.int32, (S, S), 1),
                      1.0, 0.0).astype(BF16)
    pc = jnp.dot(stacked.astype(BF16), upper, preferred_element_type=F32)
    pa = pc[:rows]
    pt = pc[rows:]
    need = kf - jnp.sum(jnp.where(above, 1.0, 0.0), axis=1, keepdims=True)
    sel = jnp.logical_or(above, jnp.logical_and(tie, pt <= need))
    rank = pa + jnp.minimum(pt, need) - 1.0
    rank_ref[...] = jnp.where(sel, rank, -1.0).astype(jnp.int32).reshape(B, E, S)
    gate_ref[...] = jnp.where(sel, v, 0.0).reshape(B, E, S)


def _route(lt2, *, cap):
    B, E2, S = lt2.shape
    E = E2 // 2
    spec = pl.BlockSpec((B, E, S), lambda i: (0, 0, 0))
    return pl.pallas_call(
        functools.partial(_route_kernel, cap=cap),
        grid=(1,),
        in_specs=[pl.BlockSpec((B, E2, S), lambda i: (0, 0, 0))],
        out_specs=[spec, spec],
        out_shape=[jax.ShapeDtypeStruct((B, E, S), jnp.int32), jax.ShapeDtypeStruct((B, E, S), F32)],
        compiler_params=pltpu.CompilerParams(
            dimension_semantics=("arbitrary",), vmem_limit_bytes=VMEM_LIMIT_BYTES),
        name="route",
    )(lt2)


def _dispatch_kernel(rank_ref, h_ref, xg_ref, *, cap):
    E, S = rank_ref.shape[1], rank_ref.shape[2]
    slot = lax.broadcasted_iota(jnp.int32, (cap, S), 0)
    hb = h_ref[0]
    for e in range(E):
        onehot = jnp.where(rank_ref[0, e:e + 1, :] == slot, 1.0, 0.0).astype(BF16)
        xg_ref[e, 0] = jnp.dot(onehot, hb, preferred_element_type=F32).astype(BF16)


def _dispatch(rank, h1b, *, cap):
    B, E, S = rank.shape
    D = h1b.shape[-1]
    return pl.pallas_call(
        functools.partial(_dispatch_kernel, cap=cap),
        grid=(B,),
        in_specs=[pl.BlockSpec((1, E, S), lambda b: (b, 0, 0)), pl.BlockSpec((1, S, D), lambda b: (b, 0, 0))],
        out_specs=pl.BlockSpec((E, 1, cap, D), lambda b: (0, b, 0, 0)),
        out_shape=jax.ShapeDtypeStruct((E, B, cap, D), BF16),
        compiler_params=pltpu.CompilerParams(
            dimension_semantics=("arbitrary",), vmem_limit_bytes=VMEM_LIMIT_BYTES),
        name="dispatch",
    )(rank, h1b)


EXPERT_ROW_BLOCK = 512


def _expert_kernel(x_ref, wg_ref, wu_ref, wd_ref, y_ref, acc_ref, wgb_ref, wub_ref, wdb_ref):
    f = pl.program_id(1)
    rows = x_ref.shape[1]
    rb = min(rows, EXPERT_ROW_BLOCK)

    @pl.when(f == 0)
    def _():
        acc_ref[...] = jnp.zeros(acc_ref.shape, F32)

    wgb_ref[...] = wg_ref[0].astype(BF16)
    wub_ref[...] = wu_ref[0].astype(BF16)
    wdb_ref[...] = wd_ref[0].astype(BF16)
    for r0 in range(0, rows, rb):
        x = x_ref[0, r0:r0 + rb, :]
        a = jnp.dot(x, wgb_ref[...], preferred_element_type=F32)
        u = jnp.dot(x, wub_ref[...], preferred_element_type=F32)
        hmid = (a * (1.0 / (1.0 + jnp.exp(-a))) * u).astype(BF16)
        acc_ref[r0:r0 + rb, :] += jnp.dot(hmid, wdb_ref[...], preferred_element_type=F32)

    @pl.when(f == pl.num_programs(1) - 1)
    def _():
        y_ref[0] = acc_ref[...].astype(BF16)


def _experts(xg, w_gate, w_up, w_down, *, tf):
    E, rows, D = xg.shape
    F = w_gate.shape[-1]
    return pl.pallas_call(
        _expert_kernel,
        grid=(E, F // tf),
        in_specs=[
            pl.BlockSpec((1, rows, D), lambda e, f: (e, 0, 0)),
            pl.BlockSpec((1, D, tf), lambda e, f: (e, 0, f)),
            pl.BlockSpec((1, D, tf), lambda e, f: (e, 0, f)),
            pl.BlockSpec((1, tf, D), lambda e, f: (e, f, 0)),
        ],
        out_specs=pl.BlockSpec((1, rows, D), lambda e, f: (e, 0, 0)),
        out_shape=jax.ShapeDtypeStruct((E, rows, D), BF16),
        scratch_shapes=[pltpu.VMEM((rows, D), F32), pltpu.VMEM((D, tf), BF16), pltpu.VMEM((D, tf), BF16),
                        pltpu.VMEM((tf, D), BF16)],
        compiler_params=pltpu.CompilerParams(
            dimension_semantics=("arbitrary", "arbitrary"), vmem_limit_bytes=VMEM_LIMIT_BYTES),
        name="experts",
    )(xg, w_gate, w_up, w_down)


def _combine_kernel(rank_ref, gate_ref, y_ref, h_ref, g2_ref, b2_ref, o_ref, *, cap):
    ts, E = rank_ref.shape[1], rank_ref.shape[2]
    slot = lax.broadcasted_iota(jnp.int32, (ts, cap), 1)
    acc = DEEPNORM_ALPHA * h_ref[0]
    for e in range(E):
        w = jnp.where(rank_ref[0, :, e:e + 1] == slot, gate_ref[0, :, e:e + 1], 0.0).astype(BF16)
        acc = acc + jnp.dot(w, y_ref[e, 0], preferred_element_type=F32)
    o_ref[0] = _layer_norm(acc, g2_ref[...], b2_ref[...])


def _combine(rank_t, gate_t, y, h1, g2, b2, *, ts):
    B, S, E = rank_t.shape
    cap, D = y.shape[2], y.shape[3]
    full = lambda a: pl.BlockSpec(a.shape, lambda b, s: (0,) * a.ndim)
    return pl.pallas_call(
        functools.partial(_combine_kernel, cap=cap),
        grid=(B, S // ts),
        in_specs=[
            pl.BlockSpec((1, ts, E), lambda b, s: (b, s, 0)),
            pl.BlockSpec((1, ts, E), lambda b, s: (b, s, 0)),
            pl.BlockSpec((E, 1, cap, D), lambda b, s: (0, b, 0, 0)),
            pl.BlockSpec((1, ts, D), lambda b, s: (b, s, 0)),
            full(g2), full(b2),
        ],
        out_specs=pl.BlockSpec((1, ts, D), lambda b, s: (b, s, 0)),
        out_shape=jax.ShapeDtypeStruct((B, S, D), F32),
        compiler_params=pltpu.CompilerParams(
            dimension_semantics=("arbitrary", "arbitrary"), vmem_limit_bytes=VMEM_LIMIT_BYTES),
        name="combine",
    )(rank_t, gate_t, y, h1, g2, b2)


def _tile(n, target):
    t = min(n, target)
    assert n % t == 0, (n, t)
    return t


def kernel(x, positions, emb_ln_g, emb_ln_b, w_in, q_norm_g, w_qb, kv_norm_g, w_kvb, conv_w, conv_b,
           conv_ln_g, conv_ln_b, w_o, ln1_g, ln1_b, w_router, w_gate, w_up, w_down, ln2_g, ln2_b):
    B, S, D = x.shape
    T = B * S
    H = MLA_HEADS
    q_rank = q_norm_g.shape[-1]
    kv_rank = kv_norm_g.shape[-1]
    conv_ch = conv_w.shape[-1]
    qk_dim = QK_NOPE_DIM + QK_ROPE_DIM
    cap = CAPACITY_FACTOR * S // N_EXPERTS
    assert w_in.shape[0] == DEPTH == 1
    row = lambda a: a.reshape(1, -1)

    wi = w_in[0]
    c1, c2, c3 = q_rank, q_rank + kv_rank, q_rank + kv_rank + QK_ROPE_DIM
    tail = LANES - QK_NOPE_DIM - QK_ROPE_DIM
    kr_cols = jnp.pad(wi[:, c2:c3], ((0, 0), (QK_NOPE_DIM, tail)))
    w_in_r = jnp.concatenate(
        [wi[:, :c2], wi[:, c3:c3 + conv_ch], wi[:, c3 + conv_ch:], kr_cols], axis=1).astype(BF16)
    w_qb_pad = jnp.pad(w_qb[0].reshape(q_rank, H, qk_dim),
                       ((0, 0), (0, 0), (0, HEAD_PAD - qk_dim))).reshape(q_rank, H * HEAD_PAD).astype(BF16)
    wkv = w_kvb[0].reshape(kv_rank, H, QK_NOPE_DIM + V_HEAD_DIM)
    wk_pad = jnp.pad(wkv[:, :, :QK_NOPE_DIM], ((0, 0), (0, 0), (0, HEAD_PAD - QK_NOPE_DIM)))
    w_kv_r = jnp.concatenate([wk_pad.reshape(kv_rank, H * HEAD_PAD),
                              wkv[:, :, QK_NOPE_DIM:].reshape(kv_rank, H * V_HEAD_DIM)], axis=1).astype(BF16)
    half = QK_ROPE_DIM // 2
    inv_freq = (ROPE_THETA ** (-jnp.arange(half, dtype=F32) / half)).reshape(half, 1)

    x2d = x.reshape(T, D)
    tm = _tile(S, 512)
    q, k, v, hc = _in_proj(x2d, positions.reshape(1, T), inv_freq, row(emb_ln_g), row(emb_ln_b), w_in_r,
                           row(q_norm_g[0]), w_qb_pad, row(kv_norm_g[0]), w_kv_r, tm=tm)
    cv = _conv(hc.reshape(B, S, conv_ch), conv_w[0], row(conv_b[0]), row(conv_ln_g[0]), row(conv_ln_b[0]))
    att = _attn(q.reshape(B, S, H * HEAD_PAD), k.reshape(B, S, H * HEAD_PAD),
                v.reshape(B, S, H * V_HEAD_DIM), tq=_tile(S, 512))
    wr_hi = w_router[0].astype(BF16)
    w_r2 = jnp.concatenate([wr_hi, (w_router[0] - wr_hi.astype(F32)).astype(BF16)], axis=1)
    h1, h1b, lg = _out_proj(x2d, att.reshape(T, H * V_HEAD_DIM), cv.reshape(T, conv_ch), row(emb_ln_g),
                            row(emb_ln_b), w_o[0].astype(BF16), row(ln1_g[0]), row(ln1_b[0]),
                            w_r2, tm=_tile(S, 1024))
    rank, gate = _route(jnp.swapaxes(lg.reshape(B, S, 2 * N_EXPERTS), 1, 2), cap=cap)
    xg = _dispatch(rank, h1b.reshape(B, S, D), cap=cap)
    y = _experts(xg.reshape(N_EXPERTS, B * cap, D), w_gate[0], w_up[0], w_down[0],
                 tf=_tile(w_gate.shape[-1], 512)).reshape(N_EXPERTS, B, cap, D)
    out = _combine(jnp.swapaxes(rank, 1, 2), jnp.swapaxes(gate, 1, 2), y, h1.reshape(B, S, D),
                   row(ln2_g[0]), row(ln2_b[0]), ts=_tile(S, 1024))
    return out
```

```python
import functools

import jax
import jax.numpy as jnp
from jax import lax
from jax.experimental import pallas as pl
from jax.experimental.pallas import tpu as pltpu

F32 = jnp.float32
BF16 = jnp.bfloat16

MLA_HEADS = 8
QK_NOPE_DIM = 64
QK_ROPE_DIM = 32
V_HEAD_DIM = 64
CONV_WIDTH = 31
ROPE_THETA = 10000.0
N_EXPERTS = 16
CAPACITY_FACTOR = 2
DEPTH = 1
DEEPNORM_ALPHA = (2.0 * DEPTH) ** 0.25
LN_EPS = 1e-5
RMS_EPS = 1e-6

LANES = 128
SUBLANES = 8
HEAD_PAD = LANES
VMEM_LIMIT_BYTES = 56 * 1024 * 1024
TOKEN_SUB_TILE = 256

TOPK_SEARCH_STEPS = 36
TOPK_SEARCH_FLOOR = 1e-30


def _layer_norm(x, g, b):
    mu = jnp.mean(x, axis=-1, keepdims=True)
    xc = x - mu
    var = jnp.mean(xc * xc, axis=-1, keepdims=True)
    return xc * lax.rsqrt(var + LN_EPS) * g + b


def _rms_norm(x, g):
    return x * lax.rsqrt(jnp.mean(x * x, axis=-1, keepdims=True) + RMS_EPS) * g


def _rope_tile(x, cos_t, sin_t, lane):
    half = QK_ROPE_DIM // 2
    fwd = pltpu.roll(x, LANES - half, 1)
    bwd = pltpu.roll(x, half, 1)
    partner = jnp.where(lane < QK_NOPE_DIM + half, fwd, bwd)
    return x * cos_t + partner * sin_t


def _in_proj_kernel(x_ref, pos_ref, invf_ref, lng_ref, lnb_ref, win_ref, qg_ref, wqb_ref,
                    kvg_ref, wkv_ref, q_ref, k_ref, v_ref, hc_ref, *, q_rank, kv_rank, conv_ch):
    tm = x_ref.shape[0]
    sub = min(tm, TOKEN_SUB_TILE)
    c1 = q_rank
    c2 = c1 + kv_rank
    c3 = c2 + conv_ch
    c4 = c3 + conv_ch
    tail = LANES - QK_NOPE_DIM - QK_ROPE_DIM
    nk = MLA_HEADS * HEAD_PAD
    scale = (QK_NOPE_DIM + QK_ROPE_DIM) ** -0.5
    ones = jnp.ones((QK_NOPE_DIM, sub), F32)
    zeros = jnp.zeros((QK_NOPE_DIM, sub), F32)
    lane = lax.broadcasted_iota(jnp.int32, (sub, LANES), 1)
    for r0 in range(0, tm, sub):
        rows = slice(r0, r0 + sub)
        h = _layer_norm(x_ref[rows, :], lng_ref[...], lnb_ref[...])
        proj = jnp.dot(h.astype(BF16), win_ref[...], preferred_element_type=F32)
        cq = proj[:, :c1]
        ckv = proj[:, c1:c2]
        a = proj[:, c2:c3]
        g = proj[:, c3:c4]
        kr = proj[:, c4:c4 + LANES]

        ang = pos_ref[:, rows].astype(F32) * invf_ref[...]
        cos = jnp.cos(ang)
        sin = jnp.sin(ang)
        cos_t = jnp.concatenate([ones, cos, cos, ones[:tail]], axis=0).T
        sin_t = jnp.concatenate([zeros, -sin, sin, zeros[:tail]], axis=0).T

        cqn = _rms_norm(cq, qg_ref[...])
        q = jnp.dot(cqn.astype(BF16), wqb_ref[...], preferred_element_type=F32) * scale
        ckvn = _rms_norm(ckv, kvg_ref[...])
        kv = jnp.dot(ckvn.astype(BF16), wkv_ref[...], preferred_element_type=F32)
        k_pe = _rope_tile(kr, cos_t, sin_t, lane)
        for hd in range(MLA_HEADS):
            sl = slice(hd * HEAD_PAD, (hd + 1) * HEAD_PAD)
            q_ref[rows, sl] = _rope_tile(q[:, sl], cos_t, sin_t, lane).astype(BF16)
            k_ref[rows, sl] = (kv[:, sl] + k_pe).astype(BF16)
        v_ref[rows, :] = kv[:, nk:].astype(BF16)
        hc_ref[rows, :] = a * (1.0 / (1.0 + jnp.exp(-g)))


def _in_proj(x2d, pos_row, invf_col, lng, lnb, w_in_r, qg, w_qb_pad, kvg, w_kv_r, *, tm):
    T, D = x2d.shape
    q_rank = w_qb_pad.shape[0]
    kv_rank = w_kv_r.shape[0]
    conv_ch = (w_in_r.shape[1] - q_rank - kv_rank - LANES) // 2
    nk = MLA_HEADS * HEAD_PAD
    nv = MLA_HEADS * V_HEAD_DIM
    full = lambda a: pl.BlockSpec(a.shape, lambda i: (0,) * a.ndim)
    return pl.pallas_call(
        functools.partial(_in_proj_kernel, q_rank=q_rank, kv_rank=kv_rank, conv_ch=conv_ch),
        grid=(T // tm,),
        in_specs=[
            pl.BlockSpec((tm, D), lambda i: (i, 0)),
            pl.BlockSpec((1, tm), lambda i: (0, i)),
            full(invf_col), full(lng), full(lnb), full(w_in_r), full(qg), full(w_qb_pad),
            full(kvg), full(w_kv_r),
        ],
        out_specs=[
            pl.BlockSpec((tm, nk), lambda i: (i, 0)),
            pl.BlockSpec((tm, nk), lambda i: (i, 0)),
            pl.BlockSpec((tm, nv), lambda i: (i, 0)),
            pl.BlockSpec((tm, conv_ch), lambda i: (i, 0)),
        ],
        out_shape=[
            jax.ShapeDtypeStruct((T, nk), BF16),
            jax.ShapeDtypeStruct((T, nk), BF16),
            jax.ShapeDtypeStruct((T, nv), BF16),
            jax.ShapeDtypeStruct((T, conv_ch), F32),
        ],
        compiler_params=pltpu.CompilerParams(
            dimension_semantics=("arbitrary",), vmem_limit_bytes=VMEM_LIMIT_BYTES),
        name="in_proj",
    )(x2d, pos_row, invf_col, lng, lnb, w_in_r, qg, w_qb_pad, kvg, w_kv_r)


CONV_PAD_ROWS = 16
CONV_ROW_CHUNK = 128
CONV_WINDOW = CONV_ROW_CHUNK + 2 * CONV_PAD_ROWS


def _conv_kernel(hc_ref, cw_ref, cb_ref, g_ref, b_ref, o_ref, pad_ref):
    S, C = hc_ref.shape[1], hc_ref.shape[2]
    pad_ref[0:CONV_PAD_ROWS, :] = jnp.zeros((CONV_PAD_ROWS, C), F32)
    pad_ref[CONV_PAD_ROWS + S:, :] = jnp.zeros((CONV_PAD_ROWS, C), F32)
    pad_ref[CONV_PAD_ROWS:CONV_PAD_ROWS + S, :] = hc_ref[0]
    first = CONV_PAD_ROWS - CONV_WIDTH // 2

    def chunk(r, carry):
        base = pl.multiple_of(r * CONV_ROW_CHUNK, CONV_ROW_CHUNK)
        cols = []
        for c0 in range(0, C, LANES):
            win = pad_ref[pl.ds(base, CONV_WINDOW), c0:c0 + LANES]
            acc = jnp.zeros((CONV_ROW_CHUNK, LANES), F32)
            for res in range(SUBLANES):
                shifted = win if res == 0 else pltpu.roll(win, CONV_WINDOW - res, 0)
                for j in range(CONV_WIDTH):
                    off = first + j
                    if off % SUBLANES == res:
                        a0 = off - res
                        acc = acc + shifted[a0:a0 + CONV_ROW_CHUNK, :] * cw_ref[j:j + 1, c0:c0 + LANES]
            cols.append(acc)
        y = _layer_norm(jnp.concatenate(cols, axis=1) + cb_ref[...], g_ref[...], b_ref[...])
        o_ref[0, pl.ds(base, CONV_ROW_CHUNK), :] = (y * (1.0 / (1.0 + jnp.exp(-y)))).astype(BF16)
        return carry

    lax.fori_loop(0, S // CONV_ROW_CHUNK, chunk, 0)


def _conv(hc, cw, cb, g, b):
    B, S, C = hc.shape
    full = lambda a: pl.BlockSpec(a.shape, lambda i: (0,) * a.ndim)
    return pl.pallas_call(
        _conv_kernel,
        grid=(B,),
        in_specs=[pl.BlockSpec((1, S, C), lambda i: (i, 0, 0)), full(cw), full(cb), full(g), full(b)],
        out_specs=pl.BlockSpec((1, S, C), lambda i: (i, 0, 0)),
        out_shape=jax.ShapeDtypeStruct((B, S, C), BF16),
        scratch_shapes=[pltpu.VMEM((S + 2 * CONV_PAD_ROWS, C), F32)],
        compiler_params=pltpu.CompilerParams(
            dimension_semantics=("arbitrary",), vmem_limit_bytes=VMEM_LIMIT_BYTES),
        name="conv",
    )(hc, cw, cb, g, b)


HEADS_PER_STEP = LANES // V_HEAD_DIM


def _attn_kernel(q_ref, k_ref, v_ref, o_ref, *, tq):
    S = q_ref.shape[1]
    lane = lax.broadcasted_iota(jnp.int32, (tq, LANES), 1)
    v = v_ref[0]
    for qi in range(S // tq):
        rows = slice(qi * tq, (qi + 1) * tq)
        outs = []
        for hh in range(HEADS_PER_STEP):
            cols = slice(hh * HEAD_PAD, (hh + 1) * HEAD_PAD)
            s = lax.dot_general(q_ref[0, rows, cols], k_ref[0, :, cols],
                                (((1,), (1,)), ((), ())), preferred_element_type=F32)
            p = jnp.exp(s - jnp.max(s, axis=-1, keepdims=True))
            l = jnp.sum(p, axis=-1, keepdims=True)
            o = jnp.dot(p.astype(BF16), v, preferred_element_type=F32)
            outs.append(o * (1.0 / l))
        o_ref[0, rows, :] = jnp.where(lane < V_HEAD_DIM, outs[0], outs[1]).astype(BF16)


def _attn(q, k, v, *, tq):
    B, S, _ = q.shape
    qw = HEADS_PER_STEP * HEAD_PAD
    return pl.pallas_call(
        functools.partial(_attn_kernel, tq=tq),
        grid=(B, MLA_HEADS // HEADS_PER_STEP),
        in_specs=[
            pl.BlockSpec((1, S, qw), lambda b, h: (b, 0, h)),
            pl.BlockSpec((1, S, qw), lambda b, h: (b, 0, h)),
            pl.BlockSpec((1, S, LANES), lambda b, h: (b, 0, h)),
        ],
        out_specs=pl.BlockSpec((1, S, LANES), lambda b, h: (b, 0, h)),
        out_shape=jax.ShapeDtypeStruct((B, S, MLA_HEADS * V_HEAD_DIM), BF16),
        compiler_params=pltpu.CompilerParams(
            dimension_semantics=("arbitrary", "arbitrary"), vmem_limit_bytes=VMEM_LIMIT_BYTES),
        name="attn",
    )(q, k, v)


def _out_proj_kernel(x_ref, att_ref, cv_ref, lng_ref, lnb_ref, wo_ref, g1_ref, b1_ref, wr_ref,
                     h1_ref, h1b_ref, lg_ref):
    na = att_ref.shape[1]
    tm = x_ref.shape[0]
    sub = min(tm, TOKEN_SUB_TILE)
    for r0 in range(0, tm, sub):
        rows = slice(r0, r0 + sub)
        h0 = _layer_norm(x_ref[rows, :], lng_ref[...], lnb_ref[...])
        mix = jnp.dot(att_ref[rows, :], wo_ref[:na, :], preferred_element_type=F32)
        mix = mix + jnp.dot(cv_ref[rows, :], wo_ref[na:, :], preferred_element_type=F32)
        h1 = _layer_norm(DEEPNORM_ALPHA * h0 + mix, g1_ref[...], b1_ref[...])
        h1_ref[rows, :] = h1
        h1b = h1.astype(BF16)
        h1b_ref[rows, :] = h1b
        lg_ref[rows, :] = jnp.dot(h1b, wr_ref[...], preferred_element_type=F32)


def _out_proj(x2d, att, cv, lng, lnb, w_o, g1, b1, w_r2, *, tm):
    T, D = x2d.shape
    full = lambda a: pl.BlockSpec(a.shape, lambda i: (0,) * a.ndim)
    return pl.pallas_call(
        _out_proj_kernel,
        grid=(T // tm,),
        in_specs=[
            pl.BlockSpec((tm, D), lambda i: (i, 0)),
            pl.BlockSpec((tm, att.shape[1]), lambda i: (i, 0)),
            pl.BlockSpec((tm, cv.shape[1]), lambda i: (i, 0)),
            full(lng), full(lnb), full(w_o), full(g1), full(b1), full(w_r2),
        ],
        out_specs=[
            pl.BlockSpec((tm, D), lambda i: (i, 0)),
            pl.BlockSpec((tm, D), lambda i: (i, 0)),
            pl.BlockSpec((tm, w_r2.shape[1]), lambda i: (i, 0)),
        ],
        out_shape=[
            jax.ShapeDtypeStruct((T, D), F32),
            jax.ShapeDtypeStruct((T, D), BF16),
            jax.ShapeDtypeStruct((T, w_r2.shape[1]), F32),
        ],
        compiler_params=pltpu.CompilerParams(
            dimension_semantics=("arbitrary",), vmem_limit_bytes=VMEM_LIMIT_BYTES),
        name="out_proj",
    )(x2d, att, cv, lng, lnb, w_o, g1, b1, w_r2)


def _route_kernel(lt_ref, rank_ref, gate_ref, off_ref, *, cap, tb):
    B, E, S = rank_ref.shape
    lt = lt_ref[:, :E, :] + lt_ref[:, E:, :]
    ex = jnp.exp(lt - jnp.max(lt, axis=1, keepdims=True))
    aff = ex / jnp.sum(ex, axis=1, keepdims=True)
    v = aff.reshape(B * E, S)
    rows = B * E
    kf = float(cap)

    def step(_, carry):
        lo, hi = carry
        mid = jnp.sqrt(jnp.maximum(lo, TOPK_SEARCH_FLOOR)) * jnp.sqrt(hi)
        cnt = jnp.sum(jnp.where(v >= mid, 1.0, 0.0), axis=1, keepdims=True)
        ge = cnt >= kf
        return jnp.where(ge, mid, lo), jnp.where(ge, hi, mid)

    lo, hi = lax.fori_loop(0, TOPK_SEARCH_STEPS, step,
                           (jnp.zeros((rows, 1), F32), jnp.full((rows, 1), 2.0, F32)))
    above = v >= hi
    tie = jnp.logical_and(v >= lo, jnp.logical_not(above))
    stacked = jnp.concatenate([jnp.where(above, 1.0, 0.0), jnp.where(tie, 1.0, 0.0)], axis=0)
    upper = jnp.where(lax.broadcasted_iota(jnp.int32, (S, S), 0) <= lax.broadcasted_iota(jnp.int32, (S, S), 1),
                      1.0, 0.0).astype(BF16)
    pc = jnp.dot(stacked.astype(BF16), upper, preferred_element_type=F32)
    pa = pc[:rows]
    pt = pc[rows:]
    need = kf - jnp.sum(jnp.where(above, 1.0, 0.0), axis=1, keepdims=True)
    sel = jnp.logical_or(above, jnp.logical_and(tie, pt <= need))
    taken = pa + jnp.minimum(pt, need)
    rank_ref[...] = jnp.where(sel, taken - 1.0, -1.0).astype(jnp.int32).reshape(B, E, S)
    gate_ref[...] = jnp.where(sel, v, 0.0).reshape(B, E, S)
    pick = jnp.where(lax.broadcasted_iota(jnp.int32, (S, LANES), 0) + 1
                     == lax.broadcasted_iota(jnp.int32, (S, LANES), 1) * tb, 1.0, 0.0).astype(BF16)
    off = jnp.dot(taken.astype(BF16), pick, preferred_element_type=F32)
    off_ref[...] = off.astype(jnp.int32).reshape(B, E, LANES)


def _route(lt2, *, cap, tb):
    B, E2, S = lt2.shape
    E = E2 // 2
    assert cap <= 256 and S // tb < LANES
    spec = pl.BlockSpec((B, E, S), lambda i: (0, 0, 0))
    ospec = pl.BlockSpec((B, E, LANES), lambda i: (0, 0, 0))
    return pl.pallas_call(
        functools.partial(_route_kernel, cap=cap, tb=tb),
        grid=(1,),
        in_specs=[pl.BlockSpec((B, E2, S), lambda i: (0, 0, 0))],
        out_specs=[spec, spec, ospec],
        out_shape=[jax.ShapeDtypeStruct((B, E, S), jnp.int32), jax.ShapeDtypeStruct((B, E, S), F32),
                   jax.ShapeDtypeStruct((B, E, LANES), jnp.int32)],
        compiler_params=pltpu.CompilerParams(
            dimension_semantics=("arbitrary",), vmem_limit_bytes=VMEM_LIMIT_BYTES),
        name="route",
    )(lt2)


ROUTE_TOKEN_BLOCK = 256
SLOT_WINDOW = 64
SLOT_ALIGN = 16


def _slot_windows(off_ref, base, n_experts, cap, win):
    starts = []
    n_pass = jnp.int32(1)
    for e in range(n_experts):
        off = off_ref[base + e]
        end = off_ref[base + n_experts + e]
        start = jnp.minimum((off // SLOT_ALIGN) * SLOT_ALIGN, cap - win)
        starts.append(start)
        n_pass = jnp.maximum(n_pass, (end - start + (win - 1)) // win)
    return starts, n_pass


def _dispatch_kernel(off_ref, rank_ref, h_ref, xg_ref, *, cap, win):
    E, tb = rank_ref.shape[1], rank_ref.shape[2]
    b, j = pl.program_id(0), pl.program_id(1)
    starts, n_pass = _slot_windows(off_ref, (b * (pl.num_programs(1) + 1) + j) * E, E, cap, win)

    @pl.when(j == 0)
    def _():
        xg_ref[...] = jnp.zeros(xg_ref.shape, BF16)

    hb = h_ref[0]
    sub = lax.broadcasted_iota(jnp.int32, (win, tb), 0)

    def one_pass(p, first):
        pieces, rows = [], []
        for e in range(E):
            lo = starts[e] + p * win
            ws = pl.multiple_of(jnp.minimum(lo, cap - win), SLOT_ALIGN)
            tgt = ws + sub
            hit = rank_ref[0, e:e + 1, :] == tgt
            if not first:
                hit = jnp.logical_and(hit, tgt >= lo)
            pieces.append(jnp.where(hit, 1.0, 0.0).astype(BF16))
            rows.append(ws)
        onehot = jnp.concatenate(pieces, axis=0)
        got = jnp.dot(onehot, hb, preferred_element_type=F32).astype(BF16)
        for e in range(E):
            xg_ref[e, 0, pl.ds(rows[e], win), :] += got[e * win:(e + 1) * win, :]

    one_pass(0, True)

    def extra(p, carry):
        one_pass(p, False)
        return carry

    lax.fori_loop(1, n_pass, extra, 0)


def _dispatch(off_flat, rank, h1b, *, cap, tb):
    B, E, S = rank.shape
    D = h1b.shape[-1]
    win = min(SLOT_WINDOW, cap)
    assert cap % SLOT_ALIGN == 0 and win % SLOT_ALIGN == 0
    return pl.pallas_call(
        functools.partial(_dispatch_kernel, cap=cap, win=win),
        grid_spec=pltpu.PrefetchScalarGridSpec(
            num_scalar_prefetch=1,
            grid=(B, S // tb),
            in_specs=[pl.BlockSpec((1, E, tb), lambda b, j, off: (b, 0, j)),
                      pl.BlockSpec((1, tb, D), lambda b, j, off: (b, j, 0))],
            out_specs=pl.BlockSpec((E, 1, cap, D), lambda b, j, off: (0, b, 0, 0)),
        ),
        out_shape=jax.ShapeDtypeStruct((E, B, cap, D), BF16),
        compiler_params=pltpu.CompilerParams(
            dimension_semantics=("arbitrary", "arbitrary"), vmem_limit_bytes=VMEM_LIMIT_BYTES),
        name="dispatch",
    )(off_flat, rank, h1b)


EXPERT_ROW_BLOCK = 512


def _expert_kernel(x_ref, wg_ref, wu_ref, wd_ref, y_ref, acc_ref, wgb_ref, wub_ref, wdb_ref):
    f = pl.program_id(1)
    rows = x_ref.shape[1]
    rb = min(rows, EXPERT_ROW_BLOCK)

    @pl.when(f == 0)
    def _():
        acc_ref[...] = jnp.zeros(acc_ref.shape, F32)

    wgb_ref[...] = wg_ref[0].astype(BF16)
    wub_ref[...] = wu_ref[0].astype(BF16)
    wdb_ref[...] = wd_ref[0].astype(BF16)
    for r0 in range(0, rows, rb):
        x = x_ref[0, r0:r0 + rb, :]
        a = jnp.dot(x, wgb_ref[...], preferred_element_type=F32)
        u = jnp.dot(x, wub_ref[...], preferred_element_type=F32)
        hmid = (a * (1.0 / (1.0 + jnp.exp(-a))) * u).astype(BF16)
        acc_ref[r0:r0 + rb, :] += jnp.dot(hmid, wdb_ref[...], preferred_element_type=F32)

    @pl.when(f == pl.num_programs(1) - 1)
    def _():
        y_ref[0] = acc_ref[...].astype(BF16)


def _experts(xg, w_gate, w_up, w_down, *, tf):
    E, rows, D = xg.shape
    F = w_gate.shape[-1]
    return pl.pallas_call(
        _expert_kernel,
        grid=(E, F // tf),
        in_specs=[
            pl.BlockSpec((1, rows, D), lambda e, f: (e, 0, 0)),
            pl.BlockSpec((1, D, tf), lambda e, f: (e, 0, f)),
            pl.BlockSpec((1, D, tf), lambda e, f: (e, 0, f)),
            pl.BlockSpec((1, tf, D), lambda e, f: (e, f, 0)),
        ],
        out_specs=pl.BlockSpec((1, rows, D), lambda e, f: (e, 0, 0)),
        out_shape=jax.ShapeDtypeStruct((E, rows, D), BF16),
        scratch_shapes=[pltpu.VMEM((rows, D), F32), pltpu.VMEM((D, tf), BF16), pltpu.VMEM((D, tf), BF16),
                        pltpu.VMEM((tf, D), BF16)],
        compiler_params=pltpu.CompilerParams(
            dimension_semantics=("arbitrary", "arbitrary"), vmem_limit_bytes=VMEM_LIMIT_BYTES),
        name="experts",
    )(xg, w_gate, w_up, w_down)


def _combine_kernel(off_ref, rank_ref, gate_ref, y_ref, h_ref, g2_ref, b2_ref, o_ref, *, cap, win):
    tb, E = rank_ref.shape[1], rank_ref.shape[2]
    b, j = pl.program_id(0), pl.program_id(1)
    starts, n_pass = _slot_windows(off_ref, (b * (pl.num_programs(1) + 1) + j) * E, E, cap, win)
    per_tile = LANES // win
    lane = lax.broadcasted_iota(jnp.int32, (tb, LANES), 1)

    def one_pass(p, first):
        tiles, ys = [], []
        for t in range(E // per_tile):
            tgt = rk = gt = lo_v = None
            for u in range(per_tile):
                e = t * per_tile + u
                lo = starts[e] + p * win
                ws = pl.multiple_of(jnp.minimum(lo, cap - win), SLOT_ALIGN)
                ys.append(y_ref[e, 0, pl.ds(ws, win), :])
                r_u = rank_ref[0, :, e:e + 1]
                g_u = gate_ref[0, :, e:e + 1]
                t_u = ws + lane - u * win
                if u == 0:
                    tgt, rk, gt, lo_v = t_u, r_u, g_u, lo
                else:
                    here = lane >= u * win
                    tgt = jnp.where(here, t_u, tgt)
                    rk = jnp.where(here, r_u, rk)
                    gt = jnp.where(here, g_u, gt)
                    lo_v = jnp.where(here, lo, lo_v)
            hit = rk == tgt
            if not first:
                hit = jnp.logical_and(hit, tgt >= lo_v)
            tiles.append(jnp.where(hit, gt, 0.0).astype(BF16))
        gates = jnp.concatenate(tiles, axis=1)
        return jnp.dot(gates, jnp.concatenate(ys, axis=0), preferred_element_type=F32)

    o_ref[0] = DEEPNORM_ALPHA * h_ref[0] + one_pass(0, True)

    def extra(p, carry):
        o_ref[0] += one_pass(p, False)
        return carry

    lax.fori_loop(1, n_pass, extra, 0)
    o_ref[0] = _layer_norm(o_ref[0], g2_ref[...], b2_ref[...])


def _combine(off_flat, rank_t, gate_t, y, h1, g2, b2, *, tb):
    B, S, E = rank_t.shape
    cap, D = y.shape[2], y.shape[3]
    win = min(SLOT_WINDOW, cap)
    assert cap % SLOT_ALIGN == 0 and win % SLOT_ALIGN == 0 and LANES % win == 0 and E % (LANES // win) == 0
    full = lambda a: pl.BlockSpec(a.shape, lambda b, j, off: (0,) * a.ndim)
    return pl.pallas_call(
        functools.partial(_combine_kernel, cap=cap, win=win),
        grid_spec=pltpu.PrefetchScalarGridSpec(
            num_scalar_prefetch=1,
            grid=(B, S // tb),
            in_specs=[
                pl.BlockSpec((1, tb, E), lambda b, j, off: (b, j, 0)),
                pl.BlockSpec((1, tb, E), lambda b, j, off: (b, j, 0)),
                pl.BlockSpec((E, 1, cap, D), lambda b, j, off: (0, b, 0, 0)),
                pl.BlockSpec((1, tb, D), lambda b, j, off: (b, j, 0)),
                full(g2), full(b2),
            ],
            out_specs=pl.BlockSpec((1, tb, D), lambda b, j, off: (b, j, 0)),
        ),
        out_shape=jax.ShapeDtypeStruct((B, S, D), F32),
        compiler_params=pltpu.CompilerParams(
            dimension_semantics=("arbitrary", "arbitrary"), vmem_limit_bytes=VMEM_LIMIT_BYTES),
        name="combine",
    )(off_flat, rank_t, gate_t, y, h1, g2, b2)


def _tile(n, target):
    t = min(n, target)
    assert n % t == 0, (n, t)
    return t


def kernel(x, positions, emb_ln_g, emb_ln_b, w_in, q_norm_g, w_qb, kv_norm_g, w_kvb, conv_w, conv_b,
           conv_ln_g, conv_ln_b, w_o, ln1_g, ln1_b, w_router, w_gate, w_up, w_down, ln2_g, ln2_b):
    B, S, D = x.shape
    T = B * S
    H = MLA_HEADS
    q_rank = q_norm_g.shape[-1]
    kv_rank = kv_norm_g.shape[-1]
    conv_ch = conv_w.shape[-1]
    qk_dim = QK_NOPE_DIM + QK_ROPE_DIM
    cap = CAPACITY_FACTOR * S // N_EXPERTS
    assert w_in.shape[0] == DEPTH == 1
    row = lambda a: a.reshape(1, -1)

    wi = w_in[0]
    c1, c2, c3 = q_rank, q_rank + kv_rank, q_rank + kv_rank + QK_ROPE_DIM
    tail = LANES - QK_NOPE_DIM - QK_ROPE_DIM
    kr_cols = jnp.pad(wi[:, c2:c3], ((0, 0), (QK_NOPE_DIM, tail)))
    w_in_r = jnp.concatenate(
        [wi[:, :c2], wi[:, c3:c3 + conv_ch], wi[:, c3 + conv_ch:], kr_cols], axis=1).astype(BF16)
    w_qb_pad = jnp.pad(w_qb[0].reshape(q_rank, H, qk_dim),
                       ((0, 0), (0, 0), (0, HEAD_PAD - qk_dim))).reshape(q_rank, H * HEAD_PAD).astype(BF16)
    wkv = w_kvb[0].reshape(kv_rank, H, QK_NOPE_DIM + V_HEAD_DIM)
    wk_pad = jnp.pad(wkv[:, :, :QK_NOPE_DIM], ((0, 0), (0, 0), (0, HEAD_PAD - QK_NOPE_DIM)))
    w_kv_r = jnp.concatenate([wk_pad.reshape(kv_rank, H * HEAD_PAD),
                              wkv[:, :, QK_NOPE_DIM:].reshape(kv_rank, H * V_HEAD_DIM)], axis=1).astype(BF16)
    half = QK_ROPE_DIM // 2
    inv_freq = (ROPE_THETA ** (-jnp.arange(half, dtype=F32) / half)).reshape(half, 1)

    x2d = x.reshape(T, D)
    tm = _tile(S, 512)
    q, k, v, hc = _in_proj(x2d, positions.reshape(1, T), inv_freq, row(emb_ln_g), row(emb_ln_b), w_in_r,
                           row(q_norm_g[0]), w_qb_pad, row(kv_norm_g[0]), w_kv_r, tm=tm)
    cv = _conv(hc.reshape(B, S, conv_ch), conv_w[0], row(conv_b[0]), row(conv_ln_g[0]), row(conv_ln_b[0]))
    att = _attn(q.reshape(B, S, H * HEAD_PAD), k.reshape(B, S, H * HEAD_PAD),
                v.reshape(B, S, H * V_HEAD_DIM), tq=_tile(S, 512))
    wr_hi = w_router[0].astype(BF16)
    w_r2 = jnp.concatenate([wr_hi, (w_router[0] - wr_hi.astype(F32)).astype(BF16)], axis=1)
    h1, h1b, lg = _out_proj(x2d, att.reshape(T, H * V_HEAD_DIM), cv.reshape(T, conv_ch), row(emb_ln_g),
                            row(emb_ln_b), w_o[0].astype(BF16), row(ln1_g[0]), row(ln1_b[0]),
                            w_r2, tm=_tile(S, 1024))
    tb = _tile(S, ROUTE_TOKEN_BLOCK)
    rank, gate, off = _route(jnp.swapaxes(lg.reshape(B, S, 2 * N_EXPERTS), 1, 2), cap=cap, tb=tb)
    off_flat = jnp.swapaxes(off[:, :, :S // tb + 1], 1, 2).reshape(-1)
    xg = _dispatch(off_flat, rank, h1b.reshape(B, S, D), cap=cap, tb=tb)
    y = _experts(xg.reshape(N_EXPERTS, B * cap, D), w_gate[0], w_up[0], w_down[0],
                 tf=_tile(w_gate.shape[-1], 512)).reshape(N_EXPERTS, B, cap, D)
    out = _combine(off_flat, jnp.swapaxes(rank, 1, 2), jnp.swapaxes(gate, 1, 2), y, h1.reshape(B, S, D),
                   row(ln2_g[0]), row(ln2_b[0]), tb=tb)
    return out
```

```python
import functools

import jax
import jax.numpy as jnp
from jax import lax
from jax.experimental import pallas as pl
from jax.experimental.pallas import tpu as pltpu

F32 = jnp.float32
BF16 = jnp.bfloat16

MLA_HEADS = 8
QK_NOPE_DIM = 64
QK_ROPE_DIM = 32
V_HEAD_DIM = 64
CONV_WIDTH = 31
ROPE_THETA = 10000.0
N_EXPERTS = 16
CAPACITY_FACTOR = 2
DEPTH = 1
DEEPNORM_ALPHA = (2.0 * DEPTH) ** 0.25
LN_EPS = 1e-5
RMS_EPS = 1e-6

LANES = 128
SUBLANES = 8
HEAD_PAD = LANES
VMEM_LIMIT_BYTES = 56 * 1024 * 1024
TOKEN_SUB_TILE = 256

TOPK_SEARCH_STEPS = 36
TOPK_SEARCH_FLOOR = 1e-30


def _layer_norm(x, g, b):
    mu = jnp.mean(x, axis=-1, keepdims=True)
    xc = x - mu
    var = jnp.mean(xc * xc, axis=-1, keepdims=True)
    return xc * lax.rsqrt(var + LN_EPS) * g + b


def _rms_norm(x, g):
    return x * lax.rsqrt(jnp.mean(x * x, axis=-1, keepdims=True) + RMS_EPS) * g


def _rope_tile(x, cos_t, sin_t, lane):
    half = QK_ROPE_DIM // 2
    fwd = pltpu.roll(x, LANES - half, 1)
    bwd = pltpu.roll(x, half, 1)
    partner = jnp.where(lane < QK_NOPE_DIM + half, fwd, bwd)
    return x * cos_t + partner * sin_t


def _in_proj_kernel(x_ref, pos_ref, invf_ref, lng_ref, lnb_ref, win_ref, qg_ref, wqb_ref,
                    kvg_ref, wkv_ref, q_ref, k_ref, v_ref, hc_ref, *, q_rank, kv_rank, conv_ch):
    tm = x_ref.shape[0]
    sub = min(tm, TOKEN_SUB_TILE)
    c1 = q_rank
    c2 = c1 + kv_rank
    c3 = c2 + conv_ch
    c4 = c3 + conv_ch
    tail = LANES - QK_NOPE_DIM - QK_ROPE_DIM
    nk = MLA_HEADS * HEAD_PAD
    scale = (QK_NOPE_DIM + QK_ROPE_DIM) ** -0.5
    ones = jnp.ones((QK_NOPE_DIM, sub), F32)
    zeros = jnp.zeros((QK_NOPE_DIM, sub), F32)
    lane = lax.broadcasted_iota(jnp.int32, (sub, LANES), 1)
    for r0 in range(0, tm, sub):
        rows = slice(r0, r0 + sub)
        h = _layer_norm(x_ref[rows, :], lng_ref[...], lnb_ref[...])
        proj = jnp.dot(h.astype(BF16), win_ref[...], preferred_element_type=F32)
        cq = proj[:, :c1]
        ckv = proj[:, c1:c2]
        a = proj[:, c2:c3]
        g = proj[:, c3:c4]
        kr = proj[:, c4:c4 + LANES]

        ang = pos_ref[:, rows].astype(F32) * invf_ref[...]
        cos = jnp.cos(ang)
        sin = jnp.sin(ang)
        cos_t = jnp.concatenate([ones, cos, cos, ones[:tail]], axis=0).T
        sin_t = jnp.concatenate([zeros, -sin, sin, zeros[:tail]], axis=0).T

        cqn = _rms_norm(cq, qg_ref[...])
        q = jnp.dot(cqn.astype(BF16), wqb_ref[...], preferred_element_type=F32) * scale
        ckvn = _rms_norm(ckv, kvg_ref[...])
        kv = jnp.dot(ckvn.astype(BF16), wkv_ref[...], preferred_element_type=F32)
        k_pe = _rope_tile(kr, cos_t, sin_t, lane)
        for hd in range(MLA_HEADS):
            sl = slice(hd * HEAD_PAD, (hd + 1) * HEAD_PAD)
            q_ref[rows, sl] = _rope_tile(q[:, sl], cos_t, sin_t, lane).astype(BF16)
            k_ref[rows, sl] = (kv[:, sl] + k_pe).astype(BF16)
        v_ref[rows, :] = kv[:, nk:].astype(BF16)
        hc_ref[rows, :] = a * (1.0 / (1.0 + jnp.exp(-g)))


def _in_proj(x2d, pos_row, invf_col, lng, lnb, w_in_r, qg, w_qb_pad, kvg, w_kv_r, *, tm):
    T, D = x2d.shape
    q_rank = w_qb_pad.shape[0]
    kv_rank = w_kv_r.shape[0]
    conv_ch = (w_in_r.shape[1] - q_rank - kv_rank - LANES) // 2
    nk = MLA_HEADS * HEAD_PAD
    nv = MLA_HEADS * V_HEAD_DIM
    full = lambda a: pl.BlockSpec(a.shape, lambda i: (0,) * a.ndim)
    return pl.pallas_call(
        functools.partial(_in_proj_kernel, q_rank=q_rank, kv_rank=kv_rank, conv_ch=conv_ch),
        grid=(T // tm,),
        in_specs=[
            pl.BlockSpec((tm, D), lambda i: (i, 0)),
            pl.BlockSpec((1, tm), lambda i: (0, i)),
            full(invf_col), full(lng), full(lnb), full(w_in_r), full(qg), full(w_qb_pad),
            full(kvg), full(w_kv_r),
        ],
        out_specs=[
            pl.BlockSpec((tm, nk), lambda i: (i, 0)),
            pl.BlockSpec((tm, nk), lambda i: (i, 0)),
            pl.BlockSpec((tm, nv), lambda i: (i, 0)),
            pl.BlockSpec((tm, conv_ch), lambda i: (i, 0)),
        ],
        out_shape=[
            jax.ShapeDtypeStruct((T, nk), BF16),
            jax.ShapeDtypeStruct((T, nk), BF16),
            jax.ShapeDtypeStruct((T, nv), BF16),
            jax.ShapeDtypeStruct((T, conv_ch), F32),
        ],
        compiler_params=pltpu.CompilerParams(
            dimension_semantics=("arbitrary",), vmem_limit_bytes=VMEM_LIMIT_BYTES),
        name="in_proj",
    )(x2d, pos_row, invf_col, lng, lnb, w_in_r, qg, w_qb_pad, kvg, w_kv_r)


CONV_PAD_ROWS = 16
CONV_ROW_CHUNK = 128
CONV_WINDOW = CONV_ROW_CHUNK + 2 * CONV_PAD_ROWS


def _conv_kernel(hc_ref, cw_ref, cb_ref, g_ref, b_ref, o_ref, pad_ref):
    S, C = hc_ref.shape[1], hc_ref.shape[2]
    pad_ref[0:CONV_PAD_ROWS, :] = jnp.zeros((CONV_PAD_ROWS, C), F32)
    pad_ref[CONV_PAD_ROWS + S:, :] = jnp.zeros((CONV_PAD_ROWS, C), F32)
    pad_ref[CONV_PAD_ROWS:CONV_PAD_ROWS + S, :] = hc_ref[0]
    first = CONV_PAD_ROWS - CONV_WIDTH // 2

    def chunk(r, carry):
        base = pl.multiple_of(r * CONV_ROW_CHUNK, CONV_ROW_CHUNK)
        cols = []
        for c0 in range(0, C, LANES):
            win = pad_ref[pl.ds(base, CONV_WINDOW), c0:c0 + LANES]
            acc = jnp.zeros((CONV_ROW_CHUNK, LANES), F32)
            for res in range(SUBLANES):
                shifted = win if res == 0 else pltpu.roll(win, CONV_WINDOW - res, 0)
                for j in range(CONV_WIDTH):
                    off = first + j
                    if off % SUBLANES == res:
                        a0 = off - res
                        acc = acc + shifted[a0:a0 + CONV_ROW_CHUNK, :] * cw_ref[j:j + 1, c0:c0 + LANES]
            cols.append(acc)
        y = _layer_norm(jnp.concatenate(cols, axis=1) + cb_ref[...], g_ref[...], b_ref[...])
        o_ref[0, pl.ds(base, CONV_ROW_CHUNK), :] = (y * (1.0 / (1.0 + jnp.exp(-y)))).astype(BF16)
        return carry

    lax.fori_loop(0, S // CONV_ROW_CHUNK, chunk, 0)


def _conv(hc, cw, cb, g, b):
    B, S, C = hc.shape
    full = lambda a: pl.BlockSpec(a.shape, lambda i: (0,) * a.ndim)
    return pl.pallas_call(
        _conv_kernel,
        grid=(B,),
        in_specs=[pl.BlockSpec((1, S, C), lambda i: (i, 0, 0)), full(cw), full(cb), full(g), full(b)],
        out_specs=pl.BlockSpec((1, S, C), lambda i: (i, 0, 0)),
        out_shape=jax.ShapeDtypeStruct((B, S, C), BF16),
        scratch_shapes=[pltpu.VMEM((S + 2 * CONV_PAD_ROWS, C), F32)],
        compiler_params=pltpu.CompilerParams(
            dimension_semantics=("arbitrary",), vmem_limit_bytes=VMEM_LIMIT_BYTES),
        name="conv",
    )(hc, cw, cb, g, b)


HEADS_PER_TILE = LANES // V_HEAD_DIM
ATTN_HEADS_PER_STEP = 4


ATTN_SLOTS = 2


def _attn_kernel(q_ref, k_ref, v_ref, o_ref, s_ref, p_ref, *, tq):
    S = q_ref.shape[1]
    lane = lax.broadcasted_iota(jnp.int32, (tq, LANES), 1)
    item = 0
    for pr in range(v_ref.shape[2] // LANES):
        v = v_ref[0, :, pr * LANES:(pr + 1) * LANES]
        for qi in range(S // tq):
            rows = slice(qi * tq, (qi + 1) * tq)
            outs = []
            for hh in range(HEADS_PER_TILE):
                slot = item % ATTN_SLOTS
                item += 1
                c0 = (pr * HEADS_PER_TILE + hh) * HEAD_PAD
                s_ref[slot] = lax.dot_general(q_ref[0, rows, c0:c0 + HEAD_PAD], k_ref[0, :, c0:c0 + HEAD_PAD],
                                              (((1,), (1,)), ((), ())), preferred_element_type=F32)
                s = s_ref[slot]
                p = jnp.exp(s - jnp.max(s, axis=-1, keepdims=True))
                l = jnp.sum(p, axis=-1, keepdims=True)
                p_ref[slot] = p.astype(BF16)
                o = jnp.dot(p_ref[slot], v, preferred_element_type=F32)
                outs.append(o * (1.0 / l))
            o_ref[0, rows, pr * LANES:(pr + 1) * LANES] = jnp.where(lane < V_HEAD_DIM, outs[0], outs[1]).astype(BF16)


def _attn(q, k, v, *, tq):
    B, S, _ = q.shape
    qw = ATTN_HEADS_PER_STEP * HEAD_PAD
    vw = ATTN_HEADS_PER_STEP * V_HEAD_DIM
    return pl.pallas_call(
        functools.partial(_attn_kernel, tq=tq),
        grid=(B, MLA_HEADS // ATTN_HEADS_PER_STEP),
        in_specs=[
            pl.BlockSpec((1, S, qw), lambda b, h: (b, 0, h)),
            pl.BlockSpec((1, S, qw), lambda b, h: (b, 0, h)),
            pl.BlockSpec((1, S, vw), lambda b, h: (b, 0, h)),
        ],
        out_specs=pl.BlockSpec((1, S, vw), lambda b, h: (b, 0, h)),
        out_shape=jax.ShapeDtypeStruct((B, S, MLA_HEADS * V_HEAD_DIM), BF16),
        scratch_shapes=[pltpu.VMEM((ATTN_SLOTS, tq, S), F32), pltpu.VMEM((ATTN_SLOTS, tq, S), BF16)],
        compiler_params=pltpu.CompilerParams(
            dimension_semantics=("arbitrary", "arbitrary"), vmem_limit_bytes=62 * 1024 * 1024),
        name="attn",
    )(q, k, v)


def _out_proj_kernel(x_ref, att_ref, cv_ref, lng_ref, lnb_ref, wo_ref, g1_ref, b1_ref, wr_ref,
                     h1_ref, h1b_ref, lg_ref):
    na = att_ref.shape[1]
    tm = x_ref.shape[0]
    sub = min(tm, TOKEN_SUB_TILE)
    for r0 in range(0, tm, sub):
        rows = slice(r0, r0 + sub)
        h0 = _layer_norm(x_ref[rows, :], lng_ref[...], lnb_ref[...])
        mix = jnp.dot(att_ref[rows, :], wo_ref[:na, :], preferred_element_type=F32)
        mix = mix + jnp.dot(cv_ref[rows, :], wo_ref[na:, :], preferred_element_type=F32)
        h1 = _layer_norm(DEEPNORM_ALPHA * h0 + mix, g1_ref[...], b1_ref[...])
        h1_ref[rows, :] = h1
        h1b = h1.astype(BF16)
        h1b_ref[rows, :] = h1b
        lg_ref[rows, :] = jnp.dot(h1b, wr_ref[...], preferred_element_type=F32)


def _out_proj(x2d, att, cv, lng, lnb, w_o, g1, b1, w_r2, *, tm):
    T, D = x2d.shape
    full = lambda a: pl.BlockSpec(a.shape, lambda i: (0,) * a.ndim)
    return pl.pallas_call(
        _out_proj_kernel,
        grid=(T // tm,),
        in_specs=[
            pl.BlockSpec((tm, D), lambda i: (i, 0)),
            pl.BlockSpec((tm, att.shape[1]), lambda i: (i, 0)),
            pl.BlockSpec((tm, cv.shape[1]), lambda i: (i, 0)),
            full(lng), full(lnb), full(w_o), full(g1), full(b1), full(w_r2),
        ],
        out_specs=[
            pl.BlockSpec((tm, D), lambda i: (i, 0)),
            pl.BlockSpec((tm, D), lambda i: (i, 0)),
            pl.BlockSpec((tm, w_r2.shape[1]), lambda i: (i, 0)),
        ],
        out_shape=[
            jax.ShapeDtypeStruct((T, D), F32),
            jax.ShapeDtypeStruct((T, D), BF16),
            jax.ShapeDtypeStruct((T, w_r2.shape[1]), F32),
        ],
        compiler_params=pltpu.CompilerParams(
            dimension_semantics=("arbitrary",), vmem_limit_bytes=VMEM_LIMIT_BYTES),
        name="out_proj",
    )(x2d, att, cv, lng, lnb, w_o, g1, b1, w_r2)


def _route_kernel(lt_ref, rank_ref, gate_ref, off_ref, *, cap, tb):
    B, E, S = rank_ref.shape
    lt = lt_ref[:, :E, :] + lt_ref[:, E:, :]
    ex = jnp.exp(lt - jnp.max(lt, axis=1, keepdims=True))
    aff = ex / jnp.sum(ex, axis=1, keepdims=True)
    v = aff.reshape(B * E, S)
    rows = B * E
    kf = float(cap)

    def step(_, carry):
        lo, hi = carry
        mid = jnp.sqrt(jnp.maximum(lo, TOPK_SEARCH_FLOOR)) * jnp.sqrt(hi)
        cnt = jnp.sum(jnp.where(v >= mid, 1.0, 0.0), axis=1, keepdims=True)
        ge = cnt >= kf
        return jnp.where(ge, mid, lo), jnp.where(ge, hi, mid)

    lo, hi = lax.fori_loop(0, TOPK_SEARCH_STEPS, step,
                           (jnp.zeros((rows, 1), F32), jnp.full((rows, 1), 2.0, F32)))
    above = v >= hi
    tie = jnp.logical_and(v >= lo, jnp.logical_not(above))
    stacked = jnp.concatenate([jnp.where(above, 1.0, 0.0), jnp.where(tie, 1.0, 0.0)], axis=0)
    upper = jnp.where(lax.broadcasted_iota(jnp.int32, (S, S), 0) <= lax.broadcasted_iota(jnp.int32, (S, S), 1),
                      1.0, 0.0).astype(BF16)
    pc = jnp.dot(stacked.astype(BF16), upper, preferred_element_type=F32)
    pa = pc[:rows]
    pt = pc[rows:]
    need = kf - jnp.sum(jnp.where(above, 1.0, 0.0), axis=1, keepdims=True)
    sel = jnp.logical_or(above, jnp.logical_and(tie, pt <= need))
    taken = pa + jnp.minimum(pt, need)
    rank_ref[...] = jnp.where(sel, taken - 1.0, -1.0).astype(jnp.int32).reshape(B, E, S)
    gate_ref[...] = jnp.where(sel, v, 0.0).reshape(B, E, S)
    pick = jnp.where(lax.broadcasted_iota(jnp.int32, (S, LANES), 0) + 1
                     == lax.broadcasted_iota(jnp.int32, (S, LANES), 1) * tb, 1.0, 0.0).astype(BF16)
    off = jnp.dot(taken.astype(BF16), pick, preferred_element_type=F32)
    off_ref[...] = off.astype(jnp.int32).reshape(B, E, LANES)


def _route(lt2, *, cap, tb):
    B, E2, S = lt2.shape
    E = E2 // 2
    assert cap <= 256 and S // tb < LANES
    spec = pl.BlockSpec((B, E, S), lambda i: (0, 0, 0))
    ospec = pl.BlockSpec((B, E, LANES), lambda i: (0, 0, 0))
    return pl.pallas_call(
        functools.partial(_route_kernel, cap=cap, tb=tb),
        grid=(1,),
        in_specs=[pl.BlockSpec((B, E2, S), lambda i: (0, 0, 0))],
        out_specs=[spec, spec, ospec],
        out_shape=[jax.ShapeDtypeStruct((B, E, S), jnp.int32), jax.ShapeDtypeStruct((B, E, S), F32),
                   jax.ShapeDtypeStruct((B, E, LANES), jnp.int32)],
        compiler_params=pltpu.CompilerParams(
            dimension_semantics=("arbitrary",), vmem_limit_bytes=VMEM_LIMIT_BYTES),
        name="route",
    )(lt2)


ROUTE_TOKEN_BLOCK = 256
ROUTE_BLOCKS_PER_STEP = 2
SLOT_WINDOW = 64
SLOT_ALIGN = 16


def _slot_windows(off_ref, base, n_experts, cap, win):
    starts = []
    n_pass = jnp.int32(1)
    for e in range(n_experts):
        off = off_ref[base + e]
        end = off_ref[base + n_experts + e]
        start = jnp.minimum((off // SLOT_ALIGN) * SLOT_ALIGN, cap - win)
        starts.append(start)
        n_pass = jnp.maximum(n_pass, (end - start + (win - 1)) // win)
    return starts, n_pass


def _dispatch_kernel(off_ref, rank_ref, h_ref, xg_ref, *, cap, win, tb):
    E = rank_ref.shape[1]
    nsb = rank_ref.shape[2] // tb
    b, j = pl.program_id(0), pl.program_id(1)
    n_bounds = pl.num_programs(1) * nsb + 1

    @pl.when(j == 0)
    def _():
        xg_ref[...] = jnp.zeros(xg_ref.shape, BF16)

    sub = lax.broadcasted_iota(jnp.int32, (win, tb), 0)

    def one_pass(sb, starts, p, first):
        cols = slice(sb * tb, (sb + 1) * tb)
        pieces, rows = [], []
        for e in range(E):
            lo = starts[e] + p * win
            ws = pl.multiple_of(jnp.minimum(lo, cap - win), SLOT_ALIGN)
            tgt = ws + sub
            hit = rank_ref[0, e:e + 1, cols] == tgt
            if not first:
                hit = jnp.logical_and(hit, tgt >= lo)
            pieces.append(jnp.where(hit, 1.0, 0.0).astype(BF16))
            rows.append(ws)
        onehot = jnp.concatenate(pieces, axis=0)
        got = jnp.dot(onehot, h_ref[0, cols, :], preferred_element_type=F32).astype(BF16)
        for e in range(E):
            xg_ref[e, 0, pl.ds(rows[e], win), :] += got[e * win:(e + 1) * win, :]

    plans = []
    for sb in range(nsb):
        starts, n_pass = _slot_windows(off_ref, (b * n_bounds + j * nsb + sb) * E, E, cap, win)
        plans.append((starts, n_pass))
        one_pass(sb, starts, 0, True)
    for sb in range(nsb):
        starts, n_pass = plans[sb]

        def extra(p, carry, sb=sb, starts=starts):
            one_pass(sb, starts, p, False)
            return carry

        lax.fori_loop(1, n_pass, extra, 0)


def _dispatch(off_flat, rank, h1b, *, cap, tb, nsb):
    B, E, S = rank.shape
    D = h1b.shape[-1]
    win = min(SLOT_WINDOW, cap)
    ts = tb * nsb
    assert cap % SLOT_ALIGN == 0 and win % SLOT_ALIGN == 0 and S % ts == 0
    return pl.pallas_call(
        functools.partial(_dispatch_kernel, cap=cap, win=win, tb=tb),
        grid_spec=pltpu.PrefetchScalarGridSpec(
            num_scalar_prefetch=1,
            grid=(B, S // ts),
            in_specs=[pl.BlockSpec((1, E, ts), lambda b, j, off: (b, 0, j)),
                      pl.BlockSpec((1, ts, D), lambda b, j, off: (b, j, 0))],
            out_specs=pl.BlockSpec((E, 1, cap, D), lambda b, j, off: (0, b, 0, 0)),
        ),
        out_shape=jax.ShapeDtypeStruct((E, B, cap, D), BF16),
        compiler_params=pltpu.CompilerParams(
            dimension_semantics=("arbitrary", "arbitrary"), vmem_limit_bytes=VMEM_LIMIT_BYTES),
        name="dispatch",
    )(off_flat, rank, h1b)


EXPERT_ROW_BLOCK = 512


def _expert_kernel(x_ref, wg_ref, wu_ref, wd_ref, y_ref, acc_ref, wgb_ref, wub_ref, wdb_ref, *, n_chunks):
    f = pl.program_id(1)
    rows = x_ref.shape[1]
    rb = min(rows, EXPERT_ROW_BLOCK)
    last = n_chunks - 1

    def chunk(first, final):
        wgb_ref[...] = wg_ref[0].astype(BF16)
        wub_ref[...] = wu_ref[0].astype(BF16)
        wdb_ref[...] = wd_ref[0].astype(BF16)
        for r0 in range(0, rows, rb):
            x = x_ref[0, r0:r0 + rb, :]
            a = jnp.dot(x, wgb_ref[...], preferred_element_type=F32)
            u = jnp.dot(x, wub_ref[...], preferred_element_type=F32)
            hmid = (a * (1.0 / (1.0 + jnp.exp(-a))) * u).astype(BF16)
            part = jnp.dot(hmid, wdb_ref[...], preferred_element_type=F32)
            if not first:
                part = acc_ref[r0:r0 + rb, :] + part
            if final:
                y_ref[0, r0:r0 + rb, :] = part.astype(BF16)
            else:
                acc_ref[r0:r0 + rb, :] = part

    if n_chunks == 1:
        chunk(True, True)
    else:
        pl.when(f == 0)(lambda: chunk(True, False))
        pl.when(jnp.logical_and(f > 0, f < last))(lambda: chunk(False, False))
        pl.when(f == last)(lambda: chunk(False, True))


def _experts(xg, w_gate, w_up, w_down, *, tf):
    E, rows, D = xg.shape
    F = w_gate.shape[-1]
    return pl.pallas_call(
        functools.partial(_expert_kernel, n_chunks=F // tf),
        grid=(E, F // tf),
        in_specs=[
            pl.BlockSpec((1, rows, D), lambda e, f: (e, 0, 0)),
            pl.BlockSpec((1, D, tf), lambda e, f: (e, 0, f)),
            pl.BlockSpec((1, D, tf), lambda e, f: (e, 0, f)),
            pl.BlockSpec((1, tf, D), lambda e, f: (e, f, 0)),
        ],
        out_specs=pl.BlockSpec((1, rows, D), lambda e, f: (e, 0, 0)),
        out_shape=jax.ShapeDtypeStruct((E, rows, D), BF16),
        scratch_shapes=[pltpu.VMEM((rows, D), F32), pltpu.VMEM((D, tf), BF16), pltpu.VMEM((D, tf), BF16),
                        pltpu.VMEM((tf, D), BF16)],
        compiler_params=pltpu.CompilerParams(
            dimension_semantics=("arbitrary", "arbitrary"), vmem_limit_bytes=VMEM_LIMIT_BYTES),
        name="experts",
    )(xg, w_gate, w_up, w_down)


def _combine_kernel(off_ref, rank_ref, gate_ref, y_ref, h_ref, g2_ref, b2_ref, o_ref, *, cap, win, tb):
    E = rank_ref.shape[2]
    nsb = rank_ref.shape[1] // tb
    b, j = pl.program_id(0), pl.program_id(1)
    n_bounds = pl.num_programs(1) * nsb + 1
    per_tile = LANES // win
    lane = lax.broadcasted_iota(jnp.int32, (1, LANES), 1)
    first_lane = lax.broadcasted_iota(jnp.int32, (E, E * win), 0) * win
    col = lax.broadcasted_iota(jnp.int32, (E, E * win), 1)
    spread = jnp.where(jnp.logical_and(col >= first_lane, col < first_lane + win), 1.0, 0.0).astype(BF16)

    def one_pass(sb, starts, p, first):
        rows = slice(sb * tb, (sb + 1) * tb)
        rk = jnp.dot(rank_ref[0, rows, :].astype(F32).astype(BF16), spread, preferred_element_type=F32)
        gt = jnp.dot(gate_ref[0, rows, :].astype(BF16), spread, preferred_element_type=F32)
        tgts, los, ys = [], [], []
        for t in range(E // per_tile):
            tgt = lo_v = None
            for u in range(per_tile):
                e = t * per_tile + u
                lo = starts[e] + p * win
                ws = pl.multiple_of(jnp.minimum(lo, cap - win), SLOT_ALIGN)
                ys.append(y_ref[e, 0, pl.ds(ws, win), :])
                t_u = ws + lane - u * win
                if u == 0:
                    tgt, lo_v = t_u, jnp.full((1, LANES), lo, jnp.int32)
                else:
                    here = lane >= u * win
                    tgt = jnp.where(here, t_u, tgt)
                    lo_v = jnp.where(here, lo, lo_v)
            tgts.append(tgt)
            los.append(lo_v)
        tgt = jnp.concatenate(tgts, axis=1)
        hit = rk == tgt.astype(F32)
        if not first:
            hit = jnp.logical_and(hit, tgt >= jnp.concatenate(los, axis=1))
        gates = jnp.where(hit, gt, 0.0).astype(BF16)
        return jnp.dot(gates, jnp.concatenate(ys, axis=0), preferred_element_type=F32)

    plans = []
    for sb in range(nsb):
        rows = slice(sb * tb, (sb + 1) * tb)
        starts, n_pass = _slot_windows(off_ref, (b * n_bounds + j * nsb + sb) * E, E, cap, win)
        plans.append((starts, n_pass))
        o_ref[0, rows, :] = DEEPNORM_ALPHA * h_ref[0, rows, :] + one_pass(sb, starts, 0, True)
    for sb in range(nsb):
        starts, n_pass = plans[sb]

        def extra(p, carry, sb=sb, starts=starts):
            o_ref[0, sb * tb:(sb + 1) * tb, :] += one_pass(sb, starts, p, False)
            return carry

        lax.fori_loop(1, n_pass, extra, 0)
    o_ref[0] = _layer_norm(o_ref[0], g2_ref[...], b2_ref[...])


def _combine(off_flat, rank_t, gate_t, y, h1, g2, b2, *, tb, nsb):
    B, S, E = rank_t.shape
    cap, D = y.shape[2], y.shape[3]
    win = min(SLOT_WINDOW, cap)
    ts = tb * nsb
    assert cap % SLOT_ALIGN == 0 and win % SLOT_ALIGN == 0 and LANES % win == 0 and E % (LANES // win) == 0
    assert S % ts == 0
    full = lambda a: pl.BlockSpec(a.shape, lambda b, j, off: (0,) * a.ndim)
    return pl.pallas_call(
        functools.partial(_combine_kernel, cap=cap, win=win, tb=tb),
        grid_spec=pltpu.PrefetchScalarGridSpec(
            num_scalar_prefetch=1,
            grid=(B, S // ts),
            in_specs=[
                pl.BlockSpec((1, ts, E), lambda b, j, off: (b, j, 0)),
                pl.BlockSpec((1, ts, E), lambda b, j, off: (b, j, 0)),
                pl.BlockSpec((E, 1, cap, D), lambda b, j, off: (0, b, 0, 0)),
                pl.BlockSpec((1, ts, D), lambda b, j, off: (b, j, 0)),
                full(g2), full(b2),
            ],
            out_specs=pl.BlockSpec((1, ts, D), lambda b, j, off: (b, j, 0)),
        ),
        out_shape=jax.ShapeDtypeStruct((B, S, D), F32),
        compiler_params=pltpu.CompilerParams(
            dimension_semantics=("arbitrary", "arbitrary"), vmem_limit_bytes=VMEM_LIMIT_BYTES),
        name="combine",
    )(off_flat, rank_t, gate_t, y, h1, g2, b2)


def _tile(n, target):
    t = min(n, target)
    assert n % t == 0, (n, t)
    return t


def kernel(x, positions, emb_ln_g, emb_ln_b, w_in, q_norm_g, w_qb, kv_norm_g, w_kvb, conv_w, conv_b,
           conv_ln_g, conv_ln_b, w_o, ln1_g, ln1_b, w_router, w_gate, w_up, w_down, ln2_g, ln2_b):
    B, S, D = x.shape
    T = B * S
    H = MLA_HEADS
    q_rank = q_norm_g.shape[-1]
    kv_rank = kv_norm_g.shape[-1]
    conv_ch = conv_w.shape[-1]
    qk_dim = QK_NOPE_DIM + QK_ROPE_DIM
    cap = CAPACITY_FACTOR * S // N_EXPERTS
    assert w_in.shape[0] == DEPTH == 1
    row = lambda a: a.reshape(1, -1)

    wi = w_in[0]
    c1, c2, c3 = q_rank, q_rank + kv_rank, q_rank + kv_rank + QK_ROPE_DIM
    tail = LANES - QK_NOPE_DIM - QK_ROPE_DIM
    kr_cols = jnp.pad(wi[:, c2:c3], ((0, 0), (QK_NOPE_DIM, tail)))
    w_in_r = jnp.concatenate(
        [wi[:, :c2], wi[:, c3:c3 + conv_ch], wi[:, c3 + conv_ch:], kr_cols], axis=1).astype(BF16)
    w_qb_pad = jnp.pad(w_qb[0].reshape(q_rank, H, qk_dim),
                       ((0, 0), (0, 0), (0, HEAD_PAD - qk_dim))).reshape(q_rank, H * HEAD_PAD).astype(BF16)
    wkv = w_kvb[0].reshape(kv_rank, H, QK_NOPE_DIM + V_HEAD_DIM)
    wk_pad = jnp.pad(wkv[:, :, :QK_NOPE_DIM], ((0, 0), (0, 0), (0, HEAD_PAD - QK_NOPE_DIM)))
    w_kv_r = jnp.concatenate([wk_pad.reshape(kv_rank, H * HEAD_PAD),
                              wkv[:, :, QK_NOPE_DIM:].reshape(kv_rank, H * V_HEAD_DIM)], axis=1).astype(BF16)
    half = QK_ROPE_DIM // 2
    inv_freq = (ROPE_THETA ** (-jnp.arange(half, dtype=F32) / half)).reshape(half, 1)

    x2d = x.reshape(T, D)
    tm = _tile(S, 512)
    q, k, v, hc = _in_proj(x2d, positions.reshape(1, T), inv_freq, row(emb_ln_g), row(emb_ln_b), w_in_r,
                           row(q_norm_g[0]), w_qb_pad, row(kv_norm_g[0]), w_kv_r, tm=tm)
    cv = _conv(hc.reshape(B, S, conv_ch), conv_w[0], row(conv_b[0]), row(conv_ln_g[0]), row(conv_ln_b[0]))
    att = _attn(q.reshape(B, S, H * HEAD_PAD), k.reshape(B, S, H * HEAD_PAD),
                v.reshape(B, S, H * V_HEAD_DIM), tq=_tile(S, 512))
    wr_hi = w_router[0].astype(BF16)
    w_r2 = jnp.concatenate([wr_hi, (w_router[0] - wr_hi.astype(F32)).astype(BF16)], axis=1)
    h1, h1b, lg = _out_proj(x2d, att.reshape(T, H * V_HEAD_DIM), cv.reshape(T, conv_ch), row(emb_ln_g),
                            row(emb_ln_b), w_o[0].astype(BF16), row(ln1_g[0]), row(ln1_b[0]),
                            w_r2, tm=_tile(S, 1024))
    tb = _tile(S, ROUTE_TOKEN_BLOCK)
    rank, gate, off = _route(jnp.swapaxes(lg.reshape(B, S, 2 * N_EXPERTS), 1, 2), cap=cap, tb=tb)
    off_flat = jnp.swapaxes(off[:, :, :S // tb + 1], 1, 2).reshape(-1)
    nsb = _tile(S // tb, ROUTE_BLOCKS_PER_STEP)
    xg = _dispatch(off_flat, rank, h1b.reshape(B, S, D), cap=cap, tb=tb, nsb=nsb)
    y = _experts(xg.reshape(N_EXPERTS, B * cap, D), w_gate[0], w_up[0], w_down[0],
                 tf=_tile(w_gate.shape[-1], 512)).reshape(N_EXPERTS, B, cap, D)
    out = _combine(off_flat, jnp.swapaxes(rank, 1, 2), jnp.swapaxes(gate, 1, 2), y, h1.reshape(B, S, D),
                   row(ln2_g[0]), row(ln2_b[0]), tb=tb, nsb=nsb)
    return out
```

```python
import functools

import jax
import jax.numpy as jnp
from jax import lax
from jax.experimental import pallas as pl
from jax.experimental.pallas import tpu as pltpu

F32 = jnp.float32
BF16 = jnp.bfloat16

MLA_HEADS = 8
QK_NOPE_DIM = 64
QK_ROPE_DIM = 32
V_HEAD_DIM = 64
CONV_WIDTH = 31
ROPE_THETA = 10000.0
N_EXPERTS = 16
CAPACITY_FACTOR = 2
DEPTH = 1
DEEPNORM_ALPHA = (2.0 * DEPTH) ** 0.25
LN_EPS = 1e-5
RMS_EPS = 1e-6

LANES = 128
SUBLANES = 8
HEAD_PAD = LANES
VMEM_LIMIT_BYTES = 56 * 1024 * 1024
TOKEN_SUB_TILE = 256

TOPK_SEARCH_STEPS = 36
TOPK_SEARCH_FLOOR = 1e-30


def _layer_norm(x, g, b):
    mu = jnp.mean(x, axis=-1, keepdims=True)
    xc = x - mu
    var = jnp.mean(xc * xc, axis=-1, keepdims=True)
    return xc * lax.rsqrt(var + LN_EPS) * g + b


def _rms_norm(x, g):
    return x * lax.rsqrt(jnp.mean(x * x, axis=-1, keepdims=True) + RMS_EPS) * g


def _rope_tile(x, cos_t, sin_t, lane):
    half = QK_ROPE_DIM // 2
    fwd = pltpu.roll(x, LANES - half, 1)
    bwd = pltpu.roll(x, half, 1)
    partner = jnp.where(lane < QK_NOPE_DIM + half, fwd, bwd)
    return x * cos_t + partner * sin_t


def _in_proj_kernel(x_ref, pos_ref, invf_ref, lng_ref, lnb_ref, win_ref, qg_ref, wqb_ref,
                    kvg_ref, wkv_ref, q_ref, k_ref, v_ref, hc_ref, *, q_rank, kv_rank, conv_ch):
    tm = x_ref.shape[0]
    sub = min(tm, TOKEN_SUB_TILE)
    c1 = q_rank
    c2 = c1 + kv_rank
    c3 = c2 + conv_ch
    c4 = c3 + conv_ch
    tail = LANES - QK_NOPE_DIM - QK_ROPE_DIM
    nk = MLA_HEADS * HEAD_PAD
    scale = (QK_NOPE_DIM + QK_ROPE_DIM) ** -0.5
    ones = jnp.ones((QK_NOPE_DIM, sub), F32)
    zeros = jnp.zeros((QK_NOPE_DIM, sub), F32)
    lane = lax.broadcasted_iota(jnp.int32, (sub, LANES), 1)
    for r0 in range(0, tm, sub):
        rows = slice(r0, r0 + sub)
        h = _layer_norm(x_ref[rows, :], lng_ref[...], lnb_ref[...])
        proj = jnp.dot(h.astype(BF16), win_ref[...], preferred_element_type=F32)
        cq = proj[:, :c1]
        ckv = proj[:, c1:c2]
        a = proj[:, c2:c3]
        g = proj[:, c3:c4]
        kr = proj[:, c4:c4 + LANES]

        ang = pos_ref[:, rows].astype(F32) * invf_ref[...]
        cos = jnp.cos(ang)
        sin = jnp.sin(ang)
        cos_t = jnp.concatenate([ones, cos, cos, ones[:tail]], axis=0).T
        sin_t = jnp.concatenate([zeros, -sin, sin, zeros[:tail]], axis=0).T

        cqn = _rms_norm(cq, qg_ref[...])
        q = jnp.dot(cqn.astype(BF16), wqb_ref[...], preferred_element_type=F32) * scale
        ckvn = _rms_norm(ckv, kvg_ref[...])
        kv = jnp.dot(ckvn.astype(BF16), wkv_ref[...], preferred_element_type=F32)
        k_pe = _rope_tile(kr, cos_t, sin_t, lane)
        for hd in range(MLA_HEADS):
            sl = slice(hd * HEAD_PAD, (hd + 1) * HEAD_PAD)
            q_ref[rows, sl] = _rope_tile(q[:, sl], cos_t, sin_t, lane).astype(BF16)
            k_ref[rows, sl] = (kv[:, sl] + k_pe).astype(BF16)
        v_ref[rows, :] = kv[:, nk:].astype(BF16)
        hc_ref[rows, :] = a * (1.0 / (1.0 + jnp.exp(-g)))


def _in_proj(x2d, pos_row, invf_col, lng, lnb, w_in_r, qg, w_qb_pad, kvg, w_kv_r, *, tm):
    T, D = x2d.shape
    q_rank = w_qb_pad.shape[0]
    kv_rank = w_kv_r.shape[0]
    conv_ch = (w_in_r.shape[1] - q_rank - kv_rank - LANES) // 2
    nk = MLA_HEADS * HEAD_PAD
    nv = MLA_HEADS * V_HEAD_DIM
    full = lambda a: pl.BlockSpec(a.shape, lambda i: (0,) * a.ndim)
    return pl.pallas_call(
        functools.partial(_in_proj_kernel, q_rank=q_rank, kv_rank=kv_rank, conv_ch=conv_ch),
        grid=(T // tm,),
        in_specs=[
            pl.BlockSpec((tm, D), lambda i: (i, 0)),
            pl.BlockSpec((1, tm), lambda i: (0, i)),
            full(invf_col), full(lng), full(lnb), full(w_in_r), full(qg), full(w_qb_pad),
            full(kvg), full(w_kv_r),
        ],
        out_specs=[
            pl.BlockSpec((tm, nk), lambda i: (i, 0)),
            pl.BlockSpec((tm, nk), lambda i: (i, 0)),
            pl.BlockSpec((tm, nv), lambda i: (i, 0)),
            pl.BlockSpec((tm, conv_ch), lambda i: (i, 0)),
        ],
        out_shape=[
            jax.ShapeDtypeStruct((T, nk), BF16),
            jax.ShapeDtypeStruct((T, nk), BF16),
            jax.ShapeDtypeStruct((T, nv), BF16),
            jax.ShapeDtypeStruct((T, conv_ch), F32),
        ],
        compiler_params=pltpu.CompilerParams(
            dimension_semantics=("arbitrary",), vmem_limit_bytes=VMEM_LIMIT_BYTES),
        name="in_proj",
    )(x2d, pos_row, invf_col, lng, lnb, w_in_r, qg, w_qb_pad, kvg, w_kv_r)


CONV_PAD_ROWS = 16
CONV_ROW_CHUNK = 128
CONV_WINDOW = CONV_ROW_CHUNK + 2 * CONV_PAD_ROWS


def _conv_kernel(hc_ref, cw_ref, cb_ref, g_ref, b_ref, o_ref, win_ref, y_ref):
    S, C = hc_ref.shape[1], hc_ref.shape[2]
    n_chunks = S // CONV_ROW_CHUNK
    for r in range(n_chunks):
        lo = r * CONV_ROW_CHUNK - CONV_PAD_ROWS
        hi = lo + CONV_WINDOW
        src_lo, src_hi = max(lo, 0), min(hi, S)
        for c in range(C // LANES):
            if lo < 0:
                win_ref[r, c, 0:-lo, :] = jnp.zeros((-lo, LANES), F32)
            if hi > S:
                win_ref[r, c, CONV_WINDOW - (hi - S):, :] = jnp.zeros((hi - S, LANES), F32)
            win_ref[r, c, src_lo - lo:src_hi - lo, :] = hc_ref[0, src_lo:src_hi, c * LANES:(c + 1) * LANES]
    first = CONV_PAD_ROWS - CONV_WIDTH // 2

    def chunk(r, carry):
        base = pl.multiple_of(r * CONV_ROW_CHUNK, CONV_ROW_CHUNK)
        tiles = [slice(c * LANES, (c + 1) * LANES) for c in range(C // LANES)]
        s1 = jnp.zeros((CONV_ROW_CHUNK, LANES), F32)
        for c, lanes in enumerate(tiles):
            acc = jnp.zeros((CONV_ROW_CHUNK, LANES), F32)
            for j in range(CONV_WIDTH):
                acc = acc + win_ref[r, c, first + j:first + j + CONV_ROW_CHUNK, :] * cw_ref[j:j + 1, lanes]
            acc = acc + cb_ref[:, lanes]
            y_ref[:, lanes] = acc
            s1 = s1 + acc
        mu = jnp.sum(s1, axis=-1, keepdims=True) * (1.0 / C)
        s2 = jnp.zeros((CONV_ROW_CHUNK, LANES), F32)
        for lanes in tiles:
            yc = y_ref[:, lanes] - mu
            s2 = s2 + yc * yc
        inv = lax.rsqrt(jnp.sum(s2, axis=-1, keepdims=True) * (1.0 / C) + LN_EPS)
        for lanes in tiles:
            y = (y_ref[:, lanes] - mu) * inv * g_ref[:, lanes] + b_ref[:, lanes]
            o_ref[0, pl.ds(base, CONV_ROW_CHUNK), lanes] = (y * (1.0 / (1.0 + jnp.exp(-y)))).astype(BF16)
        return carry

    lax.fori_loop(0, n_chunks, chunk, 0)


def _conv(hc, cw, cb, g, b):
    B, S, C = hc.shape
    full = lambda a: pl.BlockSpec(a.shape, lambda i: (0,) * a.ndim)
    return pl.pallas_call(
        _conv_kernel,
        grid=(B,),
        in_specs=[pl.BlockSpec((1, S, C), lambda i: (i, 0, 0)), full(cw), full(cb), full(g), full(b)],
        out_specs=pl.BlockSpec((1, S, C), lambda i: (i, 0, 0)),
        out_shape=jax.ShapeDtypeStruct((B, S, C), BF16),
        scratch_shapes=[pltpu.VMEM((S // CONV_ROW_CHUNK, C // LANES, CONV_WINDOW, LANES), F32),
                        pltpu.VMEM((CONV_ROW_CHUNK, C), F32)],
        compiler_params=pltpu.CompilerParams(
            dimension_semantics=("arbitrary",), vmem_limit_bytes=VMEM_LIMIT_BYTES),
        name="conv",
    )(hc, cw, cb, g, b)


HEADS_PER_TILE = LANES // V_HEAD_DIM
ATTN_HEADS_PER_STEP = 4


ATTN_SLOTS = 2


def _attn_kernel(q_ref, k_ref, v_ref, o_ref, s_ref, p_ref, *, tq):
    S = q_ref.shape[1]
    lane = lax.broadcasted_iota(jnp.int32, (tq, LANES), 1)
    item = 0
    for pr in range(v_ref.shape[2] // LANES):
        v = v_ref[0, :, pr * LANES:(pr + 1) * LANES]
        for qi in range(S // tq):
            rows = slice(qi * tq, (qi + 1) * tq)
            outs = []
            for hh in range(HEADS_PER_TILE):
                slot = item % ATTN_SLOTS
                item += 1
                c0 = (pr * HEADS_PER_TILE + hh) * HEAD_PAD
                s_ref[slot] = lax.dot_general(q_ref[0, rows, c0:c0 + HEAD_PAD], k_ref[0, :, c0:c0 + HEAD_PAD],
                                              (((1,), (1,)), ((), ())), preferred_element_type=F32)
                s = s_ref[slot]
                p = jnp.exp(s - jnp.max(s, axis=-1, keepdims=True))
                l = jnp.sum(p, axis=-1, keepdims=True)
                p_ref[slot] = p.astype(BF16)
                o = jnp.dot(p_ref[slot], v, preferred_element_type=F32)
                outs.append(o * (1.0 / l))
            o_ref[0, rows, pr * LANES:(pr + 1) * LANES] = jnp.where(lane < V_HEAD_DIM, outs[0], outs[1]).astype(BF16)


def _attn(q, k, v, *, tq):
    B, S, _ = q.shape
    qw = ATTN_HEADS_PER_STEP * HEAD_PAD
    vw = ATTN_HEADS_PER_STEP * V_HEAD_DIM
    return pl.pallas_call(
        functools.partial(_attn_kernel, tq=tq),
        grid=(B, MLA_HEADS // ATTN_HEADS_PER_STEP),
        in_specs=[
            pl.BlockSpec((1, S, qw), lambda b, h: (b, 0, h)),
            pl.BlockSpec((1, S, qw), lambda b, h: (b, 0, h)),
            pl.BlockSpec((1, S, vw), lambda b, h: (b, 0, h)),
        ],
        out_specs=pl.BlockSpec((1, S, vw), lambda b, h: (b, 0, h)),
        out_shape=jax.ShapeDtypeStruct((B, S, MLA_HEADS * V_HEAD_DIM), BF16),
        scratch_shapes=[pltpu.VMEM((ATTN_SLOTS, tq, S), F32), pltpu.VMEM((ATTN_SLOTS, tq, S), BF16)],
        compiler_params=pltpu.CompilerParams(
            dimension_semantics=("arbitrary", "arbitrary"), vmem_limit_bytes=62 * 1024 * 1024),
        name="attn",
    )(q, k, v)


def _out_proj_kernel(x_ref, att_ref, cv_ref, lng_ref, lnb_ref, wo_ref, g1_ref, b1_ref, wr_ref,
                     h1_ref, lt_ref):
    na = att_ref.shape[1]
    tm = x_ref.shape[0]
    n_lt = lt_ref.shape[1]
    sub = min(tm, TOKEN_SUB_TILE)
    for r0 in range(0, tm, sub):
        rows = slice(r0, r0 + sub)
        h0 = _layer_norm(x_ref[rows, :], lng_ref[...], lnb_ref[...])
        mix = jnp.dot(att_ref[rows, :], wo_ref[:na, :], preferred_element_type=F32)
        mix = mix + jnp.dot(cv_ref[rows, :], wo_ref[na:, :], preferred_element_type=F32)
        h1 = _layer_norm(DEEPNORM_ALPHA * h0 + mix, g1_ref[...], b1_ref[...])
        h1_ref[rows, :] = h1
        lg = jnp.dot(h1.astype(BF16), wr_ref[...], preferred_element_type=F32)
        lt_ref[0, :, rows] = lg.T[:n_lt, :]


def _out_proj(x2d, att, cv, lng, lnb, w_o, g1, b1, w_r2, *, tm, seq, n_lt):
    T, D = x2d.shape
    per = seq // tm
    full = lambda a: pl.BlockSpec(a.shape, lambda i: (0,) * a.ndim)
    return pl.pallas_call(
        _out_proj_kernel,
        grid=(T // tm,),
        in_specs=[
            pl.BlockSpec((tm, D), lambda i: (i, 0)),
            pl.BlockSpec((tm, att.shape[1]), lambda i: (i, 0)),
            pl.BlockSpec((tm, cv.shape[1]), lambda i: (i, 0)),
            full(lng), full(lnb), full(w_o), full(g1), full(b1), full(w_r2),
        ],
        out_specs=[
            pl.BlockSpec((tm, D), lambda i: (i, 0)),
            pl.BlockSpec((1, n_lt, tm), lambda i: (i // per, 0, i % per)),
        ],
        out_shape=[
            jax.ShapeDtypeStruct((T, D), F32),
            jax.ShapeDtypeStruct((T // seq, n_lt, seq), F32),
        ],
        compiler_params=pltpu.CompilerParams(
            dimension_semantics=("arbitrary",), vmem_limit_bytes=VMEM_LIMIT_BYTES),
        name="out_proj",
    )(x2d, att, cv, lng, lnb, w_o, g1, b1, w_r2)


def _route_kernel(lt_ref, rank_ref, gate_ref, off_ref, *, cap, tb):
    B, E, S = rank_ref.shape
    lt = lt_ref[:, :E, :] + lt_ref[:, E:, :]
    ex = jnp.exp(lt - jnp.max(lt, axis=1, keepdims=True))
    aff = ex / jnp.sum(ex, axis=1, keepdims=True)
    v = aff.reshape(B * E, S)
    rows = B * E
    kf = float(cap)

    def step(_, carry):
        lo, hi = carry
        mid = jnp.sqrt(jnp.maximum(lo, TOPK_SEARCH_FLOOR)) * jnp.sqrt(hi)
        cnt = jnp.sum(jnp.where(v >= mid, 1.0, 0.0), axis=1, keepdims=True)
        ge = cnt >= kf
        return jnp.where(ge, mid, lo), jnp.where(ge, hi, mid)

    lo, hi = lax.fori_loop(0, TOPK_SEARCH_STEPS, step,
                           (jnp.zeros((rows, 1), F32), jnp.full((rows, 1), 2.0, F32)))
    above = v >= hi
    tie = jnp.logical_and(v >= lo, jnp.logical_not(above))
    stacked = jnp.concatenate([jnp.where(above, 1.0, 0.0), jnp.where(tie, 1.0, 0.0)], axis=0)
    upper = jnp.where(lax.broadcasted_iota(jnp.int32, (S, S), 0) <= lax.broadcasted_iota(jnp.int32, (S, S), 1),
                      1.0, 0.0).astype(BF16)
    pc = jnp.dot(stacked.astype(BF16), upper, preferred_element_type=F32)
    pa = pc[:rows]
    pt = pc[rows:]
    need = kf - jnp.sum(jnp.where(above, 1.0, 0.0), axis=1, keepdims=True)
    sel = jnp.logical_or(above, jnp.logical_and(tie, pt <= need))
    taken = pa + jnp.minimum(pt, need)
    rank_ref[...] = jnp.where(sel, taken - 1.0, -1.0).astype(jnp.int32).reshape(B, E, S)
    gate_ref[...] = jnp.where(sel, v, 0.0).reshape(B, E, S)
    pick = jnp.where(lax.broadcasted_iota(jnp.int32, (S, LANES), 0) + 1
                     == lax.broadcasted_iota(jnp.int32, (S, LANES), 1) * tb, 1.0, 0.0).astype(BF16)
    off = jnp.dot(taken.astype(BF16), pick, preferred_element_type=F32)
    off_ref[...] = off.astype(jnp.int32).reshape(B, E, LANES)


def _route(lt2, *, cap, tb):
    B, E2, S = lt2.shape
    E = E2 // 2
    assert cap <= 256 and S // tb < LANES
    spec = pl.BlockSpec((B, E, S), lambda i: (0, 0, 0))
    ospec = pl.BlockSpec((B, E, LANES), lambda i: (0, 0, 0))
    return pl.pallas_call(
        functools.partial(_route_kernel, cap=cap, tb=tb),
        grid=(1,),
        in_specs=[pl.BlockSpec((B, E2, S), lambda i: (0, 0, 0))],
        out_specs=[spec, spec, ospec],
        out_shape=[jax.ShapeDtypeStruct((B, E, S), jnp.int32), jax.ShapeDtypeStruct((B, E, S), F32),
                   jax.ShapeDtypeStruct((B, E, LANES), jnp.int32)],
        compiler_params=pltpu.CompilerParams(
            dimension_semantics=("arbitrary",), vmem_limit_bytes=VMEM_LIMIT_BYTES),
        name="route",
    )(lt2)


ROUTE_TOKEN_BLOCK = 256
ROUTE_BLOCKS_PER_STEP = 2
SLOT_WINDOW = 64
SLOT_ALIGN = 16


def _slot_windows(off_ref, base, n_experts, cap, win):
    starts = []
    n_pass = jnp.int32(1)
    for e in range(n_experts):
        off = off_ref[base + e]
        end = off_ref[base + n_experts + e]
        start = jnp.minimum((off // SLOT_ALIGN) * SLOT_ALIGN, cap - win)
        starts.append(start)
        n_pass = jnp.maximum(n_pass, (end - start + (win - 1)) // win)
    return starts, n_pass


def _dispatch_kernel(off_ref, rank_ref, h_ref, xg_ref, *, cap, win, tb):
    E = rank_ref.shape[1]
    nsb = rank_ref.shape[2] // tb
    b, j = pl.program_id(0), pl.program_id(1)
    n_bounds = pl.num_programs(1) * nsb + 1

    @pl.when(j == 0)
    def _():
        xg_ref[...] = jnp.zeros(xg_ref.shape, BF16)

    sub = lax.broadcasted_iota(jnp.int32, (win, tb), 0)

    def one_pass(sb, starts, p, first):
        cols = slice(sb * tb, (sb + 1) * tb)
        pieces, rows = [], []
        for e in range(E):
            lo = starts[e] + p * win
            ws = pl.multiple_of(jnp.minimum(lo, cap - win), SLOT_ALIGN)
            tgt = ws + sub
            hit = rank_ref[0, e:e + 1, cols] == tgt
            if not first:
                hit = jnp.logical_and(hit, tgt >= lo)
            pieces.append(jnp.where(hit, 1.0, 0.0).astype(BF16))
            rows.append(ws)
        onehot = jnp.concatenate(pieces, axis=0)
        hb = h_ref[0, cols, :].astype(BF16)
        got = jnp.dot(onehot, hb, preferred_element_type=F32).astype(BF16)
        for e in range(E):
            xg_ref[e, 0, pl.ds(rows[e], win), :] += got[e * win:(e + 1) * win, :]

    plans = []
    for sb in range(nsb):
        starts, n_pass = _slot_windows(off_ref, (b * n_bounds + j * nsb + sb) * E, E, cap, win)
        plans.append((starts, n_pass))
        one_pass(sb, starts, 0, True)
    for sb in range(nsb):
        starts, n_pass = plans[sb]

        def extra(p, carry, sb=sb, starts=starts):
            one_pass(sb, starts, p, False)
            return carry

        lax.fori_loop(1, n_pass, extra, 0)


def _dispatch(off_flat, rank, h1b, *, cap, tb, nsb):
    B, E, S = rank.shape
    D = h1b.shape[-1]
    win = min(SLOT_WINDOW, cap)
    ts = tb * nsb
    assert cap % SLOT_ALIGN == 0 and win % SLOT_ALIGN == 0 and S % ts == 0
    return pl.pallas_call(
        functools.partial(_dispatch_kernel, cap=cap, win=win, tb=tb),
        grid_spec=pltpu.PrefetchScalarGridSpec(
            num_scalar_prefetch=1,
            grid=(B, S // ts),
            in_specs=[pl.BlockSpec((1, E, ts), lambda b, j, off: (b, 0, j)),
                      pl.BlockSpec((1, ts, D), lambda b, j, off: (b, j, 0))],
            out_specs=pl.BlockSpec((E, 1, cap, D), lambda b, j, off: (0, b, 0, 0)),
        ),
        out_shape=jax.ShapeDtypeStruct((E, B, cap, D), BF16),
        compiler_params=pltpu.CompilerParams(
            dimension_semantics=("arbitrary", "arbitrary"), vmem_limit_bytes=VMEM_LIMIT_BYTES),
        name="dispatch",
    )(off_flat, rank, h1b)


EXPERT_ROW_BLOCK = 512


def _expert_kernel(x_ref, wg_ref, wu_ref, wd_ref, y_ref, acc_ref, wgb_ref, wub_ref, wdb_ref, *, n_chunks):
    f = pl.program_id(1)
    rows = x_ref.shape[1]
    rb = min(rows, EXPERT_ROW_BLOCK)
    last = n_chunks - 1

    def chunk(first, final):
        wgb_ref[...] = wg_ref[0].astype(BF16)
        wub_ref[...] = wu_ref[0].astype(BF16)
        wdb_ref[...] = wd_ref[0].astype(BF16)
        for r0 in range(0, rows, rb):
            x = x_ref[0, r0:r0 + rb, :]
            a = jnp.dot(x, wgb_ref[...], preferred_element_type=F32)
            u = jnp.dot(x, wub_ref[...], preferred_element_type=F32)
            hmid = (a * (1.0 / (1.0 + jnp.exp(-a))) * u).astype(BF16)
            part = jnp.dot(hmid, wdb_ref[...], preferred_element_type=F32)
            if not first:
                part = acc_ref[r0:r0 + rb, :] + part
            if final:
                y_ref[0, r0:r0 + rb, :] = part.astype(BF16)
            else:
                acc_ref[r0:r0 + rb, :] = part

    if n_chunks == 1:
        chunk(True, True)
    else:
        pl.when(f == 0)(lambda: chunk(True, False))
        pl.when(jnp.logical_and(f > 0, f < last))(lambda: chunk(False, False))
        pl.when(f == last)(lambda: chunk(False, True))


def _experts(xg, w_gate, w_up, w_down, *, tf):
    E, rows, D = xg.shape
    F = w_gate.shape[-1]
    return pl.pallas_call(
        functools.partial(_expert_kernel, n_chunks=F // tf),
        grid=(E, F // tf),
        in_specs=[
            pl.BlockSpec((1, rows, D), lambda e, f: (e, 0, 0)),
            pl.BlockSpec((1, D, tf), lambda e, f: (e, 0, f)),
            pl.BlockSpec((1, D, tf), lambda e, f: (e, 0, f)),
            pl.BlockSpec((1, tf, D), lambda e, f: (e, f, 0)),
        ],
        out_specs=pl.BlockSpec((1, rows, D), lambda e, f: (e, 0, 0)),
        out_shape=jax.ShapeDtypeStruct((E, rows, D), BF16),
        scratch_shapes=[pltpu.VMEM((rows, D), F32), pltpu.VMEM((D, tf), BF16), pltpu.VMEM((D, tf), BF16),
                        pltpu.VMEM((tf, D), BF16)],
        compiler_params=pltpu.CompilerParams(
            dimension_semantics=("arbitrary", "arbitrary"), vmem_limit_bytes=VMEM_LIMIT_BYTES),
        name="experts",
    )(xg, w_gate, w_up, w_down)


def _combine_kernel(off_ref, rank_ref, gate_ref, y_ref, h_ref, g2_ref, b2_ref, o_ref, *, cap, win, tb):
    E = rank_ref.shape[1]
    nsb = rank_ref.shape[2] // tb
    b, j = pl.program_id(0), pl.program_id(1)
    n_bounds = pl.num_programs(1) * nsb + 1
    per_tile = LANES // win
    lane = lax.broadcasted_iota(jnp.int32, (1, LANES), 1)
    first_lane = lax.broadcasted_iota(jnp.int32, (LANES, E * win), 0) * win
    col = lax.broadcasted_iota(jnp.int32, (LANES, E * win), 1)
    spread = jnp.where(jnp.logical_and(col >= first_lane, col < first_lane + win), 1.0, 0.0).astype(BF16)
    pad_rows = jnp.zeros((LANES - E, tb), F32)

    def token_major(x_et):
        return jnp.concatenate([x_et, pad_rows], axis=0).T.astype(BF16)

    def one_pass(sb, starts, p, first):
        cols = slice(sb * tb, (sb + 1) * tb)
        rk = jnp.dot(token_major(rank_ref[0, :, cols].astype(F32)), spread, preferred_element_type=F32)
        gt = jnp.dot(token_major(gate_ref[0, :, cols]), spread, preferred_element_type=F32)
        tgts, los, ys = [], [], []
        for t in range(E // per_tile):
            tgt = lo_v = None
            for u in range(per_tile):
                e = t * per_tile + u
                lo = starts[e] + p * win
                ws = pl.multiple_of(jnp.minimum(lo, cap - win), SLOT_ALIGN)
                ys.append(y_ref[e, 0, pl.ds(ws, win), :])
                t_u = ws + lane - u * win
                if u == 0:
                    tgt, lo_v = t_u, jnp.full((1, LANES), lo, jnp.int32)
                else:
                    here = lane >= u * win
                    tgt = jnp.where(here, t_u, tgt)
                    lo_v = jnp.where(here, lo, lo_v)
            tgts.append(tgt)
            los.append(lo_v)
        tgt = jnp.concatenate(tgts, axis=1)
        hit = rk == tgt.astype(F32)
        if not first:
            hit = jnp.logical_and(hit, tgt >= jnp.concatenate(los, axis=1))
        gates = jnp.where(hit, gt, 0.0).astype(BF16)
        return jnp.dot(gates, jnp.concatenate(ys, axis=0), preferred_element_type=F32)

    plans = []
    for sb in range(nsb):
        rows = slice(sb * tb, (sb + 1) * tb)
        starts, n_pass = _slot_windows(off_ref, (b * n_bounds + j * nsb + sb) * E, E, cap, win)
        plans.append((starts, n_pass))
        o_ref[0, rows, :] = DEEPNORM_ALPHA * h_ref[0, rows, :] + one_pass(sb, starts, 0, True)
    for sb in range(nsb):
        starts, n_pass = plans[sb]

        def extra(p, carry, sb=sb, starts=starts):
            o_ref[0, sb * tb:(sb + 1) * tb, :] += one_pass(sb, starts, p, False)
            return carry

        lax.fori_loop(1, n_pass, extra, 0)
    o_ref[0] = _layer_norm(o_ref[0], g2_ref[...], b2_ref[...])


def _combine(off_flat, rank, gate, y, h1, g2, b2, *, tb, nsb):
    B, E, S = rank.shape
    cap, D = y.shape[2], y.shape[3]
    win = min(SLOT_WINDOW, cap)
    ts = tb * nsb
    assert cap % SLOT_ALIGN == 0 and win % SLOT_ALIGN == 0 and LANES % win == 0 and E % (LANES // win) == 0
    assert S % ts == 0
    full = lambda a: pl.BlockSpec(a.shape, lambda b, j, off: (0,) * a.ndim)
    return pl.pallas_call(
        functools.partial(_combine_kernel, cap=cap, win=win, tb=tb),
        grid_spec=pltpu.PrefetchScalarGridSpec(
            num_scalar_prefetch=1,
            grid=(B, S // ts),
            in_specs=[
                pl.BlockSpec((1, E, ts), lambda b, j, off: (b, 0, j)),
                pl.BlockSpec((1, E, ts), lambda b, j, off: (b, 0, j)),
                pl.BlockSpec((E, 1, cap, D), lambda b, j, off: (0, b, 0, 0)),
                pl.BlockSpec((1, ts, D), lambda b, j, off: (b, j, 0)),
                full(g2), full(b2),
            ],
            out_specs=pl.BlockSpec((1, ts, D), lambda b, j, off: (b, j, 0)),
        ),
        out_shape=jax.ShapeDtypeStruct((B, S, D), F32),
        compiler_params=pltpu.CompilerParams(
            dimension_semantics=("arbitrary", "arbitrary"), vmem_limit_bytes=VMEM_LIMIT_BYTES),
        name="combine",
    )(off_flat, rank, gate, y, h1, g2, b2)


def _tile(n, target):
    t = min(n, target)
    assert n % t == 0, (n, t)
    return t


def kernel(x, positions, emb_ln_g, emb_ln_b, w_in, q_norm_g, w_qb, kv_norm_g, w_kvb, conv_w, conv_b,
           conv_ln_g, conv_ln_b, w_o, ln1_g, ln1_b, w_router, w_gate, w_up, w_down, ln2_g, ln2_b):
    B, S, D = x.shape
    T = B * S
    H = MLA_HEADS
    q_rank = q_norm_g.shape[-1]
    kv_rank = kv_norm_g.shape[-1]
    conv_ch = conv_w.shape[-1]
    qk_dim = QK_NOPE_DIM + QK_ROPE_DIM
    cap = CAPACITY_FACTOR * S // N_EXPERTS
    assert w_in.shape[0] == DEPTH == 1
    row = lambda a: a.reshape(1, -1)

    wi = w_in[0]
    c1, c2, c3 = q_rank, q_rank + kv_rank, q_rank + kv_rank + QK_ROPE_DIM
    tail = LANES - QK_NOPE_DIM - QK_ROPE_DIM
    kr_cols = jnp.pad(wi[:, c2:c3], ((0, 0), (QK_NOPE_DIM, tail)))
    w_in_r = jnp.concatenate(
        [wi[:, :c2], wi[:, c3:c3 + conv_ch], wi[:, c3 + conv_ch:], kr_cols], axis=1).astype(BF16)
    w_qb_pad = jnp.pad(w_qb[0].reshape(q_rank, H, qk_dim),
                       ((0, 0), (0, 0), (0, HEAD_PAD - qk_dim))).reshape(q_rank, H * HEAD_PAD).astype(BF16)
    wkv = w_kvb[0].reshape(kv_rank, H, QK_NOPE_DIM + V_HEAD_DIM)
    wk_pad = jnp.pad(wkv[:, :, :QK_NOPE_DIM], ((0, 0), (0, 0), (0, HEAD_PAD - QK_NOPE_DIM)))
    w_kv_r = jnp.concatenate([wk_pad.reshape(kv_rank, H * HEAD_PAD),
                              wkv[:, :, QK_NOPE_DIM:].reshape(kv_rank, H * V_HEAD_DIM)], axis=1).astype(BF16)
    half = QK_ROPE_DIM // 2
    inv_freq = (ROPE_THETA ** (-jnp.arange(half, dtype=F32) / half)).reshape(half, 1)

    x2d = x.reshape(T, D)
    tm = _tile(S, 512)
    q, k, v, hc = _in_proj(x2d, positions.reshape(1, T), inv_freq, row(emb_ln_g), row(emb_ln_b), w_in_r,
                           row(q_norm_g[0]), w_qb_pad, row(kv_norm_g[0]), w_kv_r, tm=tm)
    cv = _conv(hc.reshape(B, S, conv_ch), conv_w[0], row(conv_b[0]), row(conv_ln_g[0]), row(conv_ln_b[0]))
    att = _attn(q.reshape(B, S, H * HEAD_PAD), k.reshape(B, S, H * HEAD_PAD),
                v.reshape(B, S, H * V_HEAD_DIM), tq=_tile(S, 512))
    wr_hi = w_router[0].astype(BF16)
    w_r2 = jnp.concatenate([wr_hi, (w_router[0] - wr_hi.astype(F32)).astype(BF16)], axis=1)
    w_r2 = jnp.pad(w_r2, ((0, 0), (0, LANES - 2 * N_EXPERTS)))
    h1, lt2 = _out_proj(x2d, att.reshape(T, H * V_HEAD_DIM), cv.reshape(T, conv_ch), row(emb_ln_g),
                        row(emb_ln_b), w_o[0].astype(BF16), row(ln1_g[0]), row(ln1_b[0]),
                        w_r2, tm=_tile(S, 1024), seq=S, n_lt=2 * N_EXPERTS)
    h1 = h1.reshape(B, S, D)
    tb = _tile(S, ROUTE_TOKEN_BLOCK)
    rank, gate, off = _route(lt2, cap=cap, tb=tb)
    off_flat = jnp.swapaxes(off[:, :, :S // tb + 1], 1, 2).reshape(-1)
    nsb = _tile(S // tb, ROUTE_BLOCKS_PER_STEP)
    xg = _dispatch(off_flat, rank, h1, cap=cap, tb=tb, nsb=nsb)
    y = _experts(xg.reshape(N_EXPERTS, B * cap, D), w_gate[0], w_up[0], w_down[0],
                 tf=_tile(w_gate.shape[-1], 512)).reshape(N_EXPERTS, B, cap, D)
    return _combine(off_flat, rank, gate, y, h1, row(ln2_g[0]), row(ln2_b[0]), tb=tb, nsb=nsb)
```

```python
import functools

import jax
import jax.numpy as jnp
from jax import lax
from jax.experimental import pallas as pl
from jax.experimental.pallas import tpu as pltpu

F32 = jnp.float32
BF16 = jnp.bfloat16

MLA_HEADS = 8
QK_NOPE_DIM = 64
QK_ROPE_DIM = 32
V_HEAD_DIM = 64
CONV_WIDTH = 31
ROPE_THETA = 10000.0
N_EXPERTS = 16
CAPACITY_FACTOR = 2
DEPTH = 1
DEEPNORM_ALPHA = (2.0 * DEPTH) ** 0.25
LN_EPS = 1e-5
RMS_EPS = 1e-6

LANES = 128
SUBLANES = 8
HEAD_PAD = LANES
VMEM_LIMIT_BYTES = 56 * 1024 * 1024
TOKEN_SUB_TILE = 256

TOPK_SEARCH_STEPS = 36
TOPK_SEARCH_FLOOR = 1e-30


def _layer_norm(x, g, b):
    mu = jnp.mean(x, axis=-1, keepdims=True)
    xc = x - mu
    var = jnp.mean(xc * xc, axis=-1, keepdims=True)
    return xc * lax.rsqrt(var + LN_EPS) * g + b


def _rms_norm(x, g):
    return x * lax.rsqrt(jnp.mean(x * x, axis=-1, keepdims=True) + RMS_EPS) * g


def _rope_tile(x, cos_t, sin_t, lane):
    half = QK_ROPE_DIM // 2
    fwd = pltpu.roll(x, LANES - half, 1)
    bwd = pltpu.roll(x, half, 1)
    partner = jnp.where(lane < QK_NOPE_DIM + half, fwd, bwd)
    return x * cos_t + partner * sin_t


def _in_proj_kernel(x_ref, pos_ref, invf_ref, lng_ref, lnb_ref, win_ref, qg_ref, wqb_ref,
                    kvg_ref, wkv_ref, q_ref, k_ref, v_ref, hc_ref, *, q_rank, kv_rank, conv_ch):
    tm = x_ref.shape[0]
    sub = min(tm, TOKEN_SUB_TILE)
    c1 = q_rank
    c2 = c1 + kv_rank
    c3 = c2 + conv_ch
    c4 = c3 + conv_ch
    tail = LANES - QK_NOPE_DIM - QK_ROPE_DIM
    nk = MLA_HEADS * HEAD_PAD
    scale = (QK_NOPE_DIM + QK_ROPE_DIM) ** -0.5
    ones = jnp.ones((QK_NOPE_DIM, sub), F32)
    zeros = jnp.zeros((QK_NOPE_DIM, sub), F32)
    lane = lax.broadcasted_iota(jnp.int32, (sub, LANES), 1)
    for r0 in range(0, tm, sub):
        rows = slice(r0, r0 + sub)
        h = _layer_norm(x_ref[rows, :], lng_ref[...], lnb_ref[...])
        proj = jnp.dot(h.astype(BF16), win_ref[...], preferred_element_type=F32)
        cq = proj[:, :c1]
        ckv = proj[:, c1:c2]
        a = proj[:, c2:c3]
        g = proj[:, c3:c4]
        kr = proj[:, c4:c4 + LANES]

        ang = pos_ref[:, rows].astype(F32) * invf_ref[...]
        cos = jnp.cos(ang)
        sin = jnp.sin(ang)
        cos_t = jnp.concatenate([ones, cos, cos, ones[:tail]], axis=0).T
        sin_t = jnp.concatenate([zeros, -sin, sin, zeros[:tail]], axis=0).T

        cqn = _rms_norm(cq, qg_ref[...])
        q = jnp.dot(cqn.astype(BF16), wqb_ref[...], preferred_element_type=F32) * scale
        ckvn = _rms_norm(ckv, kvg_ref[...])
        kv = jnp.dot(ckvn.astype(BF16), wkv_ref[...], preferred_element_type=F32)
        k_pe = _rope_tile(kr, cos_t, sin_t, lane)
        for hd in range(MLA_HEADS):
            sl = slice(hd * HEAD_PAD, (hd + 1) * HEAD_PAD)
            q_ref[rows, sl] = _rope_tile(q[:, sl], cos_t, sin_t, lane).astype(BF16)
            k_ref[rows, sl] = (kv[:, sl] + k_pe).astype(BF16)
        v_ref[rows, :] = kv[:, nk:].astype(BF16)
        hc_ref[rows, :] = a * (1.0 / (1.0 + jnp.exp(-g)))


def _in_proj(x2d, pos_row, invf_col, lng, lnb, w_in_r, qg, w_qb_pad, kvg, w_kv_r, *, tm):
    T, D = x2d.shape
    q_rank = w_qb_pad.shape[0]
    kv_rank = w_kv_r.shape[0]
    conv_ch = (w_in_r.shape[1] - q_rank - kv_rank - LANES) // 2
    nk = MLA_HEADS * HEAD_PAD
    nv = MLA_HEADS * V_HEAD_DIM
    full = lambda a: pl.BlockSpec(a.shape, lambda i: (0,) * a.ndim)
    return pl.pallas_call(
        functools.partial(_in_proj_kernel, q_rank=q_rank, kv_rank=kv_rank, conv_ch=conv_ch),
        grid=(T // tm,),
        in_specs=[
            pl.BlockSpec((tm, D), lambda i: (i, 0)),
            pl.BlockSpec((1, tm), lambda i: (0, i)),
            full(invf_col), full(lng), full(lnb), full(w_in_r), full(qg), full(w_qb_pad),
            full(kvg), full(w_kv_r),
        ],
        out_specs=[
            pl.BlockSpec((tm, nk), lambda i: (i, 0)),
            pl.BlockSpec((tm, nk), lambda i: (i, 0)),
            pl.BlockSpec((tm, nv), lambda i: (i, 0)),
            pl.BlockSpec((tm, conv_ch), lambda i: (i, 0)),
        ],
        out_shape=[
            jax.ShapeDtypeStruct((T, nk), BF16),
            jax.ShapeDtypeStruct((T, nk), BF16),
            jax.ShapeDtypeStruct((T, nv), BF16),
            jax.ShapeDtypeStruct((T, conv_ch), F32),
        ],
        compiler_params=pltpu.CompilerParams(
            dimension_semantics=("arbitrary",), vmem_limit_bytes=VMEM_LIMIT_BYTES),
        name="in_proj",
    )(x2d, pos_row, invf_col, lng, lnb, w_in_r, qg, w_qb_pad, kvg, w_kv_r)


CONV_PAD_ROWS = 16
CONV_ROW_CHUNK = 128
CONV_CHUNKS_PER_STEP = 4
CONV_WINDOW = CONV_ROW_CHUNK + 2 * CONV_PAD_ROWS


def _conv_kernel(hc_ref, cw_ref, cb_ref, g_ref, b_ref, o_ref, win_ref, y_ref):
    S, C = hc_ref.shape[1], hc_ref.shape[2]
    n_chunks = S // CONV_ROW_CHUNK
    for r in range(n_chunks):
        lo = r * CONV_ROW_CHUNK - CONV_PAD_ROWS
        hi = lo + CONV_WINDOW
        src_lo, src_hi = max(lo, 0), min(hi, S)
        for c in range(C // LANES):
            if lo < 0:
                win_ref[r, c, 0:-lo, :] = jnp.zeros((-lo, LANES), F32)
            if hi > S:
                win_ref[r, c, CONV_WINDOW - (hi - S):, :] = jnp.zeros((hi - S, LANES), F32)
            win_ref[r, c, src_lo - lo:src_hi - lo, :] = hc_ref[0, src_lo:src_hi, c * LANES:(c + 1) * LANES]
    first = CONV_PAD_ROWS - CONV_WIDTH // 2

    n_tiles = C // LANES
    group = min(CONV_CHUNKS_PER_STEP, n_chunks)

    def chunk_group(rg, carry):
        sums = []
        for u in range(group):
            r = rg * group + u

            def tile_conv(c, s1, r=r, u=u):
                acc = jnp.zeros((CONV_ROW_CHUNK, LANES), F32)
                for j in range(CONV_WIDTH):
                    acc = acc + win_ref[r, c, first + j:first + j + CONV_ROW_CHUNK, :] * cw_ref[c, j:j + 1, :]
                acc = acc + cb_ref[c]
                y_ref[u, c] = acc
                return s1 + acc

            sums.append(lax.fori_loop(0, n_tiles, tile_conv, jnp.zeros((CONV_ROW_CHUNK, LANES), F32)))
        for u in range(group):
            base = pl.multiple_of((rg * group + u) * CONV_ROW_CHUNK, CONV_ROW_CHUNK)
            mu = jnp.sum(sums[u], axis=-1, keepdims=True) * (1.0 / C)
            s2 = jnp.zeros((CONV_ROW_CHUNK, LANES), F32)
            for c in range(n_tiles):
                yc = y_ref[u, c] - mu
                s2 = s2 + yc * yc
            inv = lax.rsqrt(jnp.sum(s2, axis=-1, keepdims=True) * (1.0 / C) + LN_EPS)
            for c in range(n_tiles):
                lanes = slice(c * LANES, (c + 1) * LANES)
                y = (y_ref[u, c] - mu) * inv * g_ref[:, lanes] + b_ref[:, lanes]
                o_ref[0, pl.ds(base, CONV_ROW_CHUNK), lanes] = (y * (1.0 / (1.0 + jnp.exp(-y)))).astype(BF16)
        return carry

    lax.fori_loop(0, n_chunks // group, chunk_group, 0)


def _conv(hc, cw, cb, g, b):
    B, S, C = hc.shape
    full = lambda a: pl.BlockSpec(a.shape, lambda i: (0,) * a.ndim)
    return pl.pallas_call(
        _conv_kernel,
        grid=(B,),
        in_specs=[pl.BlockSpec((1, S, C), lambda i: (i, 0, 0)), full(cw), full(cb), full(g), full(b)],
        out_specs=pl.BlockSpec((1, S, C), lambda i: (i, 0, 0)),
        out_shape=jax.ShapeDtypeStruct((B, S, C), BF16),
        scratch_shapes=[pltpu.VMEM((S // CONV_ROW_CHUNK, C // LANES, CONV_WINDOW, LANES), F32),
                        pltpu.VMEM((CONV_CHUNKS_PER_STEP, C // LANES, CONV_ROW_CHUNK, LANES), F32)],
        compiler_params=pltpu.CompilerParams(
            dimension_semantics=("arbitrary",), vmem_limit_bytes=VMEM_LIMIT_BYTES),
        name="conv",
    )(hc, cw, cb, g, b)


HEADS_PER_TILE = LANES // V_HEAD_DIM
ATTN_HEADS_PER_STEP = 4


ATTN_SLOTS = 2


def _attn_kernel(q_ref, k_ref, v_ref, o_ref, s_ref, p_ref, *, tq):
    S = q_ref.shape[1]
    lane = lax.broadcasted_iota(jnp.int32, (tq, LANES), 1)
    item = 0
    for pr in range(v_ref.shape[2] // LANES):
        v = v_ref[0, :, pr * LANES:(pr + 1) * LANES]
        for qi in range(S // tq):
            rows = slice(qi * tq, (qi + 1) * tq)
            outs = []
            for hh in range(HEADS_PER_TILE):
                slot = item % ATTN_SLOTS
                item += 1
                c0 = (pr * HEADS_PER_TILE + hh) * HEAD_PAD
                s_ref[slot] = lax.dot_general(q_ref[0, rows, c0:c0 + HEAD_PAD], k_ref[0, :, c0:c0 + HEAD_PAD],
                                              (((1,), (1,)), ((), ())), preferred_element_type=F32)
                s = s_ref[slot]
                p = jnp.exp(s - jnp.max(s, axis=-1, keepdims=True))
                l = jnp.sum(p, axis=-1, keepdims=True)
                p_ref[slot] = p.astype(BF16)
                o = jnp.dot(p_ref[slot], v, preferred_element_type=F32)
                outs.append(o * (1.0 / l))
            o_ref[0, rows, pr * LANES:(pr + 1) * LANES] = jnp.where(lane < V_HEAD_DIM, outs[0], outs[1]).astype(BF16)


def _attn(q, k, v, *, tq):
    B, S, _ = q.shape
    qw = ATTN_HEADS_PER_STEP * HEAD_PAD
    vw = ATTN_HEADS_PER_STEP * V_HEAD_DIM
    return pl.pallas_call(
        functools.partial(_attn_kernel, tq=tq),
        grid=(B, MLA_HEADS // ATTN_HEADS_PER_STEP),
        in_specs=[
            pl.BlockSpec((1, S, qw), lambda b, h: (b, 0, h)),
            pl.BlockSpec((1, S, qw), lambda b, h: (b, 0, h)),
            pl.BlockSpec((1, S, vw), lambda b, h: (b, 0, h)),
        ],
        out_specs=pl.BlockSpec((1, S, vw), lambda b, h: (b, 0, h)),
        out_shape=jax.ShapeDtypeStruct((B, S, MLA_HEADS * V_HEAD_DIM), BF16),
        scratch_shapes=[pltpu.VMEM((ATTN_SLOTS, tq, S), F32), pltpu.VMEM((ATTN_SLOTS, tq, S), BF16)],
        compiler_params=pltpu.CompilerParams(
            dimension_semantics=("arbitrary", "arbitrary"), vmem_limit_bytes=62 * 1024 * 1024),
        name="attn",
    )(q, k, v)


def _out_proj_kernel(x_ref, att_ref, cv_ref, lng_ref, lnb_ref, wo_ref, g1_ref, b1_ref, wr_ref,
                     h1_ref, lt_ref):
    na = att_ref.shape[1]
    tm = x_ref.shape[0]
    n_lt = lt_ref.shape[1]
    sub = min(tm, TOKEN_SUB_TILE)
    for r0 in range(0, tm, sub):
        rows = slice(r0, r0 + sub)
        h0 = _layer_norm(x_ref[rows, :], lng_ref[...], lnb_ref[...])
        mix = jnp.dot(att_ref[rows, :], wo_ref[:na, :], preferred_element_type=F32)
        mix = mix + jnp.dot(cv_ref[rows, :], wo_ref[na:, :], preferred_element_type=F32)
        h1 = _layer_norm(DEEPNORM_ALPHA * h0 + mix, g1_ref[...], b1_ref[...])
        h1_ref[rows, :] = h1
        lg = jnp.dot(h1.astype(BF16), wr_ref[...], preferred_element_type=F32)
        lt_ref[0, :, rows] = lg.T[:n_lt, :]


def _out_proj(x2d, att, cv, lng, lnb, w_o, g1, b1, w_r2, *, tm, seq, n_lt):
    T, D = x2d.shape
    per = seq // tm
    full = lambda a: pl.BlockSpec(a.shape, lambda i: (0,) * a.ndim)
    return pl.pallas_call(
        _out_proj_kernel,
        grid=(T // tm,),
        in_specs=[
            pl.BlockSpec((tm, D), lambda i: (i, 0)),
            pl.BlockSpec((tm, att.shape[1]), lambda i: (i, 0)),
            pl.BlockSpec((tm, cv.shape[1]), lambda i: (i, 0)),
            full(lng), full(lnb), full(w_o), full(g1), full(b1), full(w_r2),
        ],
        out_specs=[
            pl.BlockSpec((tm, D), lambda i: (i, 0)),
            pl.BlockSpec((1, n_lt, tm), lambda i: (i // per, 0, i % per)),
        ],
        out_shape=[
            jax.ShapeDtypeStruct((T, D), F32),
            jax.ShapeDtypeStruct((T // seq, n_lt, seq), F32),
        ],
        compiler_params=pltpu.CompilerParams(
            dimension_semantics=("arbitrary",), vmem_limit_bytes=VMEM_LIMIT_BYTES),
        name="out_proj",
    )(x2d, att, cv, lng, lnb, w_o, g1, b1, w_r2)


def _route_kernel(lt_ref, rank_ref, gate_ref, off_ref, *, cap, tb):
    B, E, S = rank_ref.shape
    lt = lt_ref[:, :E, :] + lt_ref[:, E:, :]
    ex = jnp.exp(lt - jnp.max(lt, axis=1, keepdims=True))
    aff = ex / jnp.sum(ex, axis=1, keepdims=True)
    v = aff.reshape(B * E, S)
    rows = B * E
    kf = float(cap)

    def step(_, carry):
        lo, hi = carry
        mid = jnp.sqrt(jnp.maximum(lo, TOPK_SEARCH_FLOOR)) * jnp.sqrt(hi)
        cnt = jnp.sum(jnp.where(v >= mid, 1.0, 0.0), axis=1, keepdims=True)
        ge = cnt >= kf
        return jnp.where(ge, mid, lo), jnp.where(ge, hi, mid)

    lo, hi = lax.fori_loop(0, TOPK_SEARCH_STEPS, step,
                           (jnp.zeros((rows, 1), F32), jnp.full((rows, 1), 2.0, F32)))
    above = v >= hi
    tie = jnp.logical_and(v >= lo, jnp.logical_not(above))
    stacked = jnp.concatenate([jnp.where(above, 1.0, 0.0), jnp.where(tie, 1.0, 0.0)], axis=0)
    upper = jnp.where(lax.broadcasted_iota(jnp.int32, (S, S), 0) <= lax.broadcasted_iota(jnp.int32, (S, S), 1),
                      1.0, 0.0).astype(BF16)
    pc = jnp.dot(stacked.astype(BF16), upper, preferred_element_type=F32)
    pa = pc[:rows]
    pt = pc[rows:]
    need = kf - jnp.sum(jnp.where(above, 1.0, 0.0), axis=1, keepdims=True)
    sel = jnp.logical_or(above, jnp.logical_and(tie, pt <= need))
    taken = pa + jnp.minimum(pt, need)
    rank_ref[...] = jnp.where(sel, taken - 1.0, -1.0).astype(jnp.int32).reshape(B, E, S)
    gate_ref[...] = jnp.where(sel, v, 0.0).reshape(B, E, S)
    pick = jnp.where(lax.broadcasted_iota(jnp.int32, (S, LANES), 0) + 1
                     == lax.broadcasted_iota(jnp.int32, (S, LANES), 1) * tb, 1.0, 0.0).astype(BF16)
    off = jnp.dot(taken.astype(BF16), pick, preferred_element_type=F32)
    off_ref[...] = off.astype(jnp.int32).reshape(B, E, LANES)


def _route(lt2, *, cap, tb):
    B, E2, S = lt2.shape
    E = E2 // 2
    assert cap <= 256 and S // tb < LANES
    spec = pl.BlockSpec((B, E, S), lambda i: (0, 0, 0))
    ospec = pl.BlockSpec((B, E, LANES), lambda i: (0, 0, 0))
    return pl.pallas_call(
        functools.partial(_route_kernel, cap=cap, tb=tb),
        grid=(1,),
        in_specs=[pl.BlockSpec((B, E2, S), lambda i: (0, 0, 0))],
        out_specs=[spec, spec, ospec],
        out_shape=[jax.ShapeDtypeStruct((B, E, S), jnp.int32), jax.ShapeDtypeStruct((B, E, S), F32),
                   jax.ShapeDtypeStruct((B, E, LANES), jnp.int32)],
        compiler_params=pltpu.CompilerParams(
            dimension_semantics=("arbitrary",), vmem_limit_bytes=VMEM_LIMIT_BYTES),
        name="route",
    )(lt2)


ROUTE_TOKEN_BLOCK = 256
ROUTE_BLOCKS_PER_STEP = 2
SLOT_WINDOW = 64
SLOT_ALIGN = 16


def _slot_windows(off_ref, base, n_experts, cap, win):
    starts = []
    n_pass = jnp.int32(1)
    for e in range(n_experts):
        off = off_ref[base + e]
        end = off_ref[base + n_experts + e]
        start = jnp.minimum((off // SLOT_ALIGN) * SLOT_ALIGN, cap - win)
        starts.append(start)
        n_pass = jnp.maximum(n_pass, (end - start + (win - 1)) // win)
    return starts, n_pass


def _dispatch_kernel(off_ref, rank_ref, h_ref, xg_ref, *, cap, win, tb):
    E = rank_ref.shape[1]
    nsb = rank_ref.shape[2] // tb
    b, j = pl.program_id(0), pl.program_id(1)
    n_bounds = pl.num_programs(1) * nsb + 1

    @pl.when(j == 0)
    def _():
        xg_ref[...] = jnp.zeros(xg_ref.shape, BF16)

    sub = lax.broadcasted_iota(jnp.int32, (win, tb), 0)

    def one_pass(sb, starts, p, first):
        cols = slice(sb * tb, (sb + 1) * tb)
        pieces, rows = [], []
        for e in range(E):
            lo = starts[e] + p * win
            ws = pl.multiple_of(jnp.minimum(lo, cap - win), SLOT_ALIGN)
            tgt = ws + sub
            hit = rank_ref[0, e:e + 1, cols] == tgt
            if not first:
                hit = jnp.logical_and(hit, tgt >= lo)
            pieces.append(jnp.where(hit, 1.0, 0.0).astype(BF16))
            rows.append(ws)
        onehot = jnp.concatenate(pieces, axis=0)
        hb = h_ref[0, cols, :].astype(BF16)
        got = jnp.dot(onehot, hb, preferred_element_type=F32).astype(BF16)
        for e in range(E):
            xg_ref[e, 0, pl.ds(rows[e], win), :] += got[e * win:(e + 1) * win, :]

    plans = []
    for sb in range(nsb):
        starts, n_pass = _slot_windows(off_ref, (b * n_bounds + j * nsb + sb) * E, E, cap, win)
        plans.append((starts, n_pass))
        one_pass(sb, starts, 0, True)
    for sb in range(nsb):
        starts, n_pass = plans[sb]

        def extra(p, carry, sb=sb, starts=starts):
            one_pass(sb, starts, p, False)
            return carry

        lax.fori_loop(1, n_pass, extra, 0)


def _dispatch(off_flat, rank, h1b, *, cap, tb, nsb):
    B, E, S = rank.shape
    D = h1b.shape[-1]
    win = min(SLOT_WINDOW, cap)
    ts = tb * nsb
    assert cap % SLOT_ALIGN == 0 and win % SLOT_ALIGN == 0 and S % ts == 0
    return pl.pallas_call(
        functools.partial(_dispatch_kernel, cap=cap, win=win, tb=tb),
        grid_spec=pltpu.PrefetchScalarGridSpec(
            num_scalar_prefetch=1,
            grid=(B, S // ts),
            in_specs=[pl.BlockSpec((1, E, ts), lambda b, j, off: (b, 0, j)),
                      pl.BlockSpec((1, ts, D), lambda b, j, off: (b, j, 0))],
            out_specs=pl.BlockSpec((E, 1, cap, D), lambda b, j, off: (0, b, 0, 0)),
        ),
        out_shape=jax.ShapeDtypeStruct((E, B, cap, D), BF16),
        compiler_params=pltpu.CompilerParams(
            dimension_semantics=("arbitrary", "arbitrary"), vmem_limit_bytes=VMEM_LIMIT_BYTES),
        name="dispatch",
    )(off_flat, rank, h1b)


EXPERT_ROW_BLOCK = 512


def _expert_kernel(x_ref, wg_ref, wu_ref, wd_ref, y_ref, acc_ref, wgb_ref, wub_ref, wdb_ref, *, n_chunks):
    f = pl.program_id(1)
    rows = x_ref.shape[1]
    rb = min(rows, EXPERT_ROW_BLOCK)
    last = n_chunks - 1

    def chunk(first, final):
        wgb_ref[...] = wg_ref[0].astype(BF16)
        wub_ref[...] = wu_ref[0].astype(BF16)
        wdb_ref[...] = wd_ref[0].astype(BF16)
        for r0 in range(0, rows, rb):
            x = x_ref[0, r0:r0 + rb, :]
            a = jnp.dot(x, wgb_ref[...], preferred_element_type=F32)
            u = jnp.dot(x, wub_ref[...], preferred_element_type=F32)
            hmid = (a * (1.0 / (1.0 + jnp.exp(-a))) * u).astype(BF16)
            part = jnp.dot(hmid, wdb_ref[...], preferred_element_type=F32)
            if not first:
                part = acc_ref[r0:r0 + rb, :] + part
            if final:
                y_ref[0, r0:r0 + rb, :] = part.astype(BF16)
            else:
                acc_ref[r0:r0 + rb, :] = part

    if n_chunks == 1:
        chunk(True, True)
    else:
        pl.when(f == 0)(lambda: chunk(True, False))
        pl.when(jnp.logical_and(f > 0, f < last))(lambda: chunk(False, False))
        pl.when(f == last)(lambda: chunk(False, True))


def _experts(xg, w_gate, w_up, w_down, *, tf):
    E, rows, D = xg.shape
    F = w_gate.shape[-1]
    return pl.pallas_call(
        functools.partial(_expert_kernel, n_chunks=F // tf),
        grid=(E, F // tf),
        in_specs=[
            pl.BlockSpec((1, rows, D), lambda e, f: (e, 0, 0)),
            pl.BlockSpec((1, D, tf), lambda e, f: (e, 0, f)),
            pl.BlockSpec((1, D, tf), lambda e, f: (e, 0, f)),
            pl.BlockSpec((1, tf, D), lambda e, f: (e, f, 0)),
        ],
        out_specs=pl.BlockSpec((1, rows, D), lambda e, f: (e, 0, 0)),
        out_shape=jax.ShapeDtypeStruct((E, rows, D), BF16),
        scratch_shapes=[pltpu.VMEM((rows, D), F32), pltpu.VMEM((D, tf), BF16), pltpu.VMEM((D, tf), BF16),
                        pltpu.VMEM((tf, D), BF16)],
        compiler_params=pltpu.CompilerParams(
            dimension_semantics=("arbitrary", "arbitrary"), vmem_limit_bytes=VMEM_LIMIT_BYTES),
        name="experts",
    )(xg, w_gate, w_up, w_down)


def _combine_kernel(off_ref, rank_ref, gate_ref, y_ref, h_ref, g2_ref, b2_ref, o_ref, *, cap, win, tb):
    E = rank_ref.shape[1]
    nsb = rank_ref.shape[2] // tb
    b, j = pl.program_id(0), pl.program_id(1)
    n_bounds = pl.num_programs(1) * nsb + 1
    per_tile = LANES // win
    lane = lax.broadcasted_iota(jnp.int32, (1, LANES), 1)
    first_lane = lax.broadcasted_iota(jnp.int32, (LANES, E * win), 0) * win
    col = lax.broadcasted_iota(jnp.int32, (LANES, E * win), 1)
    spread = jnp.where(jnp.logical_and(col >= first_lane, col < first_lane + win), 1.0, 0.0).astype(BF16)
    pad_rows = jnp.zeros((LANES - E, tb), F32)

    def token_major(x_et):
        return jnp.concatenate([x_et, pad_rows], axis=0).T.astype(BF16)

    def one_pass(sb, starts, p, first):
        cols = slice(sb * tb, (sb + 1) * tb)
        rk = jnp.dot(token_major(rank_ref[0, :, cols].astype(F32)), spread, preferred_element_type=F32)
        gt = jnp.dot(token_major(gate_ref[0, :, cols]), spread, preferred_element_type=F32)
        tgts, los, ys = [], [], []
        for t in range(E // per_tile):
            tgt = lo_v = None
            for u in range(per_tile):
                e = t * per_tile + u
                lo = starts[e] + p * win
                ws = pl.multiple_of(jnp.minimum(lo, cap - win), SLOT_ALIGN)
                ys.append(y_ref[e, 0, pl.ds(ws, win), :])
                t_u = ws + lane - u * win
                if u == 0:
                    tgt, lo_v = t_u, jnp.full((1, LANES), lo, jnp.int32)
                else:
                    here = lane >= u * win
                    tgt = jnp.where(here, t_u, tgt)
                    lo_v = jnp.where(here, lo, lo_v)
            tgts.append(tgt)
            los.append(lo_v)
        tgt = jnp.concatenate(tgts, axis=1)
        hit = rk == tgt.astype(F32)
        if not first:
            hit = jnp.logical_and(hit, tgt >= jnp.concatenate(los, axis=1))
        gates = jnp.where(hit, gt, 0.0).astype(BF16)
        return jnp.dot(gates, jnp.concatenate(ys, axis=0), preferred_element_type=F32)

    plans = []
    for sb in range(nsb):
        rows = slice(sb * tb, (sb + 1) * tb)
        starts, n_pass = _slot_windows(off_ref, (b * n_bounds + j * nsb + sb) * E, E, cap, win)
        plans.append((starts, n_pass))
        o_ref[0, rows, :] = DEEPNORM_ALPHA * h_ref[0, rows, :] + one_pass(sb, starts, 0, True)
    for sb in range(nsb):
        starts, n_pass = plans[sb]

        def extra(p, carry, sb=sb, starts=starts):
            o_ref[0, sb * tb:(sb + 1) * tb, :] += one_pass(sb, starts, p, False)
            return carry

        lax.fori_loop(1, n_pass, extra, 0)
    o_ref[0] = _layer_norm(o_ref[0], g2_ref[...], b2_ref[...])


def _combine(off_flat, rank, gate, y, h1, g2, b2, *, tb, nsb):
    B, E, S = rank.shape
    cap, D = y.shape[2], y.shape[3]
    win = min(SLOT_WINDOW, cap)
    ts = tb * nsb
    assert cap % SLOT_ALIGN == 0 and win % SLOT_ALIGN == 0 and LANES % win == 0 and E % (LANES // win) == 0
    assert S % ts == 0
    full = lambda a: pl.BlockSpec(a.shape, lambda b, j, off: (0,) * a.ndim)
    return pl.pallas_call(
        functools.partial(_combine_kernel, cap=cap, win=win, tb=tb),
        grid_spec=pltpu.PrefetchScalarGridSpec(
            num_scalar_prefetch=1,
            grid=(B, S // ts),
            in_specs=[
                pl.BlockSpec((1, E, ts), lambda b, j, off: (b, 0, j)),
                pl.BlockSpec((1, E, ts), lambda b, j, off: (b, 0, j)),
                pl.BlockSpec((E, 1, cap, D), lambda b, j, off: (0, b, 0, 0)),
                pl.BlockSpec((1, ts, D), lambda b, j, off: (b, j, 0)),
                full(g2), full(b2),
            ],
            out_specs=pl.BlockSpec((1, ts, D), lambda b, j, off: (b, j, 0)),
        ),
        out_shape=jax.ShapeDtypeStruct((B, S, D), F32),
        compiler_params=pltpu.CompilerParams(
            dimension_semantics=("arbitrary", "arbitrary"), vmem_limit_bytes=VMEM_LIMIT_BYTES),
        name="combine",
    )(off_flat, rank, gate, y, h1, g2, b2)


def _tile(n, target):
    t = min(n, target)
    assert n % t == 0, (n, t)
    return t


def kernel(x, positions, emb_ln_g, emb_ln_b, w_in, q_norm_g, w_qb, kv_norm_g, w_kvb, conv_w, conv_b,
           conv_ln_g, conv_ln_b, w_o, ln1_g, ln1_b, w_router, w_gate, w_up, w_down, ln2_g, ln2_b):
    B, S, D = x.shape
    T = B * S
    H = MLA_HEADS
    q_rank = q_norm_g.shape[-1]
    kv_rank = kv_norm_g.shape[-1]
    conv_ch = conv_w.shape[-1]
    qk_dim = QK_NOPE_DIM + QK_ROPE_DIM
    cap = CAPACITY_FACTOR * S // N_EXPERTS
    assert w_in.shape[0] == DEPTH == 1
    row = lambda a: a.reshape(1, -1)

    wi = w_in[0]
    c1, c2, c3 = q_rank, q_rank + kv_rank, q_rank + kv_rank + QK_ROPE_DIM
    tail = LANES - QK_NOPE_DIM - QK_ROPE_DIM
    kr_cols = jnp.pad(wi[:, c2:c3], ((0, 0), (QK_NOPE_DIM, tail)))
    w_in_r = jnp.concatenate(
        [wi[:, :c2], wi[:, c3:c3 + conv_ch], wi[:, c3 + conv_ch:], kr_cols], axis=1).astype(BF16)
    w_qb_pad = jnp.pad(w_qb[0].reshape(q_rank, H, qk_dim),
                       ((0, 0), (0, 0), (0, HEAD_PAD - qk_dim))).reshape(q_rank, H * HEAD_PAD).astype(BF16)
    wkv = w_kvb[0].reshape(kv_rank, H, QK_NOPE_DIM + V_HEAD_DIM)
    wk_pad = jnp.pad(wkv[:, :, :QK_NOPE_DIM], ((0, 0), (0, 0), (0, HEAD_PAD - QK_NOPE_DIM)))
    w_kv_r = jnp.concatenate([wk_pad.reshape(kv_rank, H * HEAD_PAD),
                              wkv[:, :, QK_NOPE_DIM:].reshape(kv_rank, H * V_HEAD_DIM)], axis=1).astype(BF16)
    half = QK_ROPE_DIM // 2
    inv_freq = (ROPE_THETA ** (-jnp.arange(half, dtype=F32) / half)).reshape(half, 1)

    x2d = x.reshape(T, D)
    tm = _tile(S, 512)
    q, k, v, hc = _in_proj(x2d, positions.reshape(1, T), inv_freq, row(emb_ln_g), row(emb_ln_b), w_in_r,
                           row(q_norm_g[0]), w_qb_pad, row(kv_norm_g[0]), w_kv_r, tm=tm)
    n_ct = conv_ch // LANES
    cv = _conv(hc.reshape(B, S, conv_ch), jnp.swapaxes(conv_w[0].reshape(CONV_WIDTH, n_ct, LANES), 0, 1),
               conv_b[0].reshape(n_ct, 1, LANES), row(conv_ln_g[0]), row(conv_ln_b[0]))
    att = _attn(q.reshape(B, S, H * HEAD_PAD), k.reshape(B, S, H * HEAD_PAD),
                v.reshape(B, S, H * V_HEAD_DIM), tq=_tile(S, 512))
    wr_hi = w_router[0].astype(BF16)
    w_r2 = jnp.concatenate([wr_hi, (w_router[0] - wr_hi.astype(F32)).astype(BF16)], axis=1)
    w_r2 = jnp.pad(w_r2, ((0, 0), (0, LANES - 2 * N_EXPERTS)))
    h1, lt2 = _out_proj(x2d, att.reshape(T, H * V_HEAD_DIM), cv.reshape(T, conv_ch), row(emb_ln_g),
                        row(emb_ln_b), w_o[0].astype(BF16), row(ln1_g[0]), row(ln1_b[0]),
                        w_r2, tm=_tile(S, 1024), seq=S, n_lt=2 * N_EXPERTS)
    h1 = h1.reshape(B, S, D)
    tb = _tile(S, ROUTE_TOKEN_BLOCK)
    rank, gate, off = _route(lt2, cap=cap, tb=tb)
    off_flat = jnp.swapaxes(off[:, :, :S // tb + 1], 1, 2).reshape(-1)
    nsb = _tile(S // tb, ROUTE_BLOCKS_PER_STEP)
    xg = _dispatch(off_flat, rank, h1, cap=cap, tb=tb, nsb=nsb)
    y = _experts(xg.reshape(N_EXPERTS, B * cap, D), w_gate[0], w_up[0], w_down[0],
                 tf=_tile(w_gate.shape[-1], 512)).reshape(N_EXPERTS, B, cap, D)
    return _combine(off_flat, rank, gate, y, h1, row(ln2_g[0]), row(ln2_b[0]), tb=tb, nsb=nsb)
```

```python
import functools

import jax
import jax.numpy as jnp
from jax import lax
from jax.experimental import pallas as pl
from jax.experimental.pallas import tpu as pltpu

F32 = jnp.float32
BF16 = jnp.bfloat16

MLA_HEADS = 8
QK_NOPE_DIM = 64
QK_ROPE_DIM = 32
V_HEAD_DIM = 64
CONV_WIDTH = 31
ROPE_THETA = 10000.0
N_EXPERTS = 16
CAPACITY_FACTOR = 2
DEPTH = 1
DEEPNORM_ALPHA = (2.0 * DEPTH) ** 0.25
LN_EPS = 1e-5
RMS_EPS = 1e-6

LANES = 128
SUBLANES = 8
HEAD_PAD = LANES
VMEM_LIMIT_BYTES = 56 * 1024 * 1024
TOKEN_SUB_TILE = 256

TOPK_SEARCH_STEPS = 36
TOPK_SEARCH_FLOOR = 1e-30


def _layer_norm(x, g, b):
    mu = jnp.mean(x, axis=-1, keepdims=True)
    xc = x - mu
    var = jnp.mean(xc * xc, axis=-1, keepdims=True)
    return xc * lax.rsqrt(var + LN_EPS) * g + b


def _rms_norm(x, g):
    return x * lax.rsqrt(jnp.mean(x * x, axis=-1, keepdims=True) + RMS_EPS) * g


def _rope_tile(x, cos_t, sin_t, lane):
    half = QK_ROPE_DIM // 2
    fwd = pltpu.roll(x, LANES - half, 1)
    bwd = pltpu.roll(x, half, 1)
    partner = jnp.where(lane < QK_NOPE_DIM + half, fwd, bwd)
    return x * cos_t + partner * sin_t


def _in_proj_kernel(x_ref, pos_ref, invf_ref, lng_ref, lnb_ref, win_ref, qg_ref, wqb_ref,
                    kvg_ref, wkv_ref, q_ref, k_ref, v_ref, hc_ref, *, q_rank, kv_rank, conv_ch):
    tm = x_ref.shape[0]
    sub = min(tm, TOKEN_SUB_TILE)
    c1 = q_rank
    c2 = c1 + kv_rank
    c3 = c2 + conv_ch
    c4 = c3 + conv_ch
    tail = LANES - QK_NOPE_DIM - QK_ROPE_DIM
    nk = MLA_HEADS * HEAD_PAD
    scale = (QK_NOPE_DIM + QK_ROPE_DIM) ** -0.5
    ones = jnp.ones((QK_NOPE_DIM, sub), F32)
    zeros = jnp.zeros((QK_NOPE_DIM, sub), F32)
    lane = lax.broadcasted_iota(jnp.int32, (sub, LANES), 1)
    for r0 in range(0, tm, sub):
        rows = slice(r0, r0 + sub)
        h = _layer_norm(x_ref[rows, :], lng_ref[...], lnb_ref[...])
        proj = jnp.dot(h.astype(BF16), win_ref[...], preferred_element_type=F32)
        cq = proj[:, :c1]
        ckv = proj[:, c1:c2]
        a = proj[:, c2:c3]
        g = proj[:, c3:c4]
        kr = proj[:, c4:c4 + LANES]

        ang = pos_ref[:, rows].astype(F32) * invf_ref[...]
        cos = jnp.cos(ang)
        sin = jnp.sin(ang)
        cos_t = jnp.concatenate([ones, cos, cos, ones[:tail]], axis=0).T
        sin_t = jnp.concatenate([zeros, -sin, sin, zeros[:tail]], axis=0).T

        cqn = _rms_norm(cq, qg_ref[...])
        q = jnp.dot(cqn.astype(BF16), wqb_ref[...], preferred_element_type=F32) * scale
        ckvn = _rms_norm(ckv, kvg_ref[...])
        kv = jnp.dot(ckvn.astype(BF16), wkv_ref[...], preferred_element_type=F32)
        k_pe = _rope_tile(kr, cos_t, sin_t, lane)
        for hd in range(MLA_HEADS):
            sl = slice(hd * HEAD_PAD, (hd + 1) * HEAD_PAD)
            q_ref[rows, sl] = _rope_tile(q[:, sl], cos_t, sin_t, lane).astype(BF16)
            k_ref[rows, sl] = (kv[:, sl] + k_pe).astype(BF16)
        v_ref[rows, :] = kv[:, nk:].astype(BF16)
        hc_ref[rows, :] = a * (1.0 / (1.0 + jnp.exp(-g)))


def _in_proj(x2d, pos_row, invf_col, lng, lnb, w_in_r, qg, w_qb_pad, kvg, w_kv_r, *, tm):
    T, D = x2d.shape
    q_rank = w_qb_pad.shape[0]
    kv_rank = w_kv_r.shape[0]
    conv_ch = (w_in_r.shape[1] - q_rank - kv_rank - LANES) // 2
    nk = MLA_HEADS * HEAD_PAD
    nv = MLA_HEADS * V_HEAD_DIM
    full = lambda a: pl.BlockSpec(a.shape, lambda i: (0,) * a.ndim)
    return pl.pallas_call(
        functools.partial(_in_proj_kernel, q_rank=q_rank, kv_rank=kv_rank, conv_ch=conv_ch),
        grid=(T // tm,),
        in_specs=[
            pl.BlockSpec((tm, D), lambda i: (i, 0)),
            pl.BlockSpec((1, tm), lambda i: (0, i)),
            full(invf_col), full(lng), full(lnb), full(w_in_r), full(qg), full(w_qb_pad),
            full(kvg), full(w_kv_r),
        ],
        out_specs=[
            pl.BlockSpec((tm, nk), lambda i: (i, 0)),
            pl.BlockSpec((tm, nk), lambda i: (i, 0)),
            pl.BlockSpec((tm, nv), lambda i: (i, 0)),
            pl.BlockSpec((tm, conv_ch), lambda i: (i, 0)),
        ],
        out_shape=[
            jax.ShapeDtypeStruct((T, nk), BF16),
            jax.ShapeDtypeStruct((T, nk), BF16),
            jax.ShapeDtypeStruct((T, nv), BF16),
            jax.ShapeDtypeStruct((T, conv_ch), F32),
        ],
        compiler_params=pltpu.CompilerParams(
            dimension_semantics=("arbitrary",), vmem_limit_bytes=VMEM_LIMIT_BYTES),
        name="in_proj",
    )(x2d, pos_row, invf_col, lng, lnb, w_in_r, qg, w_qb_pad, kvg, w_kv_r)


CONV_PAD_ROWS = 16
CONV_ROW_CHUNK = 128
CONV_CHUNKS_PER_STEP = 4
CONV_WINDOW = CONV_ROW_CHUNK + 2 * CONV_PAD_ROWS


def _conv_kernel(hc_ref, cw_ref, cb_ref, g_ref, b_ref, o_ref, win_ref, y_ref):
    S, C = hc_ref.shape[1], hc_ref.shape[2]
    n_chunks = S // CONV_ROW_CHUNK
    for r in range(n_chunks):
        lo = r * CONV_ROW_CHUNK - CONV_PAD_ROWS
        hi = lo + CONV_WINDOW
        src_lo, src_hi = max(lo, 0), min(hi, S)
        for c in range(C // LANES):
            if lo < 0:
                win_ref[r, c, 0:-lo, :] = jnp.zeros((-lo, LANES), F32)
            if hi > S:
                win_ref[r, c, CONV_WINDOW - (hi - S):, :] = jnp.zeros((hi - S, LANES), F32)
            win_ref[r, c, src_lo - lo:src_hi - lo, :] = hc_ref[0, src_lo:src_hi, c * LANES:(c + 1) * LANES]
    first = CONV_PAD_ROWS - CONV_WIDTH // 2

    n_tiles = C // LANES
    group = min(CONV_CHUNKS_PER_STEP, n_chunks)

    def chunk_group(rg, carry):
        sums = []
        for u in range(group):
            r = rg * group + u

            def tile_conv(c, s1, r=r, u=u):
                acc = jnp.zeros((CONV_ROW_CHUNK, LANES), F32)
                for j in range(CONV_WIDTH):
                    acc = acc + win_ref[r, c, first + j:first + j + CONV_ROW_CHUNK, :] * cw_ref[c, j:j + 1, :]
                acc = acc + cb_ref[c]
                y_ref[u, c] = acc
                return s1 + acc

            sums.append(lax.fori_loop(0, n_tiles, tile_conv, jnp.zeros((CONV_ROW_CHUNK, LANES), F32)))
        for u in range(group):
            base = pl.multiple_of((rg * group + u) * CONV_ROW_CHUNK, CONV_ROW_CHUNK)
            mu = jnp.sum(sums[u], axis=-1, keepdims=True) * (1.0 / C)
            s2 = jnp.zeros((CONV_ROW_CHUNK, LANES), F32)
            for c in range(n_tiles):
                yc = y_ref[u, c] - mu
                s2 = s2 + yc * yc
            inv = lax.rsqrt(jnp.sum(s2, axis=-1, keepdims=True) * (1.0 / C) + LN_EPS)
            for c in range(n_tiles):
                lanes = slice(c * LANES, (c + 1) * LANES)
                y = (y_ref[u, c] - mu) * inv * g_ref[:, lanes] + b_ref[:, lanes]
                o_ref[0, pl.ds(base, CONV_ROW_CHUNK), lanes] = (y * (1.0 / (1.0 + jnp.exp(-y)))).astype(BF16)
        return carry

    lax.fori_loop(0, n_chunks // group, chunk_group, 0)


def _conv(hc, cw, cb, g, b):
    B, S, C = hc.shape
    full = lambda a: pl.BlockSpec(a.shape, lambda i: (0,) * a.ndim)
    return pl.pallas_call(
        _conv_kernel,
        grid=(B,),
        in_specs=[pl.BlockSpec((1, S, C), lambda i: (i, 0, 0)), full(cw), full(cb), full(g), full(b)],
        out_specs=pl.BlockSpec((1, S, C), lambda i: (i, 0, 0)),
        out_shape=jax.ShapeDtypeStruct((B, S, C), BF16),
        scratch_shapes=[pltpu.VMEM((S // CONV_ROW_CHUNK, C // LANES, CONV_WINDOW, LANES), F32),
                        pltpu.VMEM((CONV_CHUNKS_PER_STEP, C // LANES, CONV_ROW_CHUNK, LANES), F32)],
        compiler_params=pltpu.CompilerParams(
            dimension_semantics=("arbitrary",), vmem_limit_bytes=VMEM_LIMIT_BYTES),
        name="conv",
    )(hc, cw, cb, g, b)


HEADS_PER_TILE = LANES // V_HEAD_DIM
ATTN_HEADS_PER_STEP = 4


ATTN_SLOTS = 2


def _attn_kernel(q_ref, k_ref, v_ref, o_ref, s_ref, p_ref, *, tq):
    S = q_ref.shape[1]
    lane = lax.broadcasted_iota(jnp.int32, (tq, LANES), 1)
    item = 0
    for pr in range(v_ref.shape[2] // LANES):
        v = v_ref[0, :, pr * LANES:(pr + 1) * LANES]
        for qi in range(S // tq):
            rows = slice(qi * tq, (qi + 1) * tq)
            outs = []
            for hh in range(HEADS_PER_TILE):
                slot = item % ATTN_SLOTS
                item += 1
                c0 = (pr * HEADS_PER_TILE + hh) * HEAD_PAD
                s_ref[slot] = lax.dot_general(q_ref[0, rows, c0:c0 + HEAD_PAD], k_ref[0, :, c0:c0 + HEAD_PAD],
                                              (((1,), (1,)), ((), ())), preferred_element_type=F32)
                s = s_ref[slot]
                p = jnp.exp(s - jnp.max(s, axis=-1, keepdims=True))
                l = jnp.sum(p, axis=-1, keepdims=True)
                p_ref[slot] = p.astype(BF16)
                o = jnp.dot(p_ref[slot], v, preferred_element_type=F32)
                outs.append(o * (1.0 / l))
            o_ref[0, rows, pr * LANES:(pr + 1) * LANES] = jnp.where(lane < V_HEAD_DIM, outs[0], outs[1]).astype(BF16)


def _attn(q, k, v, *, tq):
    B, S, _ = q.shape
    qw = ATTN_HEADS_PER_STEP * HEAD_PAD
    vw = ATTN_HEADS_PER_STEP * V_HEAD_DIM
    return pl.pallas_call(
        functools.partial(_attn_kernel, tq=tq),
        grid=(B, MLA_HEADS // ATTN_HEADS_PER_STEP),
        in_specs=[
            pl.BlockSpec((1, S, qw), lambda b, h: (b, 0, h)),
            pl.BlockSpec((1, S, qw), lambda b, h: (b, 0, h)),
            pl.BlockSpec((1, S, vw), lambda b, h: (b, 0, h)),
        ],
        out_specs=pl.BlockSpec((1, S, vw), lambda b, h: (b, 0, h)),
        out_shape=jax.ShapeDtypeStruct((B, S, MLA_HEADS * V_HEAD_DIM), BF16),
        scratch_shapes=[pltpu.VMEM((ATTN_SLOTS, tq, S), F32), pltpu.VMEM((ATTN_SLOTS, tq, S), BF16)],
        compiler_params=pltpu.CompilerParams(
            dimension_semantics=("arbitrary", "arbitrary"), vmem_limit_bytes=62 * 1024 * 1024),
        name="attn",
    )(q, k, v)


def _out_proj_kernel(x_ref, att_ref, cv_ref, lng_ref, lnb_ref, wo_ref, g1_ref, b1_ref, wr_ref,
                     h1_ref, lt_ref):
    na = att_ref.shape[1]
    tm = x_ref.shape[0]
    n_lt = lt_ref.shape[1]
    sub = min(tm, TOKEN_SUB_TILE)
    for r0 in range(0, tm, sub):
        rows = slice(r0, r0 + sub)
        h0 = _layer_norm(x_ref[rows, :], lng_ref[...], lnb_ref[...])
        mix = jnp.dot(att_ref[rows, :], wo_ref[:na, :], preferred_element_type=F32)
        mix = mix + jnp.dot(cv_ref[rows, :], wo_ref[na:, :], preferred_element_type=F32)
        h1 = _layer_norm(DEEPNORM_ALPHA * h0 + mix, g1_ref[...], b1_ref[...])
        h1_ref[rows, :] = h1
        lg = jnp.dot(h1.astype(BF16), wr_ref[...], preferred_element_type=F32)
        lt_ref[0, :, rows] = lg.T[:n_lt, :]


def _out_proj(x2d, att, cv, lng, lnb, w_o, g1, b1, w_r2, *, tm, seq, n_lt):
    T, D = x2d.shape
    per = seq // tm
    full = lambda a: pl.BlockSpec(a.shape, lambda i: (0,) * a.ndim)
    return pl.pallas_call(
        _out_proj_kernel,
        grid=(T // tm,),
        in_specs=[
            pl.BlockSpec((tm, D), lambda i: (i, 0)),
            pl.BlockSpec((tm, att.shape[1]), lambda i: (i, 0)),
            pl.BlockSpec((tm, cv.shape[1]), lambda i: (i, 0)),
            full(lng), full(lnb), full(w_o), full(g1), full(b1), full(w_r2),
        ],
        out_specs=[
            pl.BlockSpec((tm, D), lambda i: (i, 0)),
            pl.BlockSpec((1, n_lt, tm), lambda i: (i // per, 0, i % per)),
        ],
        out_shape=[
            jax.ShapeDtypeStruct((T, D), F32),
            jax.ShapeDtypeStruct((T // seq, n_lt, seq), F32),
        ],
        compiler_params=pltpu.CompilerParams(
            dimension_semantics=("arbitrary",), vmem_limit_bytes=VMEM_LIMIT_BYTES),
        name="out_proj",
    )(x2d, att, cv, lng, lnb, w_o, g1, b1, w_r2)


def _route_kernel(lt_ref, rank_ref, gate_ref, off_ref, *, cap, tb):
    B, E, S = rank_ref.shape
    lt = lt_ref[:, :E, :] + lt_ref[:, E:, :]
    ex = jnp.exp(lt - jnp.max(lt, axis=1, keepdims=True))
    aff = ex / jnp.sum(ex, axis=1, keepdims=True)
    v = aff.reshape(B * E, S)
    rows = B * E
    kf = float(cap)

    def step(_, carry):
        lo, hi = carry
        mid = jnp.sqrt(jnp.maximum(lo, TOPK_SEARCH_FLOOR)) * jnp.sqrt(hi)
        cnt = jnp.sum(jnp.where(v >= mid, 1.0, 0.0), axis=1, keepdims=True)
        ge = cnt >= kf
        return jnp.where(ge, mid, lo), jnp.where(ge, hi, mid)

    lo, hi = lax.fori_loop(0, TOPK_SEARCH_STEPS, step,
                           (jnp.zeros((rows, 1), F32), jnp.full((rows, 1), 2.0, F32)))
    above = v >= hi
    tie = jnp.logical_and(v >= lo, jnp.logical_not(above))
    stacked = jnp.concatenate([jnp.where(above, 1.0, 0.0), jnp.where(tie, 1.0, 0.0)], axis=0)
    upper = jnp.where(lax.broadcasted_iota(jnp.int32, (S, S), 0) <= lax.broadcasted_iota(jnp.int32, (S, S), 1),
                      1.0, 0.0).astype(BF16)
    pc = jnp.dot(stacked.astype(BF16), upper, preferred_element_type=F32)
    pa = pc[:rows]
    pt = pc[rows:]
    need = kf - jnp.sum(jnp.where(above, 1.0, 0.0), axis=1, keepdims=True)
    sel = jnp.logical_or(above, jnp.logical_and(tie, pt <= need))
    taken = pa + jnp.minimum(pt, need)
    rank_ref[...] = jnp.where(sel, taken - 1.0, -1.0).astype(jnp.int32).reshape(B, E, S)
    gate_ref[...] = jnp.where(sel, v, 0.0).reshape(B, E, S)
    pick = jnp.where(lax.broadcasted_iota(jnp.int32, (S, LANES), 0) + 1
                     == lax.broadcasted_iota(jnp.int32, (S, LANES), 1) * tb, 1.0, 0.0).astype(BF16)
    off = jnp.dot(taken.astype(BF16), pick, preferred_element_type=F32)
    off_ref[...] = off.astype(jnp.int32).reshape(B, E, LANES)


def _route(lt2, *, cap, tb):
    B, E2, S = lt2.shape
    E = E2 // 2
    assert cap <= 256 and S // tb < LANES
    spec = pl.BlockSpec((B, E, S), lambda i: (0, 0, 0))
    ospec = pl.BlockSpec((B, E, LANES), lambda i: (0, 0, 0))
    return pl.pallas_call(
        functools.partial(_route_kernel, cap=cap, tb=tb),
        grid=(1,),
        in_specs=[pl.BlockSpec((B, E2, S), lambda i: (0, 0, 0))],
        out_specs=[spec, spec, ospec],
        out_shape=[jax.ShapeDtypeStruct((B, E, S), jnp.int32), jax.ShapeDtypeStruct((B, E, S), F32),
                   jax.ShapeDtypeStruct((B, E, LANES), jnp.int32)],
        compiler_params=pltpu.CompilerParams(
            dimension_semantics=("arbitrary",), vmem_limit_bytes=VMEM_LIMIT_BYTES),
        name="route",
    )(lt2)


ROUTE_TOKEN_BLOCK = 256
ROUTE_BLOCKS_PER_STEP = 4
SLOT_WINDOW = 64
SLOT_ALIGN = 16


def _slot_windows(off_ref, base, n_experts, cap, win):
    starts = []
    n_pass = jnp.int32(1)
    for e in range(n_experts):
        off = off_ref[base + e]
        end = off_ref[base + n_experts + e]
        start = jnp.minimum((off // SLOT_ALIGN) * SLOT_ALIGN, cap - win)
        starts.append(start)
        n_pass = jnp.maximum(n_pass, (end - start + (win - 1)) // win)
    return starts, n_pass


def _dispatch_kernel(off_ref, rank_ref, h_ref, xg_ref, *, cap, win, tb):
    E = rank_ref.shape[1]
    nsb = rank_ref.shape[2] // tb
    b, j = pl.program_id(0), pl.program_id(1)
    n_bounds = pl.num_programs(1) * nsb + 1

    @pl.when(j == 0)
    def _():
        xg_ref[...] = jnp.zeros(xg_ref.shape, BF16)

    sub = lax.broadcasted_iota(jnp.int32, (win, tb), 0)

    def one_pass(sb, starts, p, first):
        cols = slice(sb * tb, (sb + 1) * tb)
        pieces, rows = [], []
        for e in range(E):
            lo = starts[e] + p * win
            ws = pl.multiple_of(jnp.minimum(lo, cap - win), SLOT_ALIGN)
            tgt = ws + sub
            hit = rank_ref[0, e:e + 1, cols] == tgt
            if not first:
                hit = jnp.logical_and(hit, tgt >= lo)
            pieces.append(jnp.where(hit, 1.0, 0.0).astype(BF16))
            rows.append(ws)
        onehot = jnp.concatenate(pieces, axis=0)
        hb = h_ref[0, cols, :].astype(BF16)
        got = jnp.dot(onehot, hb, preferred_element_type=F32).astype(BF16)
        for e in range(E):
            xg_ref[e, 0, pl.ds(rows[e], win), :] += got[e * win:(e + 1) * win, :]

    plans = []
    for sb in range(nsb):
        starts, n_pass = _slot_windows(off_ref, (b * n_bounds + j * nsb + sb) * E, E, cap, win)
        plans.append((starts, n_pass))
        one_pass(sb, starts, 0, True)
    for sb in range(nsb):
        starts, n_pass = plans[sb]

        def extra(p, carry, sb=sb, starts=starts):
            one_pass(sb, starts, p, False)
            return carry

        lax.fori_loop(1, n_pass, extra, 0)


def _dispatch(off_flat, rank, h1b, *, cap, tb, nsb):
    B, E, S = rank.shape
    D = h1b.shape[-1]
    win = min(SLOT_WINDOW, cap)
    ts = tb * nsb
    assert cap % SLOT_ALIGN == 0 and win % SLOT_ALIGN == 0 and S % ts == 0
    return pl.pallas_call(
        functools.partial(_dispatch_kernel, cap=cap, win=win, tb=tb),
        grid_spec=pltpu.PrefetchScalarGridSpec(
            num_scalar_prefetch=1,
            grid=(B, S // ts),
            in_specs=[pl.BlockSpec((1, E, ts), lambda b, j, off: (b, 0, j)),
                      pl.BlockSpec((1, ts, D), lambda b, j, off: (b, j, 0))],
            out_specs=pl.BlockSpec((E, 1, cap, D), lambda b, j, off: (0, b, 0, 0)),
        ),
        out_shape=jax.ShapeDtypeStruct((E, B, cap, D), BF16),
        compiler_params=pltpu.CompilerParams(
            dimension_semantics=("arbitrary", "arbitrary"), vmem_limit_bytes=VMEM_LIMIT_BYTES),
        name="dispatch",
    )(off_flat, rank, h1b)


EXPERT_ROW_BLOCK = 512


def _expert_kernel(x_ref, wg_ref, wu_ref, wd_ref, y_ref, acc_ref, wgb_ref, wub_ref, wdb_ref, *, n_chunks):
    f = pl.program_id(1)
    rows = x_ref.shape[1]
    rb = min(rows, EXPERT_ROW_BLOCK)
    last = n_chunks - 1

    def chunk(first, final):
        wgb_ref[...] = wg_ref[0].astype(BF16)
        wub_ref[...] = wu_ref[0].astype(BF16)
        wdb_ref[...] = wd_ref[0].astype(BF16)
        for r0 in range(0, rows, rb):
            x = x_ref[0, r0:r0 + rb, :]
            a = jnp.dot(x, wgb_ref[...], preferred_element_type=F32)
            u = jnp.dot(x, wub_ref[...], preferred_element_type=F32)
            hmid = (a * (1.0 / (1.0 + jnp.exp(-a))) * u).astype(BF16)
            part = jnp.dot(hmid, wdb_ref[...], preferred_element_type=F32)
            if not first:
                part = acc_ref[r0:r0 + rb, :] + part
            if final:
                y_ref[0, r0:r0 + rb, :] = part.astype(BF16)
            else:
                acc_ref[r0:r0 + rb, :] = part

    if n_chunks == 1:
        chunk(True, True)
    else:
        pl.when(f == 0)(lambda: chunk(True, False))
        pl.when(jnp.logical_and(f > 0, f < last))(lambda: chunk(False, False))
        pl.when(f == last)(lambda: chunk(False, True))


def _experts(xg, w_gate, w_up, w_down, *, tf):
    E, rows, D = xg.shape
    F = w_gate.shape[-1]
    return pl.pallas_call(
        functools.partial(_expert_kernel, n_chunks=F // tf),
        grid=(E, F // tf),
        in_specs=[
            pl.BlockSpec((1, rows, D), lambda e, f: (e, 0, 0)),
            pl.BlockSpec((1, D, tf), lambda e, f: (e, 0, f)),
            pl.BlockSpec((1, D, tf), lambda e, f: (e, 0, f)),
            pl.BlockSpec((1, tf, D), lambda e, f: (e, f, 0)),
        ],
        out_specs=pl.BlockSpec((1, rows, D), lambda e, f: (e, 0, 0)),
        out_shape=jax.ShapeDtypeStruct((E, rows, D), BF16),
        scratch_shapes=[pltpu.VMEM((rows, D), F32), pltpu.VMEM((D, tf), BF16), pltpu.VMEM((D, tf), BF16),
                        pltpu.VMEM((tf, D), BF16)],
        compiler_params=pltpu.CompilerParams(
            dimension_semantics=("arbitrary", "arbitrary"), vmem_limit_bytes=VMEM_LIMIT_BYTES),
        name="experts",
    )(xg, w_gate, w_up, w_down)


def _combine_kernel(off_ref, rank_ref, gate_ref, y_ref, h_ref, g2_ref, b2_ref, o_ref, *, cap, win, tb):
    E = rank_ref.shape[1]
    nsb = rank_ref.shape[2] // tb
    b, j = pl.program_id(0), pl.program_id(1)
    n_bounds = pl.num_programs(1) * nsb + 1
    per_tile = LANES // win
    lane = lax.broadcasted_iota(jnp.int32, (1, LANES), 1)
    first_lane = lax.broadcasted_iota(jnp.int32, (LANES, E * win), 0) * win
    col = lax.broadcasted_iota(jnp.int32, (LANES, E * win), 1)
    spread = jnp.where(jnp.logical_and(col >= first_lane, col < first_lane + win), 1.0, 0.0).astype(BF16)
    pad_rows = jnp.zeros((LANES - E, tb), F32)

    def token_major(x_et):
        return jnp.concatenate([x_et, pad_rows], axis=0).T.astype(BF16)

    def one_pass(sb, starts, p, first):
        cols = slice(sb * tb, (sb + 1) * tb)
        rk = jnp.dot(token_major(rank_ref[0, :, cols].astype(F32)), spread, preferred_element_type=F32)
        gt = jnp.dot(token_major(gate_ref[0, :, cols]), spread, preferred_element_type=F32)
        tgts, los, ys = [], [], []
        for t in range(E // per_tile):
            tgt = lo_v = None
            for u in range(per_tile):
                e = t * per_tile + u
                lo = starts[e] + p * win
                ws = pl.multiple_of(jnp.minimum(lo, cap - win), SLOT_ALIGN)
                ys.append(y_ref[e, 0, pl.ds(ws, win), :])
                t_u = ws + lane - u * win
                if u == 0:
                    tgt, lo_v = t_u, jnp.full((1, LANES), lo, jnp.int32)
                else:
                    here = lane >= u * win
                    tgt = jnp.where(here, t_u, tgt)
                    lo_v = jnp.where(here, lo, lo_v)
            tgts.append(tgt)
            los.append(lo_v)
        tgt = jnp.concatenate(tgts, axis=1)
        hit = rk == tgt.astype(F32)
        if not first:
            hit = jnp.logical_and(hit, tgt >= jnp.concatenate(los, axis=1))
        gates = jnp.where(hit, gt, 0.0).astype(BF16)
        return jnp.dot(gates, jnp.concatenate(ys, axis=0), preferred_element_type=F32)

    plans = []
    for sb in range(nsb):
        rows = slice(sb * tb, (sb + 1) * tb)
        starts, n_pass = _slot_windows(off_ref, (b * n_bounds + j * nsb + sb) * E, E, cap, win)
        plans.append((starts, n_pass))
        o_ref[0, rows, :] = DEEPNORM_ALPHA * h_ref[0, rows, :] + one_pass(sb, starts, 0, True)
    for sb in range(nsb):
        starts, n_pass = plans[sb]

        def extra(p, carry, sb=sb, starts=starts):
            o_ref[0, sb * tb:(sb + 1) * tb, :] += one_pass(sb, starts, p, False)
            return carry

        lax.fori_loop(1, n_pass, extra, 0)
    o_ref[0] = _layer_norm(o_ref[0], g2_ref[...], b2_ref[...])


def _combine(off_flat, rank, gate, y, h1, g2, b2, *, tb, nsb):
    B, E, S = rank.shape
    cap, D = y.shape[2], y.shape[3]
    win = min(SLOT_WINDOW, cap)
    ts = tb * nsb
    assert cap % SLOT_ALIGN == 0 and win % SLOT_ALIGN == 0 and LANES % win == 0 and E % (LANES // win) == 0
    assert S % ts == 0
    full = lambda a: pl.BlockSpec(a.shape, lambda b, j, off: (0,) * a.ndim)
    return pl.pallas_call(
        functools.partial(_combine_kernel, cap=cap, win=win, tb=tb),
        grid_spec=pltpu.PrefetchScalarGridSpec(
            num_scalar_prefetch=1,
            grid=(B, S // ts),
            in_specs=[
                pl.BlockSpec((1, E, ts), lambda b, j, off: (b, 0, j)),
                pl.BlockSpec((1, E, ts), lambda b, j, off: (b, 0, j)),
                pl.BlockSpec((E, 1, cap, D), lambda b, j, off: (0, b, 0, 0)),
                pl.BlockSpec((1, ts, D), lambda b, j, off: (b, j, 0)),
                full(g2), full(b2),
            ],
            out_specs=pl.BlockSpec((1, ts, D), lambda b, j, off: (b, j, 0)),
        ),
        out_shape=jax.ShapeDtypeStruct((B, S, D), F32),
        compiler_params=pltpu.CompilerParams(
            dimension_semantics=("arbitrary", "arbitrary"), vmem_limit_bytes=VMEM_LIMIT_BYTES),
        name="combine",
    )(off_flat, rank, gate, y, h1, g2, b2)


def _tile(n, target):
    t = min(n, target)
    assert n % t == 0, (n, t)
    return t


def kernel(x, positions, emb_ln_g, emb_ln_b, w_in, q_norm_g, w_qb, kv_norm_g, w_kvb, conv_w, conv_b,
           conv_ln_g, conv_ln_b, w_o, ln1_g, ln1_b, w_router, w_gate, w_up, w_down, ln2_g, ln2_b):
    B, S, D = x.shape
    T = B * S
    H = MLA_HEADS
    q_rank = q_norm_g.shape[-1]
    kv_rank = kv_norm_g.shape[-1]
    conv_ch = conv_w.shape[-1]
    qk_dim = QK_NOPE_DIM + QK_ROPE_DIM
    cap = CAPACITY_FACTOR * S // N_EXPERTS
    assert w_in.shape[0] == DEPTH == 1
    row = lambda a: a.reshape(1, -1)

    wi = w_in[0]
    c1, c2, c3 = q_rank, q_rank + kv_rank, q_rank + kv_rank + QK_ROPE_DIM
    tail = LANES - QK_NOPE_DIM - QK_ROPE_DIM
    kr_cols = jnp.pad(wi[:, c2:c3], ((0, 0), (QK_NOPE_DIM, tail)))
    w_in_r = jnp.concatenate(
        [wi[:, :c2], wi[:, c3:c3 + conv_ch], wi[:, c3 + conv_ch:], kr_cols], axis=1).astype(BF16)
    w_qb_pad = jnp.pad(w_qb[0].reshape(q_rank, H, qk_dim),
                       ((0, 0), (0, 0), (0, HEAD_PAD - qk_dim))).reshape(q_rank, H * HEAD_PAD).astype(BF16)
    wkv = w_kvb[0].reshape(kv_rank, H, QK_NOPE_DIM + V_HEAD_DIM)
    wk_pad = jnp.pad(wkv[:, :, :QK_NOPE_DIM], ((0, 0), (0, 0), (0, HEAD_PAD - QK_NOPE_DIM)))
    w_kv_r = jnp.concatenate([wk_pad.reshape(kv_rank, H * HEAD_PAD),
                              wkv[:, :, QK_NOPE_DIM:].reshape(kv_rank, H * V_HEAD_DIM)], axis=1).astype(BF16)
    half = QK_ROPE_DIM // 2
    inv_freq = (ROPE_THETA ** (-jnp.arange(half, dtype=F32) / half)).reshape(half, 1)

    x2d = x.reshape(T, D)
    tm = _tile(S, 1024)
    q, k, v, hc = _in_proj(x2d, positions.reshape(1, T), inv_freq, row(emb_ln_g), row(emb_ln_b), w_in_r,
                           row(q_norm_g[0]), w_qb_pad, row(kv_norm_g[0]), w_kv_r, tm=tm)
    n_ct = conv_ch // LANES
    cv = _conv(hc.reshape(B, S, conv_ch), jnp.swapaxes(conv_w[0].reshape(CONV_WIDTH, n_ct, LANES), 0, 1),
               conv_b[0].reshape(n_ct, 1, LANES), row(conv_ln_g[0]), row(conv_ln_b[0]))
    att = _attn(q.reshape(B, S, H * HEAD_PAD), k.reshape(B, S, H * HEAD_PAD),
                v.reshape(B, S, H * V_HEAD_DIM), tq=_tile(S, 512))
    wr_hi = w_router[0].astype(BF16)
    w_r2 = jnp.concatenate([wr_hi, (w_router[0] - wr_hi.astype(F32)).astype(BF16)], axis=1)
    w_r2 = jnp.pad(w_r2, ((0, 0), (0, LANES - 2 * N_EXPERTS)))
    h1, lt2 = _out_proj(x2d, att.reshape(T, H * V_HEAD_DIM), cv.reshape(T, conv_ch), row(emb_ln_g),
                        row(emb_ln_b), w_o[0].astype(BF16), row(ln1_g[0]), row(ln1_b[0]),
                        w_r2, tm=_tile(S, 2048), seq=S, n_lt=2 * N_EXPERTS)
    h1 = h1.reshape(B, S, D)
    tb = _tile(S, ROUTE_TOKEN_BLOCK)
    rank, gate, off = _route(lt2, cap=cap, tb=tb)
    off_flat = jnp.swapaxes(off[:, :, :S // tb + 1], 1, 2).reshape(-1)
    nsb = _tile(S // tb, ROUTE_BLOCKS_PER_STEP)
    xg = _dispatch(off_flat, rank, h1, cap=cap, tb=tb, nsb=nsb)
    y = _experts(xg.reshape(N_EXPERTS, B * cap, D), w_gate[0], w_up[0], w_down[0],
                 tf=_tile(w_gate.shape[-1], 512)).reshape(N_EXPERTS, B, cap, D)
    return _combine(off_flat, rank, gate, y, h1, row(ln2_g[0]), row(ln2_b[0]), tb=tb, nsb=nsb)
```

```python
import functools

import jax
import jax.numpy as jnp
from jax import lax
from jax.experimental import pallas as pl
from jax.experimental.pallas import tpu as pltpu

F32 = jnp.float32
BF16 = jnp.bfloat16

MLA_HEADS = 8
QK_NOPE_DIM = 64
QK_ROPE_DIM = 32
V_HEAD_DIM = 64
CONV_WIDTH = 31
ROPE_THETA = 10000.0
N_EXPERTS = 16
CAPACITY_FACTOR = 2
DEPTH = 1
DEEPNORM_ALPHA = (2.0 * DEPTH) ** 0.25
LN_EPS = 1e-5
RMS_EPS = 1e-6

LANES = 128
SUBLANES = 8
HEAD_PAD = LANES
VMEM_LIMIT_BYTES = 56 * 1024 * 1024
TOKEN_SUB_TILE = 256

TOPK_SEARCH_STEPS = 36
TOPK_SEARCH_FLOOR = 1e-30


def _layer_norm(x, g, b):
    mu = jnp.mean(x, axis=-1, keepdims=True)
    xc = x - mu
    var = jnp.mean(xc * xc, axis=-1, keepdims=True)
    return xc * lax.rsqrt(var + LN_EPS) * g + b


def _rms_norm(x, g):
    return x * lax.rsqrt(jnp.mean(x * x, axis=-1, keepdims=True) + RMS_EPS) * g


def _rope_tile(x, cos_t, sin_t, lane):
    half = QK_ROPE_DIM // 2
    fwd = pltpu.roll(x, LANES - half, 1)
    bwd = pltpu.roll(x, half, 1)
    partner = jnp.where(lane < QK_NOPE_DIM + half, fwd, bwd)
    return x * cos_t + partner * sin_t


def _in_proj_kernel(x_ref, pos_ref, invf_ref, lng_ref, lnb_ref, win_ref, qg_ref, wqb_ref,
                    kvg_ref, wkv_ref, q_ref, k_ref, v_ref, hc_ref, *, q_rank, kv_rank, conv_ch):
    tm = x_ref.shape[0]
    sub = min(tm, TOKEN_SUB_TILE)
    c1 = q_rank
    c2 = c1 + kv_rank
    c3 = c2 + conv_ch
    c4 = c3 + conv_ch
    tail = LANES - QK_NOPE_DIM - QK_ROPE_DIM
    nk = MLA_HEADS * HEAD_PAD
    scale = (QK_NOPE_DIM + QK_ROPE_DIM) ** -0.5
    ones = jnp.ones((QK_NOPE_DIM, sub), F32)
    zeros = jnp.zeros((QK_NOPE_DIM, sub), F32)
    lane = lax.broadcasted_iota(jnp.int32, (sub, LANES), 1)
    for r0 in range(0, tm, sub):
        rows = slice(r0, r0 + sub)
        h = _layer_norm(x_ref[rows, :], lng_ref[...], lnb_ref[...])
        proj = jnp.dot(h.astype(BF16), win_ref[...], preferred_element_type=F32)
        cq = proj[:, :c1]
        ckv = proj[:, c1:c2]
        a = proj[:, c2:c3]
        g = proj[:, c3:c4]
        kr = proj[:, c4:c4 + LANES]

        ang = pos_ref[:, rows].astype(F32) * invf_ref[...]
        cos = jnp.cos(ang)
        sin = jnp.sin(ang)
        cos_t = jnp.concatenate([ones, cos, cos, ones[:tail]], axis=0).T
        sin_t = jnp.concatenate([zeros, -sin, sin, zeros[:tail]], axis=0).T

        cqn = _rms_norm(cq, qg_ref[...])
        q = jnp.dot(cqn.astype(BF16), wqb_ref[...], preferred_element_type=F32) * scale
        ckvn = _rms_norm(ckv, kvg_ref[...])
        kv = jnp.dot(ckvn.astype(BF16), wkv_ref[...], preferred_element_type=F32)
        k_pe = _rope_tile(kr, cos_t, sin_t, lane)
        for hd in range(MLA_HEADS):
            sl = slice(hd * HEAD_PAD, (hd + 1) * HEAD_PAD)
            q_ref[rows, sl] = _rope_tile(q[:, sl], cos_t, sin_t, lane).astype(BF16)
            k_ref[rows, sl] = (kv[:, sl] + k_pe).astype(BF16)
        v_ref[rows, :] = kv[:, nk:].astype(BF16)
        hc_ref[rows, :] = a * (1.0 / (1.0 + jnp.exp(-g)))


def _in_proj(x2d, pos_row, invf_col, lng, lnb, w_in_r, qg, w_qb_pad, kvg, w_kv_r, *, tm):
    T, D = x2d.shape
    q_rank = w_qb_pad.shape[0]
    kv_rank = w_kv_r.shape[0]
    conv_ch = (w_in_r.shape[1] - q_rank - kv_rank - LANES) // 2
    nk = MLA_HEADS * HEAD_PAD
    nv = MLA_HEADS * V_HEAD_DIM
    full = lambda a: pl.BlockSpec(a.shape, lambda i: (0,) * a.ndim)
    return pl.pallas_call(
        functools.partial(_in_proj_kernel, q_rank=q_rank, kv_rank=kv_rank, conv_ch=conv_ch),
        grid=(T // tm,),
        in_specs=[
            pl.BlockSpec((tm, D), lambda i: (i, 0)),
            pl.BlockSpec((1, tm), lambda i: (0, i)),
            full(invf_col), full(lng), full(lnb), full(w_in_r), full(qg), full(w_qb_pad),
            full(kvg), full(w_kv_r),
        ],
        out_specs=[
            pl.BlockSpec((tm, nk), lambda i: (i, 0)),
            pl.BlockSpec((tm, nk), lambda i: (i, 0)),
            pl.BlockSpec((tm, nv), lambda i: (i, 0)),
            pl.BlockSpec((tm, conv_ch), lambda i: (i, 0)),
        ],
        out_shape=[
            jax.ShapeDtypeStruct((T, nk), BF16),
            jax.ShapeDtypeStruct((T, nk), BF16),
            jax.ShapeDtypeStruct((T, nv), BF16),
            jax.ShapeDtypeStruct((T, conv_ch), F32),
        ],
        compiler_params=pltpu.CompilerParams(
            dimension_semantics=("arbitrary",), vmem_limit_bytes=VMEM_LIMIT_BYTES),
        name="in_proj",
    )(x2d, pos_row, invf_col, lng, lnb, w_in_r, qg, w_qb_pad, kvg, w_kv_r)


CONV_PAD_ROWS = 16
CONV_ROW_CHUNK = 128
CONV_CHUNKS_PER_STEP = 4
CONV_WINDOW = CONV_ROW_CHUNK + 2 * CONV_PAD_ROWS


def _conv_kernel(hc_ref, cw_ref, cb_ref, g_ref, b_ref, o_ref, win_ref, y_ref):
    S, C = hc_ref.shape[1], hc_ref.shape[2]
    n_chunks = S // CONV_ROW_CHUNK
    for r in range(n_chunks):
        lo = r * CONV_ROW_CHUNK - CONV_PAD_ROWS
        hi = lo + CONV_WINDOW
        src_lo, src_hi = max(lo, 0), min(hi, S)
        for c in range(C // LANES):
            if lo < 0:
                win_ref[r, c, 0:-lo, :] = jnp.zeros((-lo, LANES), F32)
            if hi > S:
                win_ref[r, c, CONV_WINDOW - (hi - S):, :] = jnp.zeros((hi - S, LANES), F32)
            win_ref[r, c, src_lo - lo:src_hi - lo, :] = hc_ref[0, src_lo:src_hi, c * LANES:(c + 1) * LANES]
    first = CONV_PAD_ROWS - CONV_WIDTH // 2

    n_tiles = C // LANES
    group = min(CONV_CHUNKS_PER_STEP, n_chunks)

    def chunk_group(rg, carry):
        sums = []
        for u in range(group):
            r = rg * group + u

            def tile_conv(c, s1, r=r, u=u):
                acc = jnp.zeros((CONV_ROW_CHUNK, LANES), F32)
                for j in range(CONV_WIDTH):
                    acc = acc + win_ref[r, c, first + j:first + j + CONV_ROW_CHUNK, :] * cw_ref[c, j:j + 1, :]
                acc = acc + cb_ref[c]
                y_ref[u, c] = acc
                return s1 + acc

            sums.append(lax.fori_loop(0, n_tiles, tile_conv, jnp.zeros((CONV_ROW_CHUNK, LANES), F32)))
        for u in range(group):
            base = pl.multiple_of((rg * group + u) * CONV_ROW_CHUNK, CONV_ROW_CHUNK)
            mu = jnp.sum(sums[u], axis=-1, keepdims=True) * (1.0 / C)
            s2 = jnp.zeros((CONV_ROW_CHUNK, LANES), F32)
            for c in range(n_tiles):
                yc = y_ref[u, c] - mu
                s2 = s2 + yc * yc
            inv = lax.rsqrt(jnp.sum(s2, axis=-1, keepdims=True) * (1.0 / C) + LN_EPS)
            for c in range(n_tiles):
                lanes = slice(c * LANES, (c + 1) * LANES)
                y = (y_ref[u, c] - mu) * inv * g_ref[:, lanes] + b_ref[:, lanes]
                o_ref[0, pl.ds(base, CONV_ROW_CHUNK), lanes] = (y * (1.0 / (1.0 + jnp.exp(-y)))).astype(BF16)
        return carry

    lax.fori_loop(0, n_chunks // group, chunk_group, 0)


def _conv(hc, cw, cb, g, b):
    B, S, C = hc.shape
    full = lambda a: pl.BlockSpec(a.shape, lambda i: (0,) * a.ndim)
    return pl.pallas_call(
        _conv_kernel,
        grid=(B,),
        in_specs=[pl.BlockSpec((1, S, C), lambda i: (i, 0, 0)), full(cw), full(cb), full(g), full(b)],
        out_specs=pl.BlockSpec((1, S, C), lambda i: (i, 0, 0)),
        out_shape=jax.ShapeDtypeStruct((B, S, C), BF16),
        scratch_shapes=[pltpu.VMEM((S // CONV_ROW_CHUNK, C // LANES, CONV_WINDOW, LANES), F32),
                        pltpu.VMEM((CONV_CHUNKS_PER_STEP, C // LANES, CONV_ROW_CHUNK, LANES), F32)],
        compiler_params=pltpu.CompilerParams(
            dimension_semantics=("arbitrary",), vmem_limit_bytes=VMEM_LIMIT_BYTES),
        name="conv",
    )(hc, cw, cb, g, b)


HEADS_PER_TILE = LANES // V_HEAD_DIM
ATTN_HEADS_PER_STEP = 4


ATTN_SLOTS = 2


def _attn_kernel(q_ref, k_ref, v_ref, o_ref, s_ref, p_ref, *, tq):
    S = q_ref.shape[1]
    lane = lax.broadcasted_iota(jnp.int32, (tq, LANES), 1)
    item = 0
    for pr in range(v_ref.shape[2] // LANES):
        v = v_ref[0, :, pr * LANES:(pr + 1) * LANES]
        for qi in range(S // tq):
            rows = slice(qi * tq, (qi + 1) * tq)
            outs = []
            for hh in range(HEADS_PER_TILE):
                slot = item % ATTN_SLOTS
                item += 1
                c0 = (pr * HEADS_PER_TILE + hh) * HEAD_PAD
                s_ref[slot] = lax.dot_general(q_ref[0, rows, c0:c0 + HEAD_PAD], k_ref[0, :, c0:c0 + HEAD_PAD],
                                              (((1,), (1,)), ((), ())), preferred_element_type=F32)
                s = s_ref[slot]
                p = jnp.exp(s - jnp.max(s, axis=-1, keepdims=True))
                l = jnp.sum(p, axis=-1, keepdims=True)
                p_ref[slot] = p.astype(BF16)
                o = jnp.dot(p_ref[slot], v, preferred_element_type=F32)
                outs.append(o * (1.0 / l))
            o_ref[0, rows, pr * LANES:(pr + 1) * LANES] = jnp.where(lane < V_HEAD_DIM, outs[0], outs[1]).astype(BF16)


def _attn(q, k, v, *, tq):
    B, S, _ = q.shape
    qw = ATTN_HEADS_PER_STEP * HEAD_PAD
    vw = ATTN_HEADS_PER_STEP * V_HEAD_DIM
    return pl.pallas_call(
        functools.partial(_attn_kernel, tq=tq),
        grid=(B, MLA_HEADS // ATTN_HEADS_PER_STEP),
        in_specs=[
            pl.BlockSpec((1, S, qw), lambda b, h: (b, 0, h)),
            pl.BlockSpec((1, S, qw), lambda b, h: (b, 0, h)),
            pl.BlockSpec((1, S, vw), lambda b, h: (b, 0, h)),
        ],
        out_specs=pl.BlockSpec((1, S, vw), lambda b, h: (b, 0, h)),
        out_shape=jax.ShapeDtypeStruct((B, S, MLA_HEADS * V_HEAD_DIM), BF16),
        scratch_shapes=[pltpu.VMEM((ATTN_SLOTS, tq, S), F32), pltpu.VMEM((ATTN_SLOTS, tq, S), BF16)],
        compiler_params=pltpu.CompilerParams(
            dimension_semantics=("arbitrary", "arbitrary"), vmem_limit_bytes=62 * 1024 * 1024),
        name="attn",
    )(q, k, v)


def _out_proj_kernel(x_ref, att_ref, cv_ref, lng_ref, lnb_ref, wo_ref, g1_ref, b1_ref, wr_ref,
                     h1_ref, lt_ref):
    na = att_ref.shape[1]
    tm = x_ref.shape[0]
    n_lt = lt_ref.shape[1]
    sub = min(tm, TOKEN_SUB_TILE)
    for r0 in range(0, tm, sub):
        rows = slice(r0, r0 + sub)
        h0 = _layer_norm(x_ref[rows, :], lng_ref[...], lnb_ref[...])
        mix = jnp.dot(att_ref[rows, :], wo_ref[:na, :], preferred_element_type=F32)
        mix = mix + jnp.dot(cv_ref[rows, :], wo_ref[na:, :], preferred_element_type=F32)
        h1 = _layer_norm(DEEPNORM_ALPHA * h0 + mix, g1_ref[...], b1_ref[...])
        h1_ref[rows, :] = h1
        lg = jnp.dot(h1.astype(BF16), wr_ref[...], preferred_element_type=F32)
        lt_ref[0, :, rows] = lg.T[:n_lt, :]


def _out_proj(x2d, att, cv, lng, lnb, w_o, g1, b1, w_r2, *, tm, seq, n_lt):
    T, D = x2d.shape
    per = seq // tm
    full = lambda a: pl.BlockSpec(a.shape, lambda i: (0,) * a.ndim)
    return pl.pallas_call(
        _out_proj_kernel,
        grid=(T // tm,),
        in_specs=[
            pl.BlockSpec((tm, D), lambda i: (i, 0)),
            pl.BlockSpec((tm, att.shape[1]), lambda i: (i, 0)),
            pl.BlockSpec((tm, cv.shape[1]), lambda i: (i, 0)),
            full(lng), full(lnb), full(w_o), full(g1), full(b1), full(w_r2),
        ],
        out_specs=[
            pl.BlockSpec((tm, D), lambda i: (i, 0)),
            pl.BlockSpec((1, n_lt, tm), lambda i: (i // per, 0, i % per)),
        ],
        out_shape=[
            jax.ShapeDtypeStruct((T, D), F32),
            jax.ShapeDtypeStruct((T // seq, n_lt, seq), F32),
        ],
        compiler_params=pltpu.CompilerParams(
            dimension_semantics=("arbitrary",), vmem_limit_bytes=VMEM_LIMIT_BYTES),
        name="out_proj",
    )(x2d, att, cv, lng, lnb, w_o, g1, b1, w_r2)


def _route_kernel(lt_ref, rank_ref, gate_ref, off_ref, *, cap, tb):
    B, E, S = rank_ref.shape
    lt = lt_ref[:, :E, :] + lt_ref[:, E:, :]
    ex = jnp.exp(lt - jnp.max(lt, axis=1, keepdims=True))
    aff = ex / jnp.sum(ex, axis=1, keepdims=True)
    v = aff.reshape(B * E, S)
    rows = B * E
    kf = float(cap)

    def step(_, carry):
        lo, hi = carry
        mid = jnp.sqrt(jnp.maximum(lo, TOPK_SEARCH_FLOOR)) * jnp.sqrt(hi)
        cnt = jnp.sum(jnp.where(v >= mid, 1.0, 0.0), axis=1, keepdims=True)
        ge = cnt >= kf
        return jnp.where(ge, mid, lo), jnp.where(ge, hi, mid)

    lo, hi = lax.fori_loop(0, TOPK_SEARCH_STEPS, step,
                           (jnp.zeros((rows, 1), F32), jnp.full((rows, 1), 2.0, F32)))
    above = v >= hi
    tie = jnp.logical_and(v >= lo, jnp.logical_not(above))
    stacked = jnp.concatenate([jnp.where(above, 1.0, 0.0), jnp.where(tie, 1.0, 0.0)], axis=0)
    upper = jnp.where(lax.broadcasted_iota(jnp.int32, (S, S), 0) <= lax.broadcasted_iota(jnp.int32, (S, S), 1),
                      1.0, 0.0).astype(BF16)
    pc = jnp.dot(stacked.astype(BF16), upper, preferred_element_type=F32)
    pa = pc[:rows]
    pt = pc[rows:]
    need = kf - jnp.sum(jnp.where(above, 1.0, 0.0), axis=1, keepdims=True)
    sel = jnp.logical_or(above, jnp.logical_and(tie, pt <= need))
    taken = pa + jnp.minimum(pt, need)
    rank_ref[...] = jnp.where(sel, taken - 1.0, -1.0).astype(jnp.int32).reshape(B, E, S)
    gate_ref[...] = jnp.where(sel, v, 0.0).reshape(B, E, S)
    pick = jnp.where(lax.broadcasted_iota(jnp.int32, (S, LANES), 0) + 1
                     == lax.broadcasted_iota(jnp.int32, (S, LANES), 1) * tb, 1.0, 0.0).astype(BF16)
    off = jnp.dot(taken.astype(BF16), pick, preferred_element_type=F32)
    off_ref[...] = off.astype(jnp.int32).reshape(B, E, LANES)


def _route(lt2, *, cap, tb):
    B, E2, S = lt2.shape
    E = E2 // 2
    assert cap <= 256 and S // tb < LANES
    spec = pl.BlockSpec((B, E, S), lambda i: (0, 0, 0))
    ospec = pl.BlockSpec((B, E, LANES), lambda i: (0, 0, 0))
    return pl.pallas_call(
        functools.partial(_route_kernel, cap=cap, tb=tb),
        grid=(1,),
        in_specs=[pl.BlockSpec((B, E2, S), lambda i: (0, 0, 0))],
        out_specs=[spec, spec, ospec],
        out_shape=[jax.ShapeDtypeStruct((B, E, S), jnp.int32), jax.ShapeDtypeStruct((B, E, S), F32),
                   jax.ShapeDtypeStruct((B, E, LANES), jnp.int32)],
        compiler_params=pltpu.CompilerParams(
            dimension_semantics=("arbitrary",), vmem_limit_bytes=VMEM_LIMIT_BYTES),
        name="route",
    )(lt2)


ROUTE_TOKEN_BLOCK = 256
ROUTE_BLOCKS_PER_STEP = 4
SLOT_WINDOW = 64
SLOT_ALIGN = 16


def _slot_windows(off_ref, base, n_experts, cap, win):
    starts = []
    n_pass = jnp.int32(1)
    for e in range(n_experts):
        off = off_ref[base + e]
        end = off_ref[base + n_experts + e]
        start = jnp.minimum((off // SLOT_ALIGN) * SLOT_ALIGN, cap - win)
        starts.append(start)
        n_pass = jnp.maximum(n_pass, (end - start + (win - 1)) // win)
    return starts, n_pass


def _dispatch_kernel(off_ref, rank_ref, h_ref, xg_ref, *, cap, win, tb):
    E = rank_ref.shape[1]
    nsb = rank_ref.shape[2] // tb
    b, j = pl.program_id(0), pl.program_id(1)
    n_bounds = pl.num_programs(1) * nsb + 1

    @pl.when(j == 0)
    def _():
        xg_ref[...] = jnp.zeros(xg_ref.shape, BF16)

    sub = lax.broadcasted_iota(jnp.int32, (win, tb), 0)

    def one_pass(sb, starts, p, first):
        cols = slice(sb * tb, (sb + 1) * tb)
        pieces, rows = [], []
        for e in range(E):
            lo = starts[e] + p * win
            ws = pl.multiple_of(jnp.minimum(lo, cap - win), SLOT_ALIGN)
            tgt = ws + sub
            hit = rank_ref[0, e:e + 1, cols] == tgt
            if not first:
                hit = jnp.logical_and(hit, tgt >= lo)
            pieces.append(jnp.where(hit, 1.0, 0.0).astype(BF16))
            rows.append(ws)
        onehot = jnp.concatenate(pieces, axis=0)
        hb = h_ref[0, cols, :].astype(BF16)
        got = jnp.dot(onehot, hb, preferred_element_type=F32).astype(BF16)
        for e in range(E):
            xg_ref[e, 0, pl.ds(rows[e], win), :] += got[e * win:(e + 1) * win, :]

    plans = []
    for sb in range(nsb):
        starts, n_pass = _slot_windows(off_ref, (b * n_bounds + j * nsb + sb) * E, E, cap, win)
        plans.append((starts, n_pass))
        one_pass(sb, starts, 0, True)
    for sb in range(nsb):
        starts, n_pass = plans[sb]

        def extra(p, carry, sb=sb, starts=starts):
            one_pass(sb, starts, p, False)
            return carry

        lax.fori_loop(1, n_pass, extra, 0)


def _dispatch(off_flat, rank, h1b, *, cap, tb, nsb):
    B, E, S = rank.shape
    D = h1b.shape[-1]
    win = min(SLOT_WINDOW, cap)
    ts = tb * nsb
    assert cap % SLOT_ALIGN == 0 and win % SLOT_ALIGN == 0 and S % ts == 0
    return pl.pallas_call(
        functools.partial(_dispatch_kernel, cap=cap, win=win, tb=tb),
        grid_spec=pltpu.PrefetchScalarGridSpec(
            num_scalar_prefetch=1,
            grid=(B, S // ts),
            in_specs=[pl.BlockSpec((1, E, ts), lambda b, j, off: (b, 0, j)),
                      pl.BlockSpec((1, ts, D), lambda b, j, off: (b, j, 0))],
            out_specs=pl.BlockSpec((E, 1, cap, D), lambda b, j, off: (0, b, 0, 0)),
        ),
        out_shape=jax.ShapeDtypeStruct((E, B, cap, D), BF16),
        compiler_params=pltpu.CompilerParams(
            dimension_semantics=("arbitrary", "arbitrary"), vmem_limit_bytes=VMEM_LIMIT_BYTES),
        name="dispatch",
    )(off_flat, rank, h1b)


EXPERT_ROW_BLOCK = 1024


def _expert_kernel(x_ref, wg_ref, wu_ref, wd_ref, y_ref, acc_ref, wgb_ref, wub_ref, wdb_ref, *, n_chunks):
    f = pl.program_id(1)
    rows = x_ref.shape[1]
    rb = min(rows, EXPERT_ROW_BLOCK)
    last = n_chunks - 1

    def chunk(first, final):
        wgb_ref[...] = wg_ref[0].astype(BF16)
        wub_ref[...] = wu_ref[0].astype(BF16)
        wdb_ref[...] = wd_ref[0].astype(BF16)
        for r0 in range(0, rows, rb):
            x = x_ref[0, r0:r0 + rb, :]
            a = jnp.dot(x, wgb_ref[...], preferred_element_type=F32)
            u = jnp.dot(x, wub_ref[...], preferred_element_type=F32)
            hmid = (a * (1.0 / (1.0 + jnp.exp(-a))) * u).astype(BF16)
            part = jnp.dot(hmid, wdb_ref[...], preferred_element_type=F32)
            if not first:
                part = acc_ref[r0:r0 + rb, :] + part
            if final:
                y_ref[0, r0:r0 + rb, :] = part.astype(BF16)
            else:
                acc_ref[r0:r0 + rb, :] = part

    if n_chunks == 1:
        chunk(True, True)
    else:
        pl.when(f == 0)(lambda: chunk(True, False))
        pl.when(jnp.logical_and(f > 0, f < last))(lambda: chunk(False, False))
        pl.when(f == last)(lambda: chunk(False, True))


def _experts(xg, w_gate, w_up, w_down, *, tf):
    E, rows, D = xg.shape
    F = w_gate.shape[-1]
    return pl.pallas_call(
        functools.partial(_expert_kernel, n_chunks=F // tf),
        grid=(E, F // tf),
        in_specs=[
            pl.BlockSpec((1, rows, D), lambda e, f: (e, 0, 0)),
            pl.BlockSpec((1, D, tf), lambda e, f: (e, 0, f)),
            pl.BlockSpec((1, D, tf), lambda e, f: (e, 0, f)),
            pl.BlockSpec((1, tf, D), lambda e, f: (e, f, 0)),
        ],
        out_specs=pl.BlockSpec((1, rows, D), lambda e, f: (e, 0, 0)),
        out_shape=jax.ShapeDtypeStruct((E, rows, D), BF16),
        scratch_shapes=[pltpu.VMEM((rows, D), F32), pltpu.VMEM((D, tf), BF16), pltpu.VMEM((D, tf), BF16),
                        pltpu.VMEM((tf, D), BF16)],
        compiler_params=pltpu.CompilerParams(
            dimension_semantics=("arbitrary", "arbitrary"), vmem_limit_bytes=VMEM_LIMIT_BYTES),
        name="experts",
    )(xg, w_gate, w_up, w_down)


def _combine_kernel(off_ref, rank_ref, gate_ref, y_ref, h_ref, g2_ref, b2_ref, o_ref, *, cap, win, tb):
    E = rank_ref.shape[1]
    nsb = rank_ref.shape[2] // tb
    b, j = pl.program_id(0), pl.program_id(1)
    n_bounds = pl.num_programs(1) * nsb + 1
    per_tile = LANES // win
    lane = lax.broadcasted_iota(jnp.int32, (1, LANES), 1)
    first_lane = lax.broadcasted_iota(jnp.int32, (LANES, E * win), 0) * win
    col = lax.broadcasted_iota(jnp.int32, (LANES, E * win), 1)
    spread = jnp.where(jnp.logical_and(col >= first_lane, col < first_lane + win), 1.0, 0.0).astype(BF16)
    pad_rows = jnp.zeros((LANES - E, tb), F32)

    def token_major(x_et):
        return jnp.concatenate([x_et, pad_rows], axis=0).T.astype(BF16)

    def one_pass(sb, starts, p, first):
        cols = slice(sb * tb, (sb + 1) * tb)
        rk = jnp.dot(token_major(rank_ref[0, :, cols].astype(F32)), spread, preferred_element_type=F32)
        gt = jnp.dot(token_major(gate_ref[0, :, cols]), spread, preferred_element_type=F32)
        tgts, los, ys = [], [], []
        for t in range(E // per_tile):
            tgt = lo_v = None
            for u in range(per_tile):
                e = t * per_tile + u
                lo = starts[e] + p * win
                ws = pl.multiple_of(jnp.minimum(lo, cap - win), SLOT_ALIGN)
                ys.append(y_ref[e, 0, pl.ds(ws, win), :])
                t_u = ws + lane - u * win
                if u == 0:
                    tgt, lo_v = t_u, jnp.full((1, LANES), lo, jnp.int32)
                else:
                    here = lane >= u * win
                    tgt = jnp.where(here, t_u, tgt)
                    lo_v = jnp.where(here, lo, lo_v)
            tgts.append(tgt)
            los.append(lo_v)
        tgt = jnp.concatenate(tgts, axis=1)
        hit = rk == tgt.astype(F32)
        if not first:
            hit = jnp.logical_and(hit, tgt >= jnp.concatenate(los, axis=1))
        gates = jnp.where(hit, gt, 0.0).astype(BF16)
        return jnp.dot(gates, jnp.concatenate(ys, axis=0), preferred_element_type=F32)

    plans = []
    for sb in range(nsb):
        rows = slice(sb * tb, (sb + 1) * tb)
        starts, n_pass = _slot_windows(off_ref, (b * n_bounds + j * nsb + sb) * E, E, cap, win)
        plans.append((starts, n_pass))
        o_ref[0, rows, :] = DEEPNORM_ALPHA * h_ref[0, rows, :] + one_pass(sb, starts, 0, True)
    for sb in range(nsb):
        starts, n_pass = plans[sb]

        def extra(p, carry, sb=sb, starts=starts):
            o_ref[0, sb * tb:(sb + 1) * tb, :] += one_pass(sb, starts, p, False)
            return carry

        lax.fori_loop(1, n_pass, extra, 0)
    o_ref[0] = _layer_norm(o_ref[0], g2_ref[...], b2_ref[...])


def _combine(off_flat, rank, gate, y, h1, g2, b2, *, tb, nsb):
    B, E, S = rank.shape
    cap, D = y.shape[2], y.shape[3]
    win = min(SLOT_WINDOW, cap)
    ts = tb * nsb
    assert cap % SLOT_ALIGN == 0 and win % SLOT_ALIGN == 0 and LANES % win == 0 and E % (LANES // win) == 0
    assert S % ts == 0
    full = lambda a: pl.BlockSpec(a.shape, lambda b, j, off: (0,) * a.ndim)
    return pl.pallas_call(
        functools.partial(_combine_kernel, cap=cap, win=win, tb=tb),
        grid_spec=pltpu.PrefetchScalarGridSpec(
            num_scalar_prefetch=1,
            grid=(B, S // ts),
            in_specs=[
                pl.BlockSpec((1, E, ts), lambda b, j, off: (b, 0, j)),
                pl.BlockSpec((1, E, ts), lambda b, j, off: (b, 0, j)),
                pl.BlockSpec((E, 1, cap, D), lambda b, j, off: (0, b, 0, 0)),
                pl.BlockSpec((1, ts, D), lambda b, j, off: (b, j, 0)),
                full(g2), full(b2),
            ],
            out_specs=pl.BlockSpec((1, ts, D), lambda b, j, off: (b, j, 0)),
        ),
        out_shape=jax.ShapeDtypeStruct((B, S, D), F32),
        compiler_params=pltpu.CompilerParams(
            dimension_semantics=("arbitrary", "arbitrary"), vmem_limit_bytes=VMEM_LIMIT_BYTES),
        name="combine",
    )(off_flat, rank, gate, y, h1, g2, b2)


def _tile(n, target):
    t = min(n, target)
    assert n % t == 0, (n, t)
    return t


def kernel(x, positions, emb_ln_g, emb_ln_b, w_in, q_norm_g, w_qb, kv_norm_g, w_kvb, conv_w, conv_b,
           conv_ln_g, conv_ln_b, w_o, ln1_g, ln1_b, w_router, w_gate, w_up, w_down, ln2_g, ln2_b):
    B, S, D = x.shape
    T = B * S
    H = MLA_HEADS
    q_rank = q_norm_g.shape[-1]
    kv_rank = kv_norm_g.shape[-1]
    conv_ch = conv_w.shape[-1]
    qk_dim = QK_NOPE_DIM + QK_ROPE_DIM
    cap = CAPACITY_FACTOR * S // N_EXPERTS
    assert w_in.shape[0] == DEPTH == 1
    row = lambda a: a.reshape(1, -1)

    wi = w_in[0]
    c1, c2, c3 = q_rank, q_rank + kv_rank, q_rank + kv_rank + QK_ROPE_DIM
    tail = LANES - QK_NOPE_DIM - QK_ROPE_DIM
    kr_cols = jnp.pad(wi[:, c2:c3], ((0, 0), (QK_NOPE_DIM, tail)))
    w_in_r = jnp.concatenate(
        [wi[:, :c2], wi[:, c3:c3 + conv_ch], wi[:, c3 + conv_ch:], kr_cols], axis=1).astype(BF16)
    w_qb_pad = jnp.pad(w_qb[0].reshape(q_rank, H, qk_dim),
                       ((0, 0), (0, 0), (0, HEAD_PAD - qk_dim))).reshape(q_rank, H * HEAD_PAD).astype(BF16)
    wkv = w_kvb[0].reshape(kv_rank, H, QK_NOPE_DIM + V_HEAD_DIM)
    wk_pad = jnp.pad(wkv[:, :, :QK_NOPE_DIM], ((0, 0), (0, 0), (0, HEAD_PAD - QK_NOPE_DIM)))
    w_kv_r = jnp.concatenate([wk_pad.reshape(kv_rank, H * HEAD_PAD),
                              wkv[:, :, QK_NOPE_DIM:].reshape(kv_rank, H * V_HEAD_DIM)], axis=1).astype(BF16)
    half = QK_ROPE_DIM // 2
    inv_freq = (ROPE_THETA ** (-jnp.arange(half, dtype=F32) / half)).reshape(half, 1)

    x2d = x.reshape(T, D)
    tm = _tile(S, 512)
    q, k, v, hc = _in_proj(x2d, positions.reshape(1, T), inv_freq, row(emb_ln_g), row(emb_ln_b), w_in_r,
                           row(q_norm_g[0]), w_qb_pad, row(kv_norm_g[0]), w_kv_r, tm=tm)
    n_ct = conv_ch // LANES
    cv = _conv(hc.reshape(B, S, conv_ch), jnp.swapaxes(conv_w[0].reshape(CONV_WIDTH, n_ct, LANES), 0, 1),
               conv_b[0].reshape(n_ct, 1, LANES), row(conv_ln_g[0]), row(conv_ln_b[0]))
    att = _attn(q.reshape(B, S, H * HEAD_PAD), k.reshape(B, S, H * HEAD_PAD),
                v.reshape(B, S, H * V_HEAD_DIM), tq=_tile(S, 512))
    wr_hi = w_router[0].astype(BF16)
    w_r2 = jnp.concatenate([wr_hi, (w_router[0] - wr_hi.astype(F32)).astype(BF16)], axis=1)
    w_r2 = jnp.pad(w_r2, ((0, 0), (0, LANES - 2 * N_EXPERTS)))
    h1, lt2 = _out_proj(x2d, att.reshape(T, H * V_HEAD_DIM), cv.reshape(T, conv_ch), row(emb_ln_g),
                        row(emb_ln_b), w_o[0].astype(BF16), row(ln1_g[0]), row(ln1_b[0]),
                        w_r2, tm=_tile(S, 2048), seq=S, n_lt=2 * N_EXPERTS)
    h1 = h1.reshape(B, S, D)
    tb = _tile(S, ROUTE_TOKEN_BLOCK)
    rank, gate, off = _route(lt2, cap=cap, tb=tb)
    off_flat = jnp.swapaxes(off[:, :, :S // tb + 1], 1, 2).reshape(-1)
    nsb = _tile(S // tb, ROUTE_BLOCKS_PER_STEP)
    xg = _dispatch(off_flat, rank, h1, cap=cap, tb=tb, nsb=nsb)
    y = _experts(xg.reshape(N_EXPERTS, B * cap, D), w_gate[0], w_up[0], w_down[0],
                 tf=_tile(w_gate.shape[-1], 512)).reshape(N_EXPERTS, B, cap, D)
    return _combine(off_flat, rank, gate, y, h1, row(ln2_g[0]), row(ln2_b[0]), tb=tb, nsb=nsb)
```

```python
import functools
import math

import jax
import jax.numpy as jnp
from jax import lax
from jax.experimental import pallas as pl
from jax.experimental.pallas import tpu as pltpu

F32 = jnp.float32
BF16 = jnp.bfloat16

MLA_HEADS = 8
QK_NOPE_DIM = 64
QK_ROPE_DIM = 32
V_HEAD_DIM = 64
CONV_WIDTH = 31
ROPE_THETA = 10000.0
N_EXPERTS = 16
CAPACITY_FACTOR = 2
DEPTH = 1
DEEPNORM_ALPHA = (2.0 * DEPTH) ** 0.25
LN_EPS = 1e-5
RMS_EPS = 1e-6
LOG2_E = math.log2(math.e)

LANES = 128
SUBLANES = 8
HEAD_PAD = LANES
VMEM_LIMIT_BYTES = 56 * 1024 * 1024
TOKEN_SUB_TILE = 256

TOPK_SEARCH_STEPS = 36
TOPK_SEARCH_FLOOR = 1e-30


def _layer_norm(x, g, b):
    mu = jnp.mean(x, axis=-1, keepdims=True)
    xc = x - mu
    var = jnp.mean(xc * xc, axis=-1, keepdims=True)
    return xc * lax.rsqrt(var + LN_EPS) * g + b


def _rms_norm(x, g):
    return x * lax.rsqrt(jnp.mean(x * x, axis=-1, keepdims=True) + RMS_EPS) * g


def _rope_tile(x, cos_t, sin_t, lane):
    half = QK_ROPE_DIM // 2
    fwd = pltpu.roll(x, LANES - half, 1)
    bwd = pltpu.roll(x, half, 1)
    partner = jnp.where(lane < QK_NOPE_DIM + half, fwd, bwd)
    return x * cos_t + partner * sin_t


def _in_proj_kernel(x_ref, pos_ref, invf_ref, lng_ref, lnb_ref, win_ref, qg_ref, wqb_ref,
                    kvg_ref, wkv_ref, q_ref, k_ref, v_ref, hc_ref, *, q_rank, kv_rank, conv_ch):
    tm = x_ref.shape[0]
    sub = min(tm, TOKEN_SUB_TILE)
    c1 = q_rank
    c2 = c1 + kv_rank
    c3 = c2 + conv_ch
    c4 = c3 + conv_ch
    tail = LANES - QK_NOPE_DIM - QK_ROPE_DIM
    nk = MLA_HEADS * HEAD_PAD
    scale = (QK_NOPE_DIM + QK_ROPE_DIM) ** -0.5 * LOG2_E
    ones = jnp.ones((QK_NOPE_DIM, sub), F32)
    zeros = jnp.zeros((QK_NOPE_DIM, sub), F32)
    lane = lax.broadcasted_iota(jnp.int32, (sub, LANES), 1)
    for r0 in range(0, tm, sub):
        rows = slice(r0, r0 + sub)
        h = _layer_norm(x_ref[rows, :], lng_ref[...], lnb_ref[...])
        proj = jnp.dot(h.astype(BF16), win_ref[...], preferred_element_type=F32)
        cq = proj[:, :c1]
        ckv = proj[:, c1:c2]
        a = proj[:, c2:c3]
        g = proj[:, c3:c4]
        kr = proj[:, c4:c4 + LANES]

        ang = pos_ref[:, rows].astype(F32) * invf_ref[...]
        cos = jnp.cos(ang)
        sin = jnp.sin(ang)
        cos_t = jnp.concatenate([ones, cos, cos, ones[:tail]], axis=0).T
        sin_t = jnp.concatenate([zeros, -sin, sin, zeros[:tail]], axis=0).T

        cqn = _rms_norm(cq, qg_ref[...])
        q = jnp.dot(cqn.astype(BF16), wqb_ref[...], preferred_element_type=F32) * scale
        ckvn = _rms_norm(ckv, kvg_ref[...])
        kv = jnp.dot(ckvn.astype(BF16), wkv_ref[...], preferred_element_type=F32)
        k_pe = _rope_tile(kr, cos_t, sin_t, lane)
        for hd in range(MLA_HEADS):
            sl = slice(hd * HEAD_PAD, (hd + 1) * HEAD_PAD)
            q_ref[rows, sl] = _rope_tile(q[:, sl], cos_t, sin_t, lane).astype(BF16)
            k_ref[rows, sl] = (kv[:, sl] + k_pe).astype(BF16)
        v_ref[rows, :] = kv[:, nk:].astype(BF16)
        hc_ref[rows, :] = a * (1.0 / (1.0 + jnp.exp(-g)))


def _in_proj(x2d, pos_row, invf_col, lng, lnb, w_in_r, qg, w_qb_pad, kvg, w_kv_r, *, tm):
    T, D = x2d.shape
    q_rank = w_qb_pad.shape[0]
    kv_rank = w_kv_r.shape[0]
    conv_ch = (w_in_r.shape[1] - q_rank - kv_rank - LANES) // 2
    nk = MLA_HEADS * HEAD_PAD
    nv = MLA_HEADS * V_HEAD_DIM
    full = lambda a: pl.BlockSpec(a.shape, lambda i: (0,) * a.ndim)
    return pl.pallas_call(
        functools.partial(_in_proj_kernel, q_rank=q_rank, kv_rank=kv_rank, conv_ch=conv_ch),
        grid=(T // tm,),
        in_specs=[
            pl.BlockSpec((tm, D), lambda i: (i, 0)),
            pl.BlockSpec((1, tm), lambda i: (0, i)),
            full(invf_col), full(lng), full(lnb), full(w_in_r), full(qg), full(w_qb_pad),
            full(kvg), full(w_kv_r),
        ],
        out_specs=[
            pl.BlockSpec((tm, nk), lambda i: (i, 0)),
            pl.BlockSpec((tm, nk), lambda i: (i, 0)),
            pl.BlockSpec((tm, nv), lambda i: (i, 0)),
            pl.BlockSpec((tm, conv_ch), lambda i: (i, 0)),
        ],
        out_shape=[
            jax.ShapeDtypeStruct((T, nk), BF16),
            jax.ShapeDtypeStruct((T, nk), BF16),
            jax.ShapeDtypeStruct((T, nv), BF16),
            jax.ShapeDtypeStruct((T, conv_ch), F32),
        ],
        compiler_params=pltpu.CompilerParams(
            dimension_semantics=("arbitrary",), vmem_limit_bytes=VMEM_LIMIT_BYTES),
        name="in_proj",
    )(x2d, pos_row, invf_col, lng, lnb, w_in_r, qg, w_qb_pad, kvg, w_kv_r)


CONV_PAD_ROWS = 16
CONV_ROW_CHUNK = 128
CONV_CHUNKS_PER_STEP = 4
CONV_WINDOW = CONV_ROW_CHUNK + 2 * CONV_PAD_ROWS


def _conv_kernel(hc_ref, cw_ref, cb_ref, g_ref, b_ref, o_ref, win_ref, y_ref):
    S, C = hc_ref.shape[1], hc_ref.shape[2]
    n_chunks = S // CONV_ROW_CHUNK
    for r in range(n_chunks):
        lo = r * CONV_ROW_CHUNK - CONV_PAD_ROWS
        hi = lo + CONV_WINDOW
        src_lo, src_hi = max(lo, 0), min(hi, S)
        for c in range(C // LANES):
            if lo < 0:
                win_ref[r, c, 0:-lo, :] = jnp.zeros((-lo, LANES), F32)
            if hi > S:
                win_ref[r, c, CONV_WINDOW - (hi - S):, :] = jnp.zeros((hi - S, LANES), F32)
            win_ref[r, c, src_lo - lo:src_hi - lo, :] = hc_ref[0, src_lo:src_hi, c * LANES:(c + 1) * LANES]
    first = CONV_PAD_ROWS - CONV_WIDTH // 2

    n_tiles = C // LANES
    group = min(CONV_CHUNKS_PER_STEP, n_chunks)

    def chunk_group(rg, carry):
        sums = []
        for u in range(group):
            r = rg * group + u

            def tile_conv(c, s1, r=r, u=u):
                acc = jnp.zeros((CONV_ROW_CHUNK, LANES), F32)
                for j in range(CONV_WIDTH):
                    acc = acc + win_ref[r, c, first + j:first + j + CONV_ROW_CHUNK, :] * cw_ref[c, j:j + 1, :]
                acc = acc + cb_ref[c]
                y_ref[u, c] = acc
                return s1 + acc

            sums.append(lax.fori_loop(0, n_tiles, tile_conv, jnp.zeros((CONV_ROW_CHUNK, LANES), F32)))
        for u in range(group):
            base = pl.multiple_of((rg * group + u) * CONV_ROW_CHUNK, CONV_ROW_CHUNK)
            mu = jnp.sum(sums[u], axis=-1, keepdims=True) * (1.0 / C)
            s2 = jnp.zeros((CONV_ROW_CHUNK, LANES), F32)
            for c in range(n_tiles):
                yc = y_ref[u, c] - mu
                s2 = s2 + yc * yc
            inv = lax.rsqrt(jnp.sum(s2, axis=-1, keepdims=True) * (1.0 / C) + LN_EPS)
            for c in range(n_tiles):
                lanes = slice(c * LANES, (c + 1) * LANES)
                y = (y_ref[u, c] - mu) * inv * g_ref[:, lanes] + b_ref[:, lanes]
                o_ref[0, pl.ds(base, CONV_ROW_CHUNK), lanes] = (y * (1.0 / (1.0 + jnp.exp(-y)))).astype(BF16)
        return carry

    lax.fori_loop(0, n_chunks // group, chunk_group, 0)


def _conv(hc, cw, cb, g, b):
    B, S, C = hc.shape
    full = lambda a: pl.BlockSpec(a.shape, lambda i: (0,) * a.ndim)
    return pl.pallas_call(
        _conv_kernel,
        grid=(B,),
        in_specs=[pl.BlockSpec((1, S, C), lambda i: (i, 0, 0)), full(cw), full(cb), full(g), full(b)],
        out_specs=pl.BlockSpec((1, S, C), lambda i: (i, 0, 0)),
        out_shape=jax.ShapeDtypeStruct((B, S, C), BF16),
        scratch_shapes=[pltpu.VMEM((S // CONV_ROW_CHUNK, C // LANES, CONV_WINDOW, LANES), F32),
                        pltpu.VMEM((CONV_CHUNKS_PER_STEP, C // LANES, CONV_ROW_CHUNK, LANES), F32)],
        compiler_params=pltpu.CompilerParams(
            dimension_semantics=("arbitrary",), vmem_limit_bytes=VMEM_LIMIT_BYTES),
        name="conv",
    )(hc, cw, cb, g, b)


HEADS_PER_TILE = LANES // V_HEAD_DIM
ATTN_HEADS_PER_STEP = 8


ATTN_SLOTS = 2


def _attn_kernel(q_ref, k_ref, v_ref, o_ref, s_ref, p_ref, *, tq):
    S = q_ref.shape[1]
    assert HEADS_PER_TILE == 2
    lane = lax.broadcasted_iota(jnp.int32, (tq, LANES), 1)
    key_lane = lax.broadcasted_iota(jnp.int32, (S, LANES), 1)
    item = 0
    for pr in range(v_ref.shape[2] // LANES):
        v = v_ref[0, :, pr * LANES:(pr + 1) * LANES]
        one = jnp.ones((S, LANES), BF16)
        v_ones = [jnp.where(key_lane < V_HEAD_DIM, v, one), jnp.where(key_lane < V_HEAD_DIM, one, v)]
        for qi in range(S // tq):
            rows = slice(qi * tq, (qi + 1) * tq)
            outs = []
            for hh in range(HEADS_PER_TILE):
                slot = item % ATTN_SLOTS
                item += 1
                c0 = (pr * HEADS_PER_TILE + hh) * HEAD_PAD
                s_ref[slot] = lax.dot_general(q_ref[0, rows, c0:c0 + HEAD_PAD], k_ref[0, :, c0:c0 + HEAD_PAD],
                                              (((1,), (1,)), ((), ())), preferred_element_type=F32)
                s = s_ref[slot]
                p_ref[slot] = jnp.exp2(s - jnp.max(s, axis=-1, keepdims=True)).astype(BF16)
                o = jnp.dot(p_ref[slot], v_ones[hh], preferred_element_type=F32)
                outs.append(o * pltpu.roll(1.0 / o, V_HEAD_DIM, 1))
            o_ref[0, rows, pr * LANES:(pr + 1) * LANES] = jnp.where(lane < V_HEAD_DIM, outs[0], outs[1]).astype(BF16)


def _attn(q, k, v, *, tq):
    B, S, _ = q.shape
    qw = ATTN_HEADS_PER_STEP * HEAD_PAD
    vw = ATTN_HEADS_PER_STEP * V_HEAD_DIM
    return pl.pallas_call(
        functools.partial(_attn_kernel, tq=tq),
        grid=(B, MLA_HEADS // ATTN_HEADS_PER_STEP),
        in_specs=[
            pl.BlockSpec((1, S, qw), lambda b, h: (b, 0, h)),
            pl.BlockSpec((1, S, qw), lambda b, h: (b, 0, h)),
            pl.BlockSpec((1, S, vw), lambda b, h: (b, 0, h)),
        ],
        out_specs=pl.BlockSpec((1, S, vw), lambda b, h: (b, 0, h)),
        out_shape=jax.ShapeDtypeStruct((B, S, MLA_HEADS * V_HEAD_DIM), BF16),
        scratch_shapes=[pltpu.VMEM((ATTN_SLOTS, tq, S), F32), pltpu.VMEM((ATTN_SLOTS, tq, S), BF16)],
        compiler_params=pltpu.CompilerParams(
            dimension_semantics=("arbitrary", "arbitrary"), vmem_limit_bytes=VMEM_LIMIT_BYTES),
        name="attn",
    )(q, k, v)


def _out_proj_kernel(x_ref, att_ref, cv_ref, lng_ref, lnb_ref, wo_ref, g1_ref, b1_ref, wr_ref,
                     h1_ref, lt_ref):
    na = att_ref.shape[1]
    tm = x_ref.shape[0]
    n_lt = lt_ref.shape[1]
    sub = min(tm, TOKEN_SUB_TILE)
    for r0 in range(0, tm, sub):
        rows = slice(r0, r0 + sub)
        h0 = _layer_norm(x_ref[rows, :], lng_ref[...], lnb_ref[...])
        mix = jnp.dot(att_ref[rows, :], wo_ref[:na, :], preferred_element_type=F32)
        mix = mix + jnp.dot(cv_ref[rows, :], wo_ref[na:, :], preferred_element_type=F32)
        h1 = _layer_norm(DEEPNORM_ALPHA * h0 + mix, g1_ref[...], b1_ref[...])
        h1_ref[rows, :] = h1
        lg = jnp.dot(h1.astype(BF16), wr_ref[...], preferred_element_type=F32)
        lt_ref[0, :, rows] = lg.T[:n_lt, :]


def _out_proj(x2d, att, cv, lng, lnb, w_o, g1, b1, w_r2, *, tm, seq, n_lt):
    T, D = x2d.shape
    per = seq // tm
    full = lambda a: pl.BlockSpec(a.shape, lambda i: (0,) * a.ndim)
    return pl.pallas_call(
        _out_proj_kernel,
        grid=(T // tm,),
        in_specs=[
            pl.BlockSpec((tm, D), lambda i: (i, 0)),
            pl.BlockSpec((tm, att.shape[1]), lambda i: (i, 0)),
            pl.BlockSpec((tm, cv.shape[1]), lambda i: (i, 0)),
            full(lng), full(lnb), full(w_o), full(g1), full(b1), full(w_r2),
        ],
        out_specs=[
            pl.BlockSpec((tm, D), lambda i: (i, 0)),
            pl.BlockSpec((1, n_lt, tm), lambda i: (i // per, 0, i % per)),
        ],
        out_shape=[
            jax.ShapeDtypeStruct((T, D), F32),
            jax.ShapeDtypeStruct((T // seq, n_lt, seq), F32),
        ],
        compiler_params=pltpu.CompilerParams(
            dimension_semantics=("arbitrary",), vmem_limit_bytes=VMEM_LIMIT_BYTES),
        name="out_proj",
    )(x2d, att, cv, lng, lnb, w_o, g1, b1, w_r2)


def _route_kernel(lt_ref, rank_ref, gate_ref, off_ref, *, cap, tb):
    B, E, S = rank_ref.shape
    lt = lt_ref[:, :E, :] + lt_ref[:, E:, :]
    ex = jnp.exp(lt - jnp.max(lt, axis=1, keepdims=True))
    aff = ex / jnp.sum(ex, axis=1, keepdims=True)
    v = aff.reshape(B * E, S)
    rows = B * E
    kf = float(cap)

    def step(_, carry):
        lo, hi = carry
        mid = jnp.sqrt(jnp.maximum(lo, TOPK_SEARCH_FLOOR)) * jnp.sqrt(hi)
        cnt = jnp.sum(jnp.where(v >= mid, 1.0, 0.0), axis=1, keepdims=True)
        ge = cnt >= kf
        return jnp.where(ge, mid, lo), jnp.where(ge, hi, mid)

    lo, hi = lax.fori_loop(0, TOPK_SEARCH_STEPS, step,
                           (jnp.zeros((rows, 1), F32), jnp.full((rows, 1), 2.0, F32)))
    above = v >= hi
    tie = jnp.logical_and(v >= lo, jnp.logical_not(above))
    stacked = jnp.concatenate([jnp.where(above, 1.0, 0.0), jnp.where(tie, 1.0, 0.0)], axis=0)
    upper = jnp.where(lax.broadcasted_iota(jnp.int32, (S, S), 0) <= lax.broadcasted_iota(jnp.int32, (S, S), 1),
                      1.0, 0.0).astype(BF16)
    pc = jnp.dot(stacked.astype(BF16), upper, preferred_element_type=F32)
    pa = pc[:rows]
    pt = pc[rows:]
    need = kf - jnp.sum(jnp.where(above, 1.0, 0.0), axis=1, keepdims=True)
    sel = jnp.logical_or(above, jnp.logical_and(tie, pt <= need))
    taken = pa + jnp.minimum(pt, need)
    rank_ref[...] = jnp.where(sel, taken - 1.0, -1.0).astype(jnp.int32).reshape(B, E, S)
    gate_ref[...] = jnp.where(sel, v, 0.0).reshape(B, E, S)
    pick = jnp.where(lax.broadcasted_iota(jnp.int32, (S, LANES), 0) + 1
                     == lax.broadcasted_iota(jnp.int32, (S, LANES), 1) * tb, 1.0, 0.0).astype(BF16)
    off = jnp.dot(taken.astype(BF16), pick, preferred_element_type=F32)
    off_ref[...] = off.astype(jnp.int32).reshape(B, E, LANES)


def _route(lt2, *, cap, tb):
    B, E2, S = lt2.shape
    E = E2 // 2
    assert cap <= 256 and S // tb < LANES
    spec = pl.BlockSpec((B, E, S), lambda i: (0, 0, 0))
    ospec = pl.BlockSpec((B, E, LANES), lambda i: (0, 0, 0))
    return pl.pallas_call(
        functools.partial(_route_kernel, cap=cap, tb=tb),
        grid=(1,),
        in_specs=[pl.BlockSpec((B, E2, S), lambda i: (0, 0, 0))],
        out_specs=[spec, spec, ospec],
        out_shape=[jax.ShapeDtypeStruct((B, E, S), jnp.int32), jax.ShapeDtypeStruct((B, E, S), F32),
                   jax.ShapeDtypeStruct((B, E, LANES), jnp.int32)],
        compiler_params=pltpu.CompilerParams(
            dimension_semantics=("arbitrary",), vmem_limit_bytes=VMEM_LIMIT_BYTES),
        name="route",
    )(lt2)


ROUTE_TOKEN_BLOCK = 256
ROUTE_BLOCKS_PER_STEP = 4
SLOT_WINDOW = 64
SLOT_ALIGN = 16


def _slot_windows(off_ref, base, n_experts, cap, win):
    starts = []
    n_pass = jnp.int32(1)
    for e in range(n_experts):
        off = off_ref[base + e]
        end = off_ref[base + n_experts + e]
        start = jnp.minimum((off // SLOT_ALIGN) * SLOT_ALIGN, cap - win)
        starts.append(start)
        n_pass = jnp.maximum(n_pass, (end - start + (win - 1)) // win)
    return starts, n_pass


def _dispatch_kernel(off_ref, rank_ref, h_ref, xg_ref, *, cap, win, tb):
    E = rank_ref.shape[1]
    nsb = rank_ref.shape[2] // tb
    b, j = pl.program_id(0), pl.program_id(1)
    n_bounds = pl.num_programs(1) * nsb + 1

    @pl.when(j == 0)
    def _():
        xg_ref[...] = jnp.zeros(xg_ref.shape, BF16)

    sub = lax.broadcasted_iota(jnp.int32, (win, tb), 0)

    def one_pass(sb, starts, p, first):
        cols = slice(sb * tb, (sb + 1) * tb)
        pieces, rows = [], []
        for e in range(E):
            lo = starts[e] + p * win
            ws = pl.multiple_of(jnp.minimum(lo, cap - win), SLOT_ALIGN)
            tgt = ws + sub
            hit = rank_ref[0, e:e + 1, cols] == tgt
            if not first:
                hit = jnp.logical_and(hit, tgt >= lo)
            pieces.append(jnp.where(hit, 1.0, 0.0).astype(BF16))
            rows.append(ws)
        onehot = jnp.concatenate(pieces, axis=0)
        hb = h_ref[0, cols, :].astype(BF16)
        got = jnp.dot(onehot, hb, preferred_element_type=F32).astype(BF16)
        for e in range(E):
            xg_ref[e, 0, pl.ds(rows[e], win), :] += got[e * win:(e + 1) * win, :]

    plans = []
    for sb in range(nsb):
        starts, n_pass = _slot_windows(off_ref, (b * n_bounds + j * nsb + sb) * E, E, cap, win)
        plans.append((starts, n_pass))
        one_pass(sb, starts, 0, True)
    for sb in range(nsb):
        starts, n_pass = plans[sb]

        def extra(p, carry, sb=sb, starts=starts):
            one_pass(sb, starts, p, False)
            return carry

        lax.fori_loop(1, n_pass, extra, 0)


def _dispatch(off_flat, rank, h1b, *, cap, tb, nsb):
    B, E, S = rank.shape
    D = h1b.shape[-1]
    win = min(SLOT_WINDOW, cap)
    ts = tb * nsb
    assert cap % SLOT_ALIGN == 0 and win % SLOT_ALIGN == 0 and S % ts == 0
    return pl.pallas_call(
        functools.partial(_dispatch_kernel, cap=cap, win=win, tb=tb),
        grid_spec=pltpu.PrefetchScalarGridSpec(
            num_scalar_prefetch=1,
            grid=(B, S // ts),
            in_specs=[pl.BlockSpec((1, E, ts), lambda b, j, off: (b, 0, j)),
                      pl.BlockSpec((1, ts, D), lambda b, j, off: (b, j, 0))],
            out_specs=pl.BlockSpec((E, 1, cap, D), lambda b, j, off: (0, b, 0, 0)),
        ),
        out_shape=jax.ShapeDtypeStruct((E, B, cap, D), BF16),
        compiler_params=pltpu.CompilerParams(
            dimension_semantics=("arbitrary", "arbitrary"), vmem_limit_bytes=VMEM_LIMIT_BYTES),
        name="dispatch",
    )(off_flat, rank, h1b)


EXPERT_ROW_BLOCK = 1024


def _expert_kernel(x_ref, wg_ref, wu_ref, wd_ref, y_ref, acc_ref, wgb_ref, wub_ref, wdb_ref, *, n_chunks):
    f = pl.program_id(1)
    rows = x_ref.shape[1]
    rb = min(rows, EXPERT_ROW_BLOCK)
    last = n_chunks - 1

    def chunk(first, final):
        wgb_ref[...] = wg_ref[0].astype(BF16)
        wub_ref[...] = wu_ref[0].astype(BF16)
        wdb_ref[...] = wd_ref[0].astype(BF16)
        for r0 in range(0, rows, rb):
            x = x_ref[0, r0:r0 + rb, :]
            a = jnp.dot(x, wgb_ref[...], preferred_element_type=F32)
            u = jnp.dot(x, wub_ref[...], preferred_element_type=F32)
            hmid = (a * (1.0 / (1.0 + jnp.exp(-a))) * u).astype(BF16)
            part = jnp.dot(hmid, wdb_ref[...], preferred_element_type=F32)
            if not first:
                part = acc_ref[r0:r0 + rb, :] + part
            if final:
                y_ref[0, r0:r0 + rb, :] = part.astype(BF16)
            else:
                acc_ref[r0:r0 + rb, :] = part

    if n_chunks == 1:
        chunk(True, True)
    else:
        pl.when(f == 0)(lambda: chunk(True, False))
        pl.when(jnp.logical_and(f > 0, f < last))(lambda: chunk(False, False))
        pl.when(f == last)(lambda: chunk(False, True))


def _experts(xg, w_gate, w_up, w_down, *, tf):
    E, rows, D = xg.shape
    F = w_gate.shape[-1]
    return pl.pallas_call(
        functools.partial(_expert_kernel, n_chunks=F // tf),
        grid=(E, F // tf),
        in_specs=[
            pl.BlockSpec((1, rows, D), lambda e, f: (e, 0, 0)),
            pl.BlockSpec((1, D, tf), lambda e, f: (e, 0, f)),
            pl.BlockSpec((1, D, tf), lambda e, f: (e, 0, f)),
            pl.BlockSpec((1, tf, D), lambda e, f: (e, f, 0)),
        ],
        out_specs=pl.BlockSpec((1, rows, D), lambda e, f: (e, 0, 0)),
        out_shape=jax.ShapeDtypeStruct((E, rows, D), BF16),
        scratch_shapes=[pltpu.VMEM((rows, D), F32), pltpu.VMEM((D, tf), BF16), pltpu.VMEM((D, tf), BF16),
                        pltpu.VMEM((tf, D), BF16)],
        compiler_params=pltpu.CompilerParams(
            dimension_semantics=("arbitrary", "arbitrary"), vmem_limit_bytes=VMEM_LIMIT_BYTES),
        name="experts",
    )(xg, w_gate, w_up, w_down)


def _combine_kernel(off_ref, rank_ref, gate_ref, y_ref, h_ref, g2_ref, b2_ref, o_ref, *, cap, win, tb):
    E = rank_ref.shape[1]
    nsb = rank_ref.shape[2] // tb
    b, j = pl.program_id(0), pl.program_id(1)
    n_bounds = pl.num_programs(1) * nsb + 1
    per_tile = LANES // win
    lane = lax.broadcasted_iota(jnp.int32, (1, LANES), 1)
    first_lane = lax.broadcasted_iota(jnp.int32, (LANES, E * win), 0) * win
    col = lax.broadcasted_iota(jnp.int32, (LANES, E * win), 1)
    spread = jnp.where(jnp.logical_and(col >= first_lane, col < first_lane + win), 1.0, 0.0).astype(BF16)
    pad_rows = jnp.zeros((LANES - E, tb), F32)

    def token_major(x_et):
        return jnp.concatenate([x_et, pad_rows], axis=0).T.astype(BF16)

    def one_pass(sb, starts, p, first):
        cols = slice(sb * tb, (sb + 1) * tb)
        rk = jnp.dot(token_major(rank_ref[0, :, cols].astype(F32)), spread, preferred_element_type=F32)
        gt = jnp.dot(token_major(gate_ref[0, :, cols]), spread, preferred_element_type=F32)
        tgts, los, ys = [], [], []
        for t in range(E // per_tile):
            tgt = lo_v = None
            for u in range(per_tile):
                e = t * per_tile + u
                lo = starts[e] + p * win
                ws = pl.multiple_of(jnp.minimum(lo, cap - win), SLOT_ALIGN)
                ys.append(y_ref[e, 0, pl.ds(ws, win), :])
                t_u = ws + lane - u * win
                if u == 0:
                    tgt, lo_v = t_u, jnp.full((1, LANES), lo, jnp.int32)
                else:
                    here = lane >= u * win
                    tgt = jnp.where(here, t_u, tgt)
                    lo_v = jnp.where(here, lo, lo_v)
            tgts.append(tgt)
            los.append(lo_v)
        tgt = jnp.concatenate(tgts, axis=1)
        hit = rk == tgt.astype(F32)
        if not first:
            hit = jnp.logical_and(hit, tgt >= jnp.concatenate(los, axis=1))
        gates = jnp.where(hit, gt, 0.0).astype(BF16)
        return jnp.dot(gates, jnp.concatenate(ys, axis=0), preferred_element_type=F32)

    plans = []
    for sb in range(nsb):
        rows = slice(sb * tb, (sb + 1) * tb)
        starts, n_pass = _slot_windows(off_ref, (b * n_bounds + j * nsb + sb) * E, E, cap, win)
        plans.append((starts, n_pass))
        o_ref[0, rows, :] = DEEPNORM_ALPHA * h_ref[0, rows, :] + one_pass(sb, starts, 0, True)
    for sb in range(nsb):
        starts, n_pass = plans[sb]

        def extra(p, carry, sb=sb, starts=starts):
            o_ref[0, sb * tb:(sb + 1) * tb, :] += one_pass(sb, starts, p, False)
            return carry

        lax.fori_loop(1, n_pass, extra, 0)
    o_ref[0] = _layer_norm(o_ref[0], g2_ref[...], b2_ref[...])


def _combine(off_flat, rank, gate, y, h1, g2, b2, *, tb, nsb):
    B, E, S = rank.shape
    cap, D = y.shape[2], y.shape[3]
    win = min(SLOT_WINDOW, cap)
    ts = tb * nsb
    assert cap % SLOT_ALIGN == 0 and win % SLOT_ALIGN == 0 and LANES % win == 0 and E % (LANES // win) == 0
    assert S % ts == 0
    full = lambda a: pl.BlockSpec(a.shape, lambda b, j, off: (0,) * a.ndim)
    return pl.pallas_call(
        functools.partial(_combine_kernel, cap=cap, win=win, tb=tb),
        grid_spec=pltpu.PrefetchScalarGridSpec(
            num_scalar_prefetch=1,
            grid=(B, S // ts),
            in_specs=[
                pl.BlockSpec((1, E, ts), lambda b, j, off: (b, 0, j)),
                pl.BlockSpec((1, E, ts), lambda b, j, off: (b, 0, j)),
                pl.BlockSpec((E, 1, cap, D), lambda b, j, off: (0, b, 0, 0)),
                pl.BlockSpec((1, ts, D), lambda b, j, off: (b, j, 0)),
                full(g2), full(b2),
            ],
            out_specs=pl.BlockSpec((1, ts, D), lambda b, j, off: (b, j, 0)),
        ),
        out_shape=jax.ShapeDtypeStruct((B, S, D), F32),
        compiler_params=pltpu.CompilerParams(
            dimension_semantics=("arbitrary", "arbitrary"), vmem_limit_bytes=VMEM_LIMIT_BYTES),
        name="combine",
    )(off_flat, rank, gate, y, h1, g2, b2)


def _tile(n, target):
    t = min(n, target)
    assert n % t == 0, (n, t)
    return t


def kernel(x, positions, emb_ln_g, emb_ln_b, w_in, q_norm_g, w_qb, kv_norm_g, w_kvb, conv_w, conv_b,
           conv_ln_g, conv_ln_b, w_o, ln1_g, ln1_b, w_router, w_gate, w_up, w_down, ln2_g, ln2_b):
    B, S, D = x.shape
    T = B * S
    H = MLA_HEADS
    q_rank = q_norm_g.shape[-1]
    kv_rank = kv_norm_g.shape[-1]
    conv_ch = conv_w.shape[-1]
    qk_dim = QK_NOPE_DIM + QK_ROPE_DIM
    cap = CAPACITY_FACTOR * S // N_EXPERTS
    assert w_in.shape[0] == DEPTH == 1
    row = lambda a: a.reshape(1, -1)

    wi = w_in[0]
    c1, c2, c3 = q_rank, q_rank + kv_rank, q_rank + kv_rank + QK_ROPE_DIM
    tail = LANES - QK_NOPE_DIM - QK_ROPE_DIM
    kr_cols = jnp.pad(wi[:, c2:c3], ((0, 0), (QK_NOPE_DIM, tail)))
    w_in_r = jnp.concatenate(
        [wi[:, :c2], wi[:, c3:c3 + conv_ch], wi[:, c3 + conv_ch:], kr_cols], axis=1).astype(BF16)
    w_qb_pad = jnp.pad(w_qb[0].reshape(q_rank, H, qk_dim),
                       ((0, 0), (0, 0), (0, HEAD_PAD - qk_dim))).reshape(q_rank, H * HEAD_PAD).astype(BF16)
    wkv = w_kvb[0].reshape(kv_rank, H, QK_NOPE_DIM + V_HEAD_DIM)
    wk_pad = jnp.pad(wkv[:, :, :QK_NOPE_DIM], ((0, 0), (0, 0), (0, HEAD_PAD - QK_NOPE_DIM)))
    w_kv_r = jnp.concatenate([wk_pad.reshape(kv_rank, H * HEAD_PAD),
                              wkv[:, :, QK_NOPE_DIM:].reshape(kv_rank, H * V_HEAD_DIM)], axis=1).astype(BF16)
    half = QK_ROPE_DIM // 2
    inv_freq = (ROPE_THETA ** (-jnp.arange(half, dtype=F32) / half)).reshape(half, 1)

    x2d = x.reshape(T, D)
    tm = _tile(S, 512)
    q, k, v, hc = _in_proj(x2d, positions.reshape(1, T), inv_freq, row(emb_ln_g), row(emb_ln_b), w_in_r,
                           row(q_norm_g[0]), w_qb_pad, row(kv_norm_g[0]), w_kv_r, tm=tm)
    n_ct = conv_ch // LANES
    cv = _conv(hc.reshape(B, S, conv_ch), jnp.swapaxes(conv_w[0].reshape(CONV_WIDTH, n_ct, LANES), 0, 1),
               conv_b[0].reshape(n_ct, 1, LANES), row(conv_ln_g[0]), row(conv_ln_b[0]))
    att = _attn(q.reshape(B, S, H * HEAD_PAD), k.reshape(B, S, H * HEAD_PAD),
                v.reshape(B, S, H * V_HEAD_DIM), tq=_tile(S, 512))
    wr_hi = w_router[0].astype(BF16)
    w_r2 = jnp.concatenate([wr_hi, (w_router[0] - wr_hi.astype(F32)).astype(BF16)], axis=1)
    w_r2 = jnp.pad(w_r2, ((0, 0), (0, LANES - 2 * N_EXPERTS)))
    h1, lt2 = _out_proj(x2d, att.reshape(T, H * V_HEAD_DIM), cv.reshape(T, conv_ch), row(emb_ln_g),
                        row(emb_ln_b), w_o[0].astype(BF16), row(ln1_g[0]), row(ln1_b[0]),
                        w_r2, tm=_tile(S, 2048), seq=S, n_lt=2 * N_EXPERTS)
    h1 = h1.reshape(B, S, D)
    tb = _tile(S, ROUTE_TOKEN_BLOCK)
    rank, gate, off = _route(lt2, cap=cap, tb=tb)
    off_flat = jnp.swapaxes(off[:, :, :S // tb + 1], 1, 2).reshape(-1)
    nsb = _tile(S // tb, ROUTE_BLOCKS_PER_STEP)
    xg = _dispatch(off_flat, rank, h1, cap=cap, tb=tb, nsb=nsb)
    y = _experts(xg.reshape(N_EXPERTS, B * cap, D), w_gate[0], w_up[0], w_down[0],
                 tf=_tile(w_gate.shape[-1], 512)).reshape(N_EXPERTS, B, cap, D)
    return _combine(off_flat, rank, gate, y, h1, row(ln2_g[0]), row(ln2_b[0]), tb=tb, nsb=nsb)
```

```python
import functools
import math

import jax
import jax.numpy as jnp
from jax import lax
from jax.experimental import pallas as pl
from jax.experimental.pallas import tpu as pltpu

F32 = jnp.float32
BF16 = jnp.bfloat16

MLA_HEADS = 8
QK_NOPE_DIM = 64
QK_ROPE_DIM = 32
V_HEAD_DIM = 64
CONV_WIDTH = 31
ROPE_THETA = 10000.0
N_EXPERTS = 16
CAPACITY_FACTOR = 2
DEPTH = 1
DEEPNORM_ALPHA = (2.0 * DEPTH) ** 0.25
LN_EPS = 1e-5
RMS_EPS = 1e-6
LOG2_E = math.log2(math.e)

LANES = 128
SUBLANES = 8
HEAD_PAD = LANES
VMEM_LIMIT_BYTES = 56 * 1024 * 1024
TOKEN_SUB_TILE = 256

TOPK_SEARCH_STEPS = 36
TOPK_SEARCH_FLOOR = 1e-30


def _layer_norm(x, g, b):
    mu = jnp.mean(x, axis=-1, keepdims=True)
    xc = x - mu
    var = jnp.mean(xc * xc, axis=-1, keepdims=True)
    return xc * lax.rsqrt(var + LN_EPS) * g + b


def _rms_norm(x, g):
    return x * lax.rsqrt(jnp.mean(x * x, axis=-1, keepdims=True) + RMS_EPS) * g


def _rope_tile(x, cos_t, sin_t, lane):
    half = QK_ROPE_DIM // 2
    fwd = pltpu.roll(x, LANES - half, 1)
    bwd = pltpu.roll(x, half, 1)
    partner = jnp.where(lane < QK_NOPE_DIM + half, fwd, bwd)
    return x * cos_t + partner * sin_t


def _in_proj_kernel(x_ref, pos_ref, invf_ref, lng_ref, lnb_ref, win_ref, qg_ref, wqb_ref,
                    kvg_ref, wkv_ref, q_ref, k_ref, v_ref, hc_ref, *, q_rank, kv_rank, conv_ch):
    tm = x_ref.shape[0]
    sub = min(tm, TOKEN_SUB_TILE)
    c1 = q_rank
    c2 = c1 + kv_rank
    c3 = c2 + conv_ch
    c4 = c3 + conv_ch
    tail = LANES - QK_NOPE_DIM - QK_ROPE_DIM
    nk = MLA_HEADS * HEAD_PAD
    scale = (QK_NOPE_DIM + QK_ROPE_DIM) ** -0.5 * LOG2_E
    ones = jnp.ones((QK_NOPE_DIM, sub), F32)
    zeros = jnp.zeros((QK_NOPE_DIM, sub), F32)
    lane = lax.broadcasted_iota(jnp.int32, (sub, LANES), 1)
    for r0 in range(0, tm, sub):
        rows = slice(r0, r0 + sub)
        h = _layer_norm(x_ref[rows, :], lng_ref[...], lnb_ref[...])
        proj = jnp.dot(h.astype(BF16), win_ref[...], preferred_element_type=F32)
        cq = proj[:, :c1]
        ckv = proj[:, c1:c2]
        a = proj[:, c2:c3]
        g = proj[:, c3:c4]
        kr = proj[:, c4:c4 + LANES]

        ang = pos_ref[:, rows].astype(F32) * invf_ref[...]
        cos = jnp.cos(ang)
        sin = jnp.sin(ang)
        cos_t = jnp.concatenate([ones, cos, cos, ones[:tail]], axis=0).T
        sin_t = jnp.concatenate([zeros, -sin, sin, zeros[:tail]], axis=0).T

        cqn = _rms_norm(cq, qg_ref[...])
        q = jnp.dot(cqn.astype(BF16), wqb_ref[...], preferred_element_type=F32) * scale
        ckvn = _rms_norm(ckv, kvg_ref[...])
        kv = jnp.dot(ckvn.astype(BF16), wkv_ref[...], preferred_element_type=F32)
        k_pe = _rope_tile(kr, cos_t, sin_t, lane)
        for hd in range(MLA_HEADS):
            sl = slice(hd * HEAD_PAD, (hd + 1) * HEAD_PAD)
            q_ref[rows, sl] = _rope_tile(q[:, sl], cos_t, sin_t, lane).astype(BF16)
            k_ref[rows, sl] = (kv[:, sl] + k_pe).astype(BF16)
        v_ref[rows, :] = kv[:, nk:].astype(BF16)
        hc_ref[rows, :] = a * (1.0 / (1.0 + jnp.exp(-g)))


def _in_proj(x2d, pos_row, invf_col, lng, lnb, w_in_r, qg, w_qb_pad, kvg, w_kv_r, *, tm):
    T, D = x2d.shape
    q_rank = w_qb_pad.shape[0]
    kv_rank = w_kv_r.shape[0]
    conv_ch = (w_in_r.shape[1] - q_rank - kv_rank - LANES) // 2
    nk = MLA_HEADS * HEAD_PAD
    nv = MLA_HEADS * V_HEAD_DIM
    full = lambda a: pl.BlockSpec(a.shape, lambda i: (0,) * a.ndim)
    return pl.pallas_call(
        functools.partial(_in_proj_kernel, q_rank=q_rank, kv_rank=kv_rank, conv_ch=conv_ch),
        grid=(T // tm,),
        in_specs=[
            pl.BlockSpec((tm, D), lambda i: (i, 0)),
            pl.BlockSpec((1, tm), lambda i: (0, i)),
            full(invf_col), full(lng), full(lnb), full(w_in_r), full(qg), full(w_qb_pad),
            full(kvg), full(w_kv_r),
        ],
        out_specs=[
            pl.BlockSpec((tm, nk), lambda i: (i, 0)),
            pl.BlockSpec((tm, nk), lambda i: (i, 0)),
            pl.BlockSpec((tm, nv), lambda i: (i, 0)),
            pl.BlockSpec((tm, conv_ch), lambda i: (i, 0)),
        ],
        out_shape=[
            jax.ShapeDtypeStruct((T, nk), BF16),
            jax.ShapeDtypeStruct((T, nk), BF16),
            jax.ShapeDtypeStruct((T, nv), BF16),
            jax.ShapeDtypeStruct((T, conv_ch), F32),
        ],
        compiler_params=pltpu.CompilerParams(
            dimension_semantics=("arbitrary",), vmem_limit_bytes=VMEM_LIMIT_BYTES),
        name="in_proj",
    )(x2d, pos_row, invf_col, lng, lnb, w_in_r, qg, w_qb_pad, kvg, w_kv_r)


CONV_PAD_ROWS = 16
CONV_ROW_CHUNK = 128
CONV_CHUNKS_PER_STEP = 4
CONV_WINDOW = CONV_ROW_CHUNK + 2 * CONV_PAD_ROWS


def _conv_kernel(hc_ref, cw_ref, cb_ref, g_ref, b_ref, o_ref, win_ref, y_ref):
    S, C = hc_ref.shape[1], hc_ref.shape[2]
    n_chunks = S // CONV_ROW_CHUNK
    for r in range(n_chunks):
        lo = r * CONV_ROW_CHUNK - CONV_PAD_ROWS
        hi = lo + CONV_WINDOW
        src_lo, src_hi = max(lo, 0), min(hi, S)
        for c in range(C // LANES):
            if lo < 0:
                win_ref[r, c, 0:-lo, :] = jnp.zeros((-lo, LANES), F32)
            if hi > S:
                win_ref[r, c, CONV_WINDOW - (hi - S):, :] = jnp.zeros((hi - S, LANES), F32)
            win_ref[r, c, src_lo - lo:src_hi - lo, :] = hc_ref[0, src_lo:src_hi, c * LANES:(c + 1) * LANES]
    first = CONV_PAD_ROWS - CONV_WIDTH // 2

    n_tiles = C // LANES
    group = min(CONV_CHUNKS_PER_STEP, n_chunks)

    def chunk_group(rg, carry):
        sums = []
        for u in range(group):
            r = rg * group + u

            def tile_conv(c, s1, r=r, u=u):
                acc = jnp.zeros((CONV_ROW_CHUNK, LANES), F32)
                for j in range(CONV_WIDTH):
                    acc = acc + win_ref[r, c, first + j:first + j + CONV_ROW_CHUNK, :] * cw_ref[c, j:j + 1, :]
                acc = acc + cb_ref[c]
                y_ref[u, c] = acc
                return s1 + acc

            sums.append(lax.fori_loop(0, n_tiles, tile_conv, jnp.zeros((CONV_ROW_CHUNK, LANES), F32)))
        for u in range(group):
            base = pl.multiple_of((rg * group + u) * CONV_ROW_CHUNK, CONV_ROW_CHUNK)
            mu = jnp.sum(sums[u], axis=-1, keepdims=True) * (1.0 / C)
            s2 = jnp.zeros((CONV_ROW_CHUNK, LANES), F32)
            for c in range(n_tiles):
                yc = y_ref[u, c] - mu
                s2 = s2 + yc * yc
            inv = lax.rsqrt(jnp.sum(s2, axis=-1, keepdims=True) * (1.0 / C) + LN_EPS)
            for c in range(n_tiles):
                lanes = slice(c * LANES, (c + 1) * LANES)
                y = (y_ref[u, c] - mu) * inv * g_ref[:, lanes] + b_ref[:, lanes]
                o_ref[0, pl.ds(base, CONV_ROW_CHUNK), lanes] = (y * (1.0 / (1.0 + jnp.exp(-y)))).astype(BF16)
        return carry

    lax.fori_loop(0, n_chunks // group, chunk_group, 0)


def _conv(hc, cw, cb, g, b):
    B, S, C = hc.shape
    full = lambda a: pl.BlockSpec(a.shape, lambda i: (0,) * a.ndim)
    return pl.pallas_call(
        _conv_kernel,
        grid=(B,),
        in_specs=[pl.BlockSpec((1, S, C), lambda i: (i, 0, 0)), full(cw), full(cb), full(g), full(b)],
        out_specs=pl.BlockSpec((1, S, C), lambda i: (i, 0, 0)),
        out_shape=jax.ShapeDtypeStruct((B, S, C), BF16),
        scratch_shapes=[pltpu.VMEM((S // CONV_ROW_CHUNK, C // LANES, CONV_WINDOW, LANES), F32),
                        pltpu.VMEM((CONV_CHUNKS_PER_STEP, C // LANES, CONV_ROW_CHUNK, LANES), F32)],
        compiler_params=pltpu.CompilerParams(
            dimension_semantics=("arbitrary",), vmem_limit_bytes=VMEM_LIMIT_BYTES),
        name="conv",
    )(hc, cw, cb, g, b)


HEADS_PER_TILE = LANES // V_HEAD_DIM
ATTN_HEADS_PER_STEP = 4


ATTN_SLOTS = 2


def _attn_kernel(q_ref, k_ref, v_ref, o_ref, s_ref, p_ref, *, tq):
    S = q_ref.shape[1]
    assert HEADS_PER_TILE == 2
    lane = lax.broadcasted_iota(jnp.int32, (tq, LANES), 1)
    key_lane = lax.broadcasted_iota(jnp.int32, (S, LANES), 1)
    item = 0
    for pr in range(v_ref.shape[2] // LANES):
        v = v_ref[0, :, pr * LANES:(pr + 1) * LANES]
        one = jnp.ones((S, LANES), BF16)
        v_ones = [jnp.where(key_lane < V_HEAD_DIM, v, one), jnp.where(key_lane < V_HEAD_DIM, one, v)]
        for qi in range(S // tq):
            rows = slice(qi * tq, (qi + 1) * tq)
            outs = []
            for hh in range(HEADS_PER_TILE):
                slot = item % ATTN_SLOTS
                item += 1
                c0 = (pr * HEADS_PER_TILE + hh) * HEAD_PAD
                s_ref[slot] = lax.dot_general(q_ref[0, rows, c0:c0 + HEAD_PAD], k_ref[0, :, c0:c0 + HEAD_PAD],
                                              (((1,), (1,)), ((), ())), preferred_element_type=F32)
                s = s_ref[slot]
                p_ref[slot] = jnp.exp2(s - jnp.max(s, axis=-1, keepdims=True)).astype(BF16)
                o = jnp.dot(p_ref[slot], v_ones[hh], preferred_element_type=F32)
                outs.append(o * pltpu.roll(1.0 / o, V_HEAD_DIM, 1))
            o_ref[0, rows, pr * LANES:(pr + 1) * LANES] = jnp.where(lane < V_HEAD_DIM, outs[0], outs[1]).astype(BF16)


def _attn(q, k, v, *, tq):
    B, S, _ = q.shape
    qw = ATTN_HEADS_PER_STEP * HEAD_PAD
    vw = ATTN_HEADS_PER_STEP * V_HEAD_DIM
    return pl.pallas_call(
        functools.partial(_attn_kernel, tq=tq),
        grid=(B, MLA_HEADS // ATTN_HEADS_PER_STEP),
        in_specs=[
            pl.BlockSpec((1, S, qw), lambda b, h: (b, 0, h)),
            pl.BlockSpec((1, S, qw), lambda b, h: (b, 0, h)),
            pl.BlockSpec((1, S, vw), lambda b, h: (b, 0, h)),
        ],
        out_specs=pl.BlockSpec((1, S, vw), lambda b, h: (b, 0, h)),
        out_shape=jax.ShapeDtypeStruct((B, S, MLA_HEADS * V_HEAD_DIM), BF16),
        scratch_shapes=[pltpu.VMEM((ATTN_SLOTS, tq, S), F32), pltpu.VMEM((ATTN_SLOTS, tq, S), BF16)],
        compiler_params=pltpu.CompilerParams(
            dimension_semantics=("arbitrary", "arbitrary"), vmem_limit_bytes=VMEM_LIMIT_BYTES),
        name="attn",
    )(q, k, v)


def _out_proj_kernel(x_ref, att_ref, cv_ref, lng_ref, lnb_ref, wo_ref, g1_ref, b1_ref, wr_ref,
                     h1_ref, lt_ref):
    na = att_ref.shape[1]
    tm = x_ref.shape[0]
    n_lt = lt_ref.shape[1]
    sub = min(tm, TOKEN_SUB_TILE)
    for r0 in range(0, tm, sub):
        rows = slice(r0, r0 + sub)
        h0 = _layer_norm(x_ref[rows, :], lng_ref[...], lnb_ref[...])
        mix = jnp.dot(att_ref[rows, :], wo_ref[:na, :], preferred_element_type=F32)
        mix = mix + jnp.dot(cv_ref[rows, :], wo_ref[na:, :], preferred_element_type=F32)
        h1 = _layer_norm(DEEPNORM_ALPHA * h0 + mix, g1_ref[...], b1_ref[...])
        h1_ref[rows, :] = h1
        lg = jnp.dot(h1.astype(BF16), wr_ref[...], preferred_element_type=F32)
        lt_ref[0, :, rows] = lg.T[:n_lt, :]


def _out_proj(x2d, att, cv, lng, lnb, w_o, g1, b1, w_r2, *, tm, seq, n_lt):
    T, D = x2d.shape
    per = seq // tm
    full = lambda a: pl.BlockSpec(a.shape, lambda i: (0,) * a.ndim)
    return pl.pallas_call(
        _out_proj_kernel,
        grid=(T // tm,),
        in_specs=[
            pl.BlockSpec((tm, D), lambda i: (i, 0)),
            pl.BlockSpec((tm, att.shape[1]), lambda i: (i, 0)),
            pl.BlockSpec((tm, cv.shape[1]), lambda i: (i, 0)),
            full(lng), full(lnb), full(w_o), full(g1), full(b1), full(w_r2),
        ],
        out_specs=[
            pl.BlockSpec((tm, D), lambda i: (i, 0)),
            pl.BlockSpec((1, n_lt, tm), lambda i: (i // per, 0, i % per)),
        ],
        out_shape=[
            jax.ShapeDtypeStruct((T, D), F32),
            jax.ShapeDtypeStruct((T // seq, n_lt, seq), F32),
        ],
        compiler_params=pltpu.CompilerParams(
            dimension_semantics=("arbitrary",), vmem_limit_bytes=VMEM_LIMIT_BYTES),
        name="out_proj",
    )(x2d, att, cv, lng, lnb, w_o, g1, b1, w_r2)


def _route_kernel(lt_ref, rank_ref, gate_ref, off_ref, *, cap, tb):
    B, E, S = rank_ref.shape
    lt = lt_ref[:, :E, :] + lt_ref[:, E:, :]
    ex = jnp.exp(lt - jnp.max(lt, axis=1, keepdims=True))
    aff = ex / jnp.sum(ex, axis=1, keepdims=True)
    v = aff.reshape(B * E, S)
    rows = B * E
    kf = float(cap)

    def step(_, carry):
        lo, hi = carry
        mid = jnp.sqrt(jnp.maximum(lo, TOPK_SEARCH_FLOOR)) * jnp.sqrt(hi)
        cnt = jnp.sum(jnp.where(v >= mid, 1.0, 0.0), axis=1, keepdims=True)
        ge = cnt >= kf
        return jnp.where(ge, mid, lo), jnp.where(ge, hi, mid)

    lo, hi = lax.fori_loop(0, TOPK_SEARCH_STEPS, step,
                           (jnp.zeros((rows, 1), F32), jnp.full((rows, 1), 2.0, F32)))
    above = v >= hi
    tie = jnp.logical_and(v >= lo, jnp.logical_not(above))
    stacked = jnp.concatenate([jnp.where(above, 1.0, 0.0), jnp.where(tie, 1.0, 0.0)], axis=0)
    upper = jnp.where(lax.broadcasted_iota(jnp.int32, (S, S), 0) <= lax.broadcasted_iota(jnp.int32, (S, S), 1),
                      1.0, 0.0).astype(BF16)
    pc = jnp.dot(stacked.astype(BF16), upper, preferred_element_type=F32)
    pa = pc[:rows]
    pt = pc[rows:]
    need = kf - jnp.sum(jnp.where(above, 1.0, 0.0), axis=1, keepdims=True)
    sel = jnp.logical_or(above, jnp.logical_and(tie, pt <= need))
    taken = pa + jnp.minimum(pt, need)
    rank_ref[...] = jnp.where(sel, taken - 1.0, -1.0).astype(jnp.int32).reshape(B, E, S)
    gate_ref[...] = jnp.where(sel, v, 0.0).reshape(B, E, S)
    pick = jnp.where(lax.broadcasted_iota(jnp.int32, (S, LANES), 0) + 1
                     == lax.broadcasted_iota(jnp.int32, (S, LANES), 1) * tb, 1.0, 0.0).astype(BF16)
    off = jnp.dot(taken.astype(BF16), pick, preferred_element_type=F32)
    off_ref[...] = off.astype(jnp.int32).reshape(B, E, LANES)


def _route(lt2, *, cap, tb):
    B, E2, S = lt2.shape
    E = E2 // 2
    assert cap <= 256 and S // tb < LANES
    spec = pl.BlockSpec((B, E, S), lambda i: (0, 0, 0))
    ospec = pl.BlockSpec((B, E, LANES), lambda i: (0, 0, 0))
    return pl.pallas_call(
        functools.partial(_route_kernel, cap=cap, tb=tb),
        grid=(1,),
        in_specs=[pl.BlockSpec((B, E2, S), lambda i: (0, 0, 0))],
        out_specs=[spec, spec, ospec],
        out_shape=[jax.ShapeDtypeStruct((B, E, S), jnp.int32), jax.ShapeDtypeStruct((B, E, S), F32),
                   jax.ShapeDtypeStruct((B, E, LANES), jnp.int32)],
        compiler_params=pltpu.CompilerParams(
            dimension_semantics=("arbitrary",), vmem_limit_bytes=VMEM_LIMIT_BYTES),
        name="route",
    )(lt2)


ROUTE_TOKEN_BLOCK = 256
ROUTE_BLOCKS_PER_STEP = 4
SLOT_WINDOW = 64
SLOT_ALIGN = 16


def _slot_windows(off_ref, base, n_experts, cap, win):
    starts = []
    n_pass = jnp.int32(1)
    for e in range(n_experts):
        off = off_ref[base + e]
        end = off_ref[base + n_experts + e]
        start = jnp.minimum((off // SLOT_ALIGN) * SLOT_ALIGN, cap - win)
        starts.append(start)
        n_pass = jnp.maximum(n_pass, (end - start + (win - 1)) // win)
    return starts, n_pass


def _dispatch_kernel(off_ref, rank_ref, h_ref, xg_ref, *, cap, win, tb):
    E = rank_ref.shape[1]
    nsb = rank_ref.shape[2] // tb
    b, j = pl.program_id(0), pl.program_id(1)
    n_bounds = pl.num_programs(1) * nsb + 1

    @pl.when(j == 0)
    def _():
        xg_ref[...] = jnp.zeros(xg_ref.shape, BF16)

    sub = lax.broadcasted_iota(jnp.int32, (win, tb), 0)

    def one_pass(sb, starts, p, first):
        cols = slice(sb * tb, (sb + 1) * tb)
        pieces, rows = [], []
        for e in range(E):
            lo = starts[e] + p * win
            ws = pl.multiple_of(jnp.minimum(lo, cap - win), SLOT_ALIGN)
            tgt = ws + sub
            hit = rank_ref[0, e:e + 1, cols] == tgt
            if not first:
                hit = jnp.logical_and(hit, tgt >= lo)
            pieces.append(jnp.where(hit, 1.0, 0.0).astype(BF16))
            rows.append(ws)
        onehot = jnp.concatenate(pieces, axis=0)
        hb = h_ref[0, cols, :].astype(BF16)
        got = jnp.dot(onehot, hb, preferred_element_type=F32).astype(BF16)
        for e in range(E):
            xg_ref[e, 0, pl.ds(rows[e], win), :] += got[e * win:(e + 1) * win, :]

    plans = []
    for sb in range(nsb):
        starts, n_pass = _slot_windows(off_ref, (b * n_bounds + j * nsb + sb) * E, E, cap, win)
        plans.append((starts, n_pass))
        one_pass(sb, starts, 0, True)
    for sb in range(nsb):
        starts, n_pass = plans[sb]

        def extra(p, carry, sb=sb, starts=starts):
            one_pass(sb, starts, p, False)
            return carry

        lax.fori_loop(1, n_pass, extra, 0)


def _dispatch(off_flat, rank, h1b, *, cap, tb, nsb):
    B, E, S = rank.shape
    D = h1b.shape[-1]
    win = min(SLOT_WINDOW, cap)
    ts = tb * nsb
    assert cap % SLOT_ALIGN == 0 and win % SLOT_ALIGN == 0 and S % ts == 0
    return pl.pallas_call(
        functools.partial(_dispatch_kernel, cap=cap, win=win, tb=tb),
        grid_spec=pltpu.PrefetchScalarGridSpec(
            num_scalar_prefetch=1,
            grid=(B, S // ts),
            in_specs=[pl.BlockSpec((1, E, ts), lambda b, j, off: (b, 0, j)),
                      pl.BlockSpec((1, ts, D), lambda b, j, off: (b, j, 0))],
            out_specs=pl.BlockSpec((E, 1, cap, D), lambda b, j, off: (0, b, 0, 0)),
        ),
        out_shape=jax.ShapeDtypeStruct((E, B, cap, D), BF16),
        compiler_params=pltpu.CompilerParams(
            dimension_semantics=("arbitrary", "arbitrary"), vmem_limit_bytes=VMEM_LIMIT_BYTES),
        name="dispatch",
    )(off_flat, rank, h1b)


EXPERT_ROW_BLOCK = 1024


def _expert_kernel(x_ref, wg_ref, wu_ref, wd_ref, y_ref, acc_ref, wgb_ref, wub_ref, wdb_ref, *, n_chunks):
    f = pl.program_id(1)
    rows = x_ref.shape[1]
    rb = min(rows, EXPERT_ROW_BLOCK)
    last = n_chunks - 1

    def chunk(first, final):
        wgb_ref[...] = wg_ref[0].astype(BF16)
        wub_ref[...] = wu_ref[0].astype(BF16)
        wdb_ref[...] = wd_ref[0].astype(BF16)
        for r0 in range(0, rows, rb):
            x = x_ref[0, r0:r0 + rb, :]
            a = jnp.dot(x, wgb_ref[...], preferred_element_type=F32)
            u = jnp.dot(x, wub_ref[...], preferred_element_type=F32)
            hmid = (a * (1.0 / (1.0 + jnp.exp(-a))) * u).astype(BF16)
            part = jnp.dot(hmid, wdb_ref[...], preferred_element_type=F32)
            if not first:
                part = acc_ref[r0:r0 + rb, :] + part
            if final:
                y_ref[0, r0:r0 + rb, :] = part.astype(BF16)
            else:
                acc_ref[r0:r0 + rb, :] = part

    if n_chunks == 1:
        chunk(True, True)
    else:
        pl.when(f == 0)(lambda: chunk(True, False))
        pl.when(jnp.logical_and(f > 0, f < last))(lambda: chunk(False, False))
        pl.when(f == last)(lambda: chunk(False, True))


def _experts(xg, w_gate, w_up, w_down, *, tf):
    E, rows, D = xg.shape
    F = w_gate.shape[-1]
    return pl.pallas_call(
        functools.partial(_expert_kernel, n_chunks=F // tf),
        grid=(E, F // tf),
        in_specs=[
            pl.BlockSpec((1, rows, D), lambda e, f: (e, 0, 0)),
            pl.BlockSpec((1, D, tf), lambda e, f: (e, 0, f)),
            pl.BlockSpec((1, D, tf), lambda e, f: (e, 0, f)),
            pl.BlockSpec((1, tf, D), lambda e, f: (e, f, 0)),
        ],
        out_specs=pl.BlockSpec((1, rows, D), lambda e, f: (e, 0, 0)),
        out_shape=jax.ShapeDtypeStruct((E, rows, D), BF16),
        scratch_shapes=[pltpu.VMEM((rows, D), F32), pltpu.VMEM((D, tf), BF16), pltpu.VMEM((D, tf), BF16),
                        pltpu.VMEM((tf, D), BF16)],
        compiler_params=pltpu.CompilerParams(
            dimension_semantics=("arbitrary", "arbitrary"), vmem_limit_bytes=VMEM_LIMIT_BYTES),
        name="experts",
    )(xg, w_gate, w_up, w_down)


def _combine_kernel(off_ref, rank_ref, gate_ref, y_ref, h_ref, g2_ref, b2_ref, o_ref, *, cap, win, tb):
    E = rank_ref.shape[1]
    nsb = rank_ref.shape[2] // tb
    b, j = pl.program_id(0), pl.program_id(1)
    n_bounds = pl.num_programs(1) * nsb + 1
    per_tile = LANES // win
    lane = lax.broadcasted_iota(jnp.int32, (1, LANES), 1)
    first_lane = lax.broadcasted_iota(jnp.int32, (LANES, E * win), 0) * win
    col = lax.broadcasted_iota(jnp.int32, (LANES, E * win), 1)
    spread = jnp.where(jnp.logical_and(col >= first_lane, col < first_lane + win), 1.0, 0.0).astype(BF16)
    pad_rows = jnp.zeros((LANES - E, tb), F32)

    def token_major(x_et):
        return jnp.concatenate([x_et, pad_rows], axis=0).T.astype(BF16)

    def one_pass(sb, starts, p, first):
        cols = slice(sb * tb, (sb + 1) * tb)
        rk = jnp.dot(token_major(rank_ref[0, :, cols].astype(F32)), spread, preferred_element_type=F32)
        gt = jnp.dot(token_major(gate_ref[0, :, cols]), spread, preferred_element_type=F32)
        tgts, los, ys = [], [], []
        for t in range(E // per_tile):
            tgt = lo_v = None
            for u in range(per_tile):
                e = t * per_tile + u
                lo = starts[e] + p * win
                ws = pl.multiple_of(jnp.minimum(lo, cap - win), SLOT_ALIGN)
                ys.append(y_ref[e, 0, pl.ds(ws, win), :])
                t_u = ws + lane - u * win
                if u == 0:
                    tgt, lo_v = t_u, jnp.full((1, LANES), lo, jnp.int32)
                else:
                    here = lane >= u * win
                    tgt = jnp.where(here, t_u, tgt)
                    lo_v = jnp.where(here, lo, lo_v)
            tgts.append(tgt)
            los.append(lo_v)
        tgt = jnp.concatenate(tgts, axis=1)
        hit = rk == tgt.astype(F32)
        if not first:
            hit = jnp.logical_and(hit, tgt >= jnp.concatenate(los, axis=1))
        gates = jnp.where(hit, gt, 0.0).astype(BF16)
        return jnp.dot(gates, jnp.concatenate(ys, axis=0), preferred_element_type=F32)

    plans = []
    for sb in range(nsb):
        rows = slice(sb * tb, (sb + 1) * tb)
        starts, n_pass = _slot_windows(off_ref, (b * n_bounds + j * nsb + sb) * E, E, cap, win)
        plans.append((starts, n_pass))
        o_ref[0, rows, :] = DEEPNORM_ALPHA * h_ref[0, rows, :] + one_pass(sb, starts, 0, True)
    for sb in range(nsb):
        starts, n_pass = plans[sb]

        def extra(p, carry, sb=sb, starts=starts):
            o_ref[0, sb * tb:(sb + 1) * tb, :] += one_pass(sb, starts, p, False)
            return carry

        lax.fori_loop(1, n_pass, extra, 0)
    o_ref[0] = _layer_norm(o_ref[0], g2_ref[...], b2_ref[...])


def _combine(off_flat, rank, gate, y, h1, g2, b2, *, tb, nsb):
    B, E, S = rank.shape
    cap, D = y.shape[2], y.shape[3]
    win = min(SLOT_WINDOW, cap)
    ts = tb * nsb
    assert cap % SLOT_ALIGN == 0 and win % SLOT_ALIGN == 0 and LANES % win == 0 and E % (LANES // win) == 0
    assert S % ts == 0
    full = lambda a: pl.BlockSpec(a.shape, lambda b, j, off: (0,) * a.ndim)
    return pl.pallas_call(
        functools.partial(_combine_kernel, cap=cap, win=win, tb=tb),
        grid_spec=pltpu.PrefetchScalarGridSpec(
            num_scalar_prefetch=1,
            grid=(B, S // ts),
            in_specs=[
                pl.BlockSpec((1, E, ts), lambda b, j, off: (b, 0, j)),
                pl.BlockSpec((1, E, ts), lambda b, j, off: (b, 0, j)),
                pl.BlockSpec((E, 1, cap, D), lambda b, j, off: (0, b, 0, 0)),
                pl.BlockSpec((1, ts, D), lambda b, j, off: (b, j, 0)),
                full(g2), full(b2),
            ],
            out_specs=pl.BlockSpec((1, ts, D), lambda b, j, off: (b, j, 0)),
        ),
        out_shape=jax.ShapeDtypeStruct((B, S, D), F32),
        compiler_params=pltpu.CompilerParams(
            dimension_semantics=("arbitrary", "arbitrary"), vmem_limit_bytes=VMEM_LIMIT_BYTES),
        name="combine",
    )(off_flat, rank, gate, y, h1, g2, b2)


def _tile(n, target):
    t = min(n, target)
    assert n % t == 0, (n, t)
    return t


def kernel(x, positions, emb_ln_g, emb_ln_b, w_in, q_norm_g, w_qb, kv_norm_g, w_kvb, conv_w, conv_b,
           conv_ln_g, conv_ln_b, w_o, ln1_g, ln1_b, w_router, w_gate, w_up, w_down, ln2_g, ln2_b):
    B, S, D = x.shape
    T = B * S
    H = MLA_HEADS
    q_rank = q_norm_g.shape[-1]
    kv_rank = kv_norm_g.shape[-1]
    conv_ch = conv_w.shape[-1]
    qk_dim = QK_NOPE_DIM + QK_ROPE_DIM
    cap = CAPACITY_FACTOR * S // N_EXPERTS
    assert w_in.shape[0] == DEPTH == 1
    row = lambda a: a.reshape(1, -1)

    wi = w_in[0]
    c1, c2, c3 = q_rank, q_rank + kv_rank, q_rank + kv_rank + QK_ROPE_DIM
    tail = LANES - QK_NOPE_DIM - QK_ROPE_DIM
    kr_cols = jnp.pad(wi[:, c2:c3], ((0, 0), (QK_NOPE_DIM, tail)))
    w_in_r = jnp.concatenate(
        [wi[:, :c2], wi[:, c3:c3 + conv_ch], wi[:, c3 + conv_ch:], kr_cols], axis=1).astype(BF16)
    w_qb_pad = jnp.pad(w_qb[0].reshape(q_rank, H, qk_dim),
                       ((0, 0), (0, 0), (0, HEAD_PAD - qk_dim))).reshape(q_rank, H * HEAD_PAD).astype(BF16)
    wkv = w_kvb[0].reshape(kv_rank, H, QK_NOPE_DIM + V_HEAD_DIM)
    wk_pad = jnp.pad(wkv[:, :, :QK_NOPE_DIM], ((0, 0), (0, 0), (0, HEAD_PAD - QK_NOPE_DIM)))
    w_kv_r = jnp.concatenate([wk_pad.reshape(kv_rank, H * HEAD_PAD),
                              wkv[:, :, QK_NOPE_DIM:].reshape(kv_rank, H * V_HEAD_DIM)], axis=1).astype(BF16)
    half = QK_ROPE_DIM // 2
    inv_freq = (ROPE_THETA ** (-jnp.arange(half, dtype=F32) / half)).reshape(half, 1)

    x2d = x.reshape(T, D)
    tm = _tile(S, 512)
    q, k, v, hc = _in_proj(x2d, positions.reshape(1, T), inv_freq, row(emb_ln_g), row(emb_ln_b), w_in_r,
                           row(q_norm_g[0]), w_qb_pad, row(kv_norm_g[0]), w_kv_r, tm=tm)
    n_ct = conv_ch // LANES
    cv = _conv(hc.reshape(B, S, conv_ch), jnp.swapaxes(conv_w[0].reshape(CONV_WIDTH, n_ct, LANES), 0, 1),
               conv_b[0].reshape(n_ct, 1, LANES), row(conv_ln_g[0]), row(conv_ln_b[0]))
    att = _attn(q.reshape(B, S, H * HEAD_PAD), k.reshape(B, S, H * HEAD_PAD),
                v.reshape(B, S, H * V_HEAD_DIM), tq=_tile(S, 512))
    wr_hi = w_router[0].astype(BF16)
    w_r2 = jnp.concatenate([wr_hi, (w_router[0] - wr_hi.astype(F32)).astype(BF16)], axis=1)
    w_r2 = jnp.pad(w_r2, ((0, 0), (0, LANES - 2 * N_EXPERTS)))
    h1, lt2 = _out_proj(x2d, att.reshape(T, H * V_HEAD_DIM), cv.reshape(T, conv_ch), row(emb_ln_g),
                        row(emb_ln_b), w_o[0].astype(BF16), row(ln1_g[0]), row(ln1_b[0]),
                        w_r2, tm=_tile(S, 2048), seq=S, n_lt=2 * N_EXPERTS)
    h1 = h1.reshape(B, S, D)
    tb = _tile(S, ROUTE_TOKEN_BLOCK)
    rank, gate, off = _route(lt2, cap=cap, tb=tb)
    off_flat = jnp.swapaxes(off[:, :, :S // tb + 1], 1, 2).reshape(-1)
    nsb = _tile(S // tb, ROUTE_BLOCKS_PER_STEP)
    xg = _dispatch(off_flat, rank, h1, cap=cap, tb=tb, nsb=nsb)
    y = _experts(xg.reshape(N_EXPERTS, B * cap, D), w_gate[0], w_up[0], w_down[0],
                 tf=_tile(w_gate.shape[-1], 512)).reshape(N_EXPERTS, B, cap, D)
    return _combine(off_flat, rank, gate, y, h1, row(ln2_g[0]), row(ln2_b[0]), tb=tb, nsb=nsb)
```

```python
import functools
import math

import jax
import jax.numpy as jnp
from jax import lax
from jax.experimental import pallas as pl
from jax.experimental.pallas import tpu as pltpu

F32 = jnp.float32
BF16 = jnp.bfloat16

MLA_HEADS = 8
QK_NOPE_DIM = 64
QK_ROPE_DIM = 32
V_HEAD_DIM = 64
CONV_WIDTH = 31
ROPE_THETA = 10000.0
N_EXPERTS = 16
CAPACITY_FACTOR = 2
DEPTH = 1
DEEPNORM_ALPHA = (2.0 * DEPTH) ** 0.25
LN_EPS = 1e-5
RMS_EPS = 1e-6
LOG2_E = math.log2(math.e)

LANES = 128
HEAD_PAD = LANES
VMEM_LIMIT_BYTES = 56 * 1024 * 1024
TOKEN_SUB_TILE = 256

IN_PROJ_TOKEN_TILE = 512
OUT_PROJ_TOKEN_TILE = 2048
ATTN_QUERY_TILE = 512
EXPERT_FF_CHUNK = 512

TOPK_SEARCH_STEPS = 36
TOPK_SEARCH_FLOOR = 1e-30


def _layer_norm(x, g, b):
    mu = jnp.mean(x, axis=-1, keepdims=True)
    xc = x - mu
    var = jnp.mean(xc * xc, axis=-1, keepdims=True)
    return xc * lax.rsqrt(var + LN_EPS) * g + b


def _rms_norm(x, g):
    return x * lax.rsqrt(jnp.mean(x * x, axis=-1, keepdims=True) + RMS_EPS) * g


def _rope_tile(x, cos_t, sin_t, lane):
    half = QK_ROPE_DIM // 2
    fwd = pltpu.roll(x, LANES - half, 1)
    bwd = pltpu.roll(x, half, 1)
    partner = jnp.where(lane < QK_NOPE_DIM + half, fwd, bwd)
    return x * cos_t + partner * sin_t


def _in_proj_kernel(x_ref, pos_ref, invf_ref, lng_ref, lnb_ref, win_ref, qg_ref, wqb_ref,
                    kvg_ref, wkv_ref, q_ref, k_ref, v_ref, hc_ref, *, q_rank, kv_rank, conv_ch):
    tm = x_ref.shape[0]
    sub = min(tm, TOKEN_SUB_TILE)
    c1 = q_rank
    c2 = c1 + kv_rank
    c3 = c2 + conv_ch
    c4 = c3 + conv_ch
    tail = LANES - QK_NOPE_DIM - QK_ROPE_DIM
    nk = MLA_HEADS * HEAD_PAD
    scale = (QK_NOPE_DIM + QK_ROPE_DIM) ** -0.5 * LOG2_E
    ones = jnp.ones((QK_NOPE_DIM, sub), F32)
    zeros = jnp.zeros((QK_NOPE_DIM, sub), F32)
    lane = lax.broadcasted_iota(jnp.int32, (sub, LANES), 1)
    for r0 in range(0, tm, sub):
        rows = slice(r0, r0 + sub)
        h = _layer_norm(x_ref[rows, :], lng_ref[...], lnb_ref[...])
        proj = jnp.dot(h.astype(BF16), win_ref[...], preferred_element_type=F32)
        cq = proj[:, :c1]
        ckv = proj[:, c1:c2]
        a = proj[:, c2:c3]
        g = proj[:, c3:c4]
        kr = proj[:, c4:c4 + LANES]

        ang = pos_ref[:, rows].astype(F32) * invf_ref[...]
        cos = jnp.cos(ang)
        sin = jnp.sin(ang)
        cos_t = jnp.concatenate([ones, cos, cos, ones[:tail]], axis=0).T
        sin_t = jnp.concatenate([zeros, -sin, sin, zeros[:tail]], axis=0).T

        cqn = _rms_norm(cq, qg_ref[...])
        q = jnp.dot(cqn.astype(BF16), wqb_ref[...], preferred_element_type=F32) * scale
        ckvn = _rms_norm(ckv, kvg_ref[...])
        kv = jnp.dot(ckvn.astype(BF16), wkv_ref[...], preferred_element_type=F32)
        k_pe = _rope_tile(kr, cos_t, sin_t, lane)
        for hd in range(MLA_HEADS):
            sl = slice(hd * HEAD_PAD, (hd + 1) * HEAD_PAD)
            q_ref[rows, sl] = _rope_tile(q[:, sl], cos_t, sin_t, lane).astype(BF16)
            k_ref[rows, sl] = (kv[:, sl] + k_pe).astype(BF16)
        v_ref[rows, :] = kv[:, nk:].astype(BF16)
        hc_ref[rows, :] = a * (1.0 / (1.0 + jnp.exp(-g)))


def _in_proj(x2d, pos_row, invf_col, lng, lnb, w_in_r, qg, w_qb_pad, kvg, w_kv_r, *, tm):
    T, D = x2d.shape
    q_rank = w_qb_pad.shape[0]
    kv_rank = w_kv_r.shape[0]
    conv_ch = (w_in_r.shape[1] - q_rank - kv_rank - LANES) // 2
    nk = MLA_HEADS * HEAD_PAD
    nv = MLA_HEADS * V_HEAD_DIM
    full = lambda a: pl.BlockSpec(a.shape, lambda i: (0,) * a.ndim)
    return pl.pallas_call(
        functools.partial(_in_proj_kernel, q_rank=q_rank, kv_rank=kv_rank, conv_ch=conv_ch),
        grid=(T // tm,),
        in_specs=[
            pl.BlockSpec((tm, D), lambda i: (i, 0)),
            pl.BlockSpec((1, tm), lambda i: (0, i)),
            full(invf_col), full(lng), full(lnb), full(w_in_r), full(qg), full(w_qb_pad),
            full(kvg), full(w_kv_r),
        ],
        out_specs=[
            pl.BlockSpec((tm, nk), lambda i: (i, 0)),
            pl.BlockSpec((tm, nk), lambda i: (i, 0)),
            pl.BlockSpec((tm, nv), lambda i: (i, 0)),
            pl.BlockSpec((tm, conv_ch), lambda i: (i, 0)),
        ],
        out_shape=[
            jax.ShapeDtypeStruct((T, nk), BF16),
            jax.ShapeDtypeStruct((T, nk), BF16),
            jax.ShapeDtypeStruct((T, nv), BF16),
            jax.ShapeDtypeStruct((T, conv_ch), F32),
        ],
        compiler_params=pltpu.CompilerParams(
            dimension_semantics=("arbitrary",), vmem_limit_bytes=VMEM_LIMIT_BYTES),
        name="in_proj",
    )(x2d, pos_row, invf_col, lng, lnb, w_in_r, qg, w_qb_pad, kvg, w_kv_r)


CONV_PAD_ROWS = 16
CONV_ROW_CHUNK = 128
CONV_CHUNKS_PER_STEP = 4
CONV_WINDOW = CONV_ROW_CHUNK + 2 * CONV_PAD_ROWS


def _conv_kernel(hc_ref, cw_ref, cb_ref, g_ref, b_ref, o_ref, win_ref, y_ref):
    S, C = hc_ref.shape[1], hc_ref.shape[2]
    n_chunks = S // CONV_ROW_CHUNK
    for r in range(n_chunks):
        lo = r * CONV_ROW_CHUNK - CONV_PAD_ROWS
        hi = lo + CONV_WINDOW
        src_lo, src_hi = max(lo, 0), min(hi, S)
        for c in range(C // LANES):
            if lo < 0:
                win_ref[r, c, 0:-lo, :] = jnp.zeros((-lo, LANES), F32)
            if hi > S:
                win_ref[r, c, CONV_WINDOW - (hi - S):, :] = jnp.zeros((hi - S, LANES), F32)
            win_ref[r, c, src_lo - lo:src_hi - lo, :] = hc_ref[0, src_lo:src_hi, c * LANES:(c + 1) * LANES]
    first = CONV_PAD_ROWS - CONV_WIDTH // 2

    n_tiles = C // LANES
    group = min(CONV_CHUNKS_PER_STEP, n_chunks)

    def chunk_group(rg, carry):
        sums = []
        for u in range(group):
            r = rg * group + u

            def tile_conv(c, s1, r=r, u=u):
                acc = jnp.zeros((CONV_ROW_CHUNK, LANES), F32)
                for j in range(CONV_WIDTH):
                    acc = acc + win_ref[r, c, first + j:first + j + CONV_ROW_CHUNK, :] * cw_ref[c, j:j + 1, :]
                acc = acc + cb_ref[c]
                y_ref[u, c] = acc
                return s1 + acc

            sums.append(lax.fori_loop(0, n_tiles, tile_conv, jnp.zeros((CONV_ROW_CHUNK, LANES), F32)))
        for u in range(group):
            base = pl.multiple_of((rg * group + u) * CONV_ROW_CHUNK, CONV_ROW_CHUNK)
            mu = jnp.sum(sums[u], axis=-1, keepdims=True) * (1.0 / C)
            s2 = jnp.zeros((CONV_ROW_CHUNK, LANES), F32)
            for c in range(n_tiles):
                yc = y_ref[u, c] - mu
                s2 = s2 + yc * yc
            inv = lax.rsqrt(jnp.sum(s2, axis=-1, keepdims=True) * (1.0 / C) + LN_EPS)
            for c in range(n_tiles):
                lanes = slice(c * LANES, (c + 1) * LANES)
                y = (y_ref[u, c] - mu) * inv * g_ref[:, lanes] + b_ref[:, lanes]
                o_ref[0, pl.ds(base, CONV_ROW_CHUNK), lanes] = (y * (1.0 / (1.0 + jnp.exp(-y)))).astype(BF16)
        return carry

    lax.fori_loop(0, n_chunks // group, chunk_group, 0)


def _conv(hc, cw, cb, g, b):
    B, S, C = hc.shape
    full = lambda a: pl.BlockSpec(a.shape, lambda i: (0,) * a.ndim)
    return pl.pallas_call(
        _conv_kernel,
        grid=(B,),
        in_specs=[pl.BlockSpec((1, S, C), lambda i: (i, 0, 0)), full(cw), full(cb), full(g), full(b)],
        out_specs=pl.BlockSpec((1, S, C), lambda i: (i, 0, 0)),
        out_shape=jax.ShapeDtypeStruct((B, S, C), BF16),
        scratch_shapes=[pltpu.VMEM((S // CONV_ROW_CHUNK, C // LANES, CONV_WINDOW, LANES), F32),
                        pltpu.VMEM((CONV_CHUNKS_PER_STEP, C // LANES, CONV_ROW_CHUNK, LANES), F32)],
        compiler_params=pltpu.CompilerParams(
            dimension_semantics=("arbitrary",), vmem_limit_bytes=VMEM_LIMIT_BYTES),
        name="conv",
    )(hc, cw, cb, g, b)


HEADS_PER_TILE = LANES // V_HEAD_DIM
ATTN_HEADS_PER_STEP = 4


ATTN_SLOTS = 2


def _attn_kernel(q_ref, k_ref, v_ref, o_ref, s_ref, p_ref, *, tq):
    S = q_ref.shape[1]
    assert HEADS_PER_TILE == 2
    lane = lax.broadcasted_iota(jnp.int32, (tq, LANES), 1)
    key_lane = lax.broadcasted_iota(jnp.int32, (S, LANES), 1)
    item = 0
    for pr in range(v_ref.shape[2] // LANES):
        v = v_ref[0, :, pr * LANES:(pr + 1) * LANES]
        one = jnp.ones((S, LANES), BF16)
        v_ones = [jnp.where(key_lane < V_HEAD_DIM, v, one), jnp.where(key_lane < V_HEAD_DIM, one, v)]
        for qi in range(S // tq):
            rows = slice(qi * tq, (qi + 1) * tq)
            outs = []
            for hh in range(HEADS_PER_TILE):
                slot = item % ATTN_SLOTS
                item += 1
                c0 = (pr * HEADS_PER_TILE + hh) * HEAD_PAD
                s_ref[slot] = lax.dot_general(q_ref[0, rows, c0:c0 + HEAD_PAD], k_ref[0, :, c0:c0 + HEAD_PAD],
                                              (((1,), (1,)), ((), ())), preferred_element_type=F32)
                s = s_ref[slot]
                p_ref[slot] = jnp.exp2(s - jnp.max(s, axis=-1, keepdims=True)).astype(BF16)
                o = jnp.dot(p_ref[slot], v_ones[hh], preferred_element_type=F32)
                outs.append(o * pltpu.roll(1.0 / o, V_HEAD_DIM, 1))
            o_ref[0, rows, pr * LANES:(pr + 1) * LANES] = jnp.where(lane < V_HEAD_DIM, outs[0], outs[1]).astype(BF16)


def _attn(q, k, v, *, tq):
    B, S, _ = q.shape
    qw = ATTN_HEADS_PER_STEP * HEAD_PAD
    vw = ATTN_HEADS_PER_STEP * V_HEAD_DIM
    return pl.pallas_call(
        functools.partial(_attn_kernel, tq=tq),
        grid=(B, MLA_HEADS // ATTN_HEADS_PER_STEP),
        in_specs=[
            pl.BlockSpec((1, S, qw), lambda b, h: (b, 0, h)),
            pl.BlockSpec((1, S, qw), lambda b, h: (b, 0, h)),
            pl.BlockSpec((1, S, vw), lambda b, h: (b, 0, h)),
        ],
        out_specs=pl.BlockSpec((1, S, vw), lambda b, h: (b, 0, h)),
        out_shape=jax.ShapeDtypeStruct((B, S, MLA_HEADS * V_HEAD_DIM), BF16),
        scratch_shapes=[pltpu.VMEM((ATTN_SLOTS, tq, S), F32), pltpu.VMEM((ATTN_SLOTS, tq, S), BF16)],
        compiler_params=pltpu.CompilerParams(
            dimension_semantics=("arbitrary", "arbitrary"), vmem_limit_bytes=VMEM_LIMIT_BYTES),
        name="attn",
    )(q, k, v)


def _out_proj_kernel(x_ref, att_ref, cv_ref, lng_ref, lnb_ref, wo_ref, g1_ref, b1_ref, wr_ref,
                     h1_ref, lt_ref):
    na = att_ref.shape[1]
    tm = x_ref.shape[0]
    n_lt = lt_ref.shape[1]
    sub = min(tm, TOKEN_SUB_TILE)
    for r0 in range(0, tm, sub):
        rows = slice(r0, r0 + sub)
        h0 = _layer_norm(x_ref[rows, :], lng_ref[...], lnb_ref[...])
        mix = jnp.dot(att_ref[rows, :], wo_ref[:na, :], preferred_element_type=F32)
        mix = mix + jnp.dot(cv_ref[rows, :], wo_ref[na:, :], preferred_element_type=F32)
        h1 = _layer_norm(DEEPNORM_ALPHA * h0 + mix, g1_ref[...], b1_ref[...])
        h1_ref[rows, :] = h1
        lg = jnp.dot(h1.astype(BF16), wr_ref[...], preferred_element_type=F32)
        lt_ref[0, :, rows] = lg.T[:n_lt, :]


def _out_proj(x2d, att, cv, lng, lnb, w_o, g1, b1, w_r2, *, tm, seq, n_lt):
    T, D = x2d.shape
    per = seq // tm
    full = lambda a: pl.BlockSpec(a.shape, lambda i: (0,) * a.ndim)
    return pl.pallas_call(
        _out_proj_kernel,
        grid=(T // tm,),
        in_specs=[
            pl.BlockSpec((tm, D), lambda i: (i, 0)),
            pl.BlockSpec((tm, att.shape[1]), lambda i: (i, 0)),
            pl.BlockSpec((tm, cv.shape[1]), lambda i: (i, 0)),
            full(lng), full(lnb), full(w_o), full(g1), full(b1), full(w_r2),
        ],
        out_specs=[
            pl.BlockSpec((tm, D), lambda i: (i, 0)),
            pl.BlockSpec((1, n_lt, tm), lambda i: (i // per, 0, i % per)),
        ],
        out_shape=[
            jax.ShapeDtypeStruct((T, D), F32),
            jax.ShapeDtypeStruct((T // seq, n_lt, seq), F32),
        ],
        compiler_params=pltpu.CompilerParams(
            dimension_semantics=("arbitrary",), vmem_limit_bytes=VMEM_LIMIT_BYTES),
        name="out_proj",
    )(x2d, att, cv, lng, lnb, w_o, g1, b1, w_r2)


def _route_kernel(lt_ref, rank_ref, gate_ref, off_ref, *, cap, tb):
    B, E, S = rank_ref.shape
    lt = lt_ref[:, :E, :] + lt_ref[:, E:, :]
    ex = jnp.exp(lt - jnp.max(lt, axis=1, keepdims=True))
    aff = ex / jnp.sum(ex, axis=1, keepdims=True)
    v = aff.reshape(B * E, S)
    rows = B * E
    kf = float(cap)

    def step(_, carry):
        lo, hi = carry
        mid = jnp.sqrt(jnp.maximum(lo, TOPK_SEARCH_FLOOR)) * jnp.sqrt(hi)
        cnt = jnp.sum(jnp.where(v >= mid, 1.0, 0.0), axis=1, keepdims=True)
        ge = cnt >= kf
        return jnp.where(ge, mid, lo), jnp.where(ge, hi, mid)

    lo, hi = lax.fori_loop(0, TOPK_SEARCH_STEPS, step,
                           (jnp.zeros((rows, 1), F32), jnp.full((rows, 1), 2.0, F32)))
    above = v >= hi
    tie = jnp.logical_and(v >= lo, jnp.logical_not(above))
    stacked = jnp.concatenate([jnp.where(above, 1.0, 0.0), jnp.where(tie, 1.0, 0.0)], axis=0)
    upper = jnp.where(lax.broadcasted_iota(jnp.int32, (S, S), 0) <= lax.broadcasted_iota(jnp.int32, (S, S), 1),
                      1.0, 0.0).astype(BF16)
    pc = jnp.dot(stacked.astype(BF16), upper, preferred_element_type=F32)
    pa = pc[:rows]
    pt = pc[rows:]
    need = kf - jnp.sum(jnp.where(above, 1.0, 0.0), axis=1, keepdims=True)
    sel = jnp.logical_or(above, jnp.logical_and(tie, pt <= need))
    taken = pa + jnp.minimum(pt, need)
    rank_ref[...] = jnp.where(sel, taken - 1.0, -1.0).astype(jnp.int32).reshape(B, E, S)
    gate_ref[...] = jnp.where(sel, v, 0.0).reshape(B, E, S)
    pick = jnp.where(lax.broadcasted_iota(jnp.int32, (S, LANES), 0) + 1
                     == lax.broadcasted_iota(jnp.int32, (S, LANES), 1) * tb, 1.0, 0.0).astype(BF16)
    off = jnp.dot(taken.astype(BF16), pick, preferred_element_type=F32)
    off_ref[...] = off.astype(jnp.int32).reshape(B, E, LANES)


def _route(lt2, *, cap, tb):
    B, E2, S = lt2.shape
    E = E2 // 2
    assert cap <= 256 and S // tb < LANES
    spec = pl.BlockSpec((B, E, S), lambda i: (0, 0, 0))
    ospec = pl.BlockSpec((B, E, LANES), lambda i: (0, 0, 0))
    return pl.pallas_call(
        functools.partial(_route_kernel, cap=cap, tb=tb),
        grid=(1,),
        in_specs=[pl.BlockSpec((B, E2, S), lambda i: (0, 0, 0))],
        out_specs=[spec, spec, ospec],
        out_shape=[jax.ShapeDtypeStruct((B, E, S), jnp.int32), jax.ShapeDtypeStruct((B, E, S), F32),
                   jax.ShapeDtypeStruct((B, E, LANES), jnp.int32)],
        compiler_params=pltpu.CompilerParams(
            dimension_semantics=("arbitrary",), vmem_limit_bytes=VMEM_LIMIT_BYTES),
        name="route",
    )(lt2)


ROUTE_TOKEN_BLOCK = 256
ROUTE_BLOCKS_PER_STEP = 4
SLOT_WINDOW = 64
SLOT_ALIGN = 16


def _slot_windows(off_ref, base, n_experts, cap, win):
    starts = []
    n_pass = jnp.int32(1)
    for e in range(n_experts):
        off = off_ref[base + e]
        end = off_ref[base + n_experts + e]
        start = jnp.minimum((off // SLOT_ALIGN) * SLOT_ALIGN, cap - win)
        starts.append(start)
        n_pass = jnp.maximum(n_pass, (end - start + (win - 1)) // win)
    return starts, n_pass


def _dispatch_kernel(off_ref, rank_ref, h_ref, xg_ref, *, cap, win, tb):
    E = rank_ref.shape[1]
    nsb = rank_ref.shape[2] // tb
    b, j = pl.program_id(0), pl.program_id(1)
    n_bounds = pl.num_programs(1) * nsb + 1

    @pl.when(j == 0)
    def _():
        xg_ref[...] = jnp.zeros(xg_ref.shape, BF16)

    sub = lax.broadcasted_iota(jnp.int32, (win, tb), 0)

    def one_pass(sb, starts, p, first):
        cols = slice(sb * tb, (sb + 1) * tb)
        pieces, rows = [], []
        for e in range(E):
            lo = starts[e] + p * win
            ws = pl.multiple_of(jnp.minimum(lo, cap - win), SLOT_ALIGN)
            tgt = ws + sub
            hit = rank_ref[0, e:e + 1, cols] == tgt
            if not first:
                hit = jnp.logical_and(hit, tgt >= lo)
            pieces.append(jnp.where(hit, 1.0, 0.0).astype(BF16))
            rows.append(ws)
        onehot = jnp.concatenate(pieces, axis=0)
        hb = h_ref[0, cols, :].astype(BF16)
        got = jnp.dot(onehot, hb, preferred_element_type=F32).astype(BF16)
        for e in range(E):
            xg_ref[e, 0, pl.ds(rows[e], win), :] += got[e * win:(e + 1) * win, :]

    plans = []
    for sb in range(nsb):
        starts, n_pass = _slot_windows(off_ref, (b * n_bounds + j * nsb + sb) * E, E, cap, win)
        plans.append((starts, n_pass))
        one_pass(sb, starts, 0, True)
    for sb in range(nsb):
        starts, n_pass = plans[sb]

        def extra(p, carry, sb=sb, starts=starts):
            one_pass(sb, starts, p, False)
            return carry

        lax.fori_loop(1, n_pass, extra, 0)


def _dispatch(off_flat, rank, h1b, *, cap, tb, nsb):
    B, E, S = rank.shape
    D = h1b.shape[-1]
    win = min(SLOT_WINDOW, cap)
    ts = tb * nsb
    assert cap % SLOT_ALIGN == 0 and win % SLOT_ALIGN == 0 and S % ts == 0
    return pl.pallas_call(
        functools.partial(_dispatch_kernel, cap=cap, win=win, tb=tb),
        grid_spec=pltpu.PrefetchScalarGridSpec(
            num_scalar_prefetch=1,
            grid=(B, S // ts),
            in_specs=[pl.BlockSpec((1, E, ts), lambda b, j, off: (b, 0, j)),
                      pl.BlockSpec((1, ts, D), lambda b, j, off: (b, j, 0))],
            out_specs=pl.BlockSpec((E, 1, cap, D), lambda b, j, off: (0, b, 0, 0)),
        ),
        out_shape=jax.ShapeDtypeStruct((E, B, cap, D), BF16),
        compiler_params=pltpu.CompilerParams(
            dimension_semantics=("arbitrary", "arbitrary"), vmem_limit_bytes=VMEM_LIMIT_BYTES),
        name="dispatch",
    )(off_flat, rank, h1b)


EXPERT_ROW_BLOCK = 1024


def _expert_kernel(x_ref, wg_ref, wu_ref, wd_ref, y_ref, acc_ref, wgb_ref, wub_ref, wdb_ref, *, n_chunks):
    f = pl.program_id(1)
    rows = x_ref.shape[1]
    rb = min(rows, EXPERT_ROW_BLOCK)
    last = n_chunks - 1

    def chunk(first, final):
        wgb_ref[...] = wg_ref[0].astype(BF16)
        wub_ref[...] = wu_ref[0].astype(BF16)
        wdb_ref[...] = wd_ref[0].astype(BF16)
        for r0 in range(0, rows, rb):
            x = x_ref[0, r0:r0 + rb, :]
            a = jnp.dot(x, wgb_ref[...], preferred_element_type=F32)
            u = jnp.dot(x, wub_ref[...], preferred_element_type=F32)
            hmid = (a * (1.0 / (1.0 + jnp.exp(-a))) * u).astype(BF16)
            part = jnp.dot(hmid, wdb_ref[...], preferred_element_type=F32)
            if not first:
                part = acc_ref[r0:r0 + rb, :] + part
            if final:
                y_ref[0, r0:r0 + rb, :] = part.astype(BF16)
            else:
                acc_ref[r0:r0 + rb, :] = part

    if n_chunks == 1:
        chunk(True, True)
    else:
        pl.when(f == 0)(lambda: chunk(True, False))
        pl.when(jnp.logical_and(f > 0, f < last))(lambda: chunk(False, False))
        pl.when(f == last)(lambda: chunk(False, True))


def _experts(xg, w_gate, w_up, w_down, *, tf):
    E, rows, D = xg.shape
    F = w_gate.shape[-1]
    return pl.pallas_call(
        functools.partial(_expert_kernel, n_chunks=F // tf),
        grid=(E, F // tf),
        in_specs=[
            pl.BlockSpec((1, rows, D), lambda e, f: (e, 0, 0)),
            pl.BlockSpec((1, D, tf), lambda e, f: (e, 0, f)),
            pl.BlockSpec((1, D, tf), lambda e, f: (e, 0, f)),
            pl.BlockSpec((1, tf, D), lambda e, f: (e, f, 0)),
        ],
        out_specs=pl.BlockSpec((1, rows, D), lambda e, f: (e, 0, 0)),
        out_shape=jax.ShapeDtypeStruct((E, rows, D), BF16),
        scratch_shapes=[pltpu.VMEM((rows, D), F32), pltpu.VMEM((D, tf), BF16), pltpu.VMEM((D, tf), BF16),
                        pltpu.VMEM((tf, D), BF16)],
        compiler_params=pltpu.CompilerParams(
            dimension_semantics=("arbitrary", "arbitrary"), vmem_limit_bytes=VMEM_LIMIT_BYTES),
        name="experts",
    )(xg, w_gate, w_up, w_down)


def _combine_kernel(off_ref, rank_ref, gate_ref, y_ref, h_ref, g2_ref, b2_ref, o_ref, *, cap, win, tb):
    E = rank_ref.shape[1]
    nsb = rank_ref.shape[2] // tb
    b, j = pl.program_id(0), pl.program_id(1)
    n_bounds = pl.num_programs(1) * nsb + 1
    per_tile = LANES // win
    lane = lax.broadcasted_iota(jnp.int32, (1, LANES), 1)
    first_lane = lax.broadcasted_iota(jnp.int32, (LANES, E * win), 0) * win
    col = lax.broadcasted_iota(jnp.int32, (LANES, E * win), 1)
    spread = jnp.where(jnp.logical_and(col >= first_lane, col < first_lane + win), 1.0, 0.0).astype(BF16)
    pad_rows = jnp.zeros((LANES - E, tb), F32)

    def token_major(x_et):
        return jnp.concatenate([x_et, pad_rows], axis=0).T.astype(BF16)

    def one_pass(sb, starts, p, first):
        cols = slice(sb * tb, (sb + 1) * tb)
        rk = jnp.dot(token_major(rank_ref[0, :, cols].astype(F32)), spread, preferred_element_type=F32)
        gt = jnp.dot(token_major(gate_ref[0, :, cols]), spread, preferred_element_type=F32)
        tgts, los, ys = [], [], []
        for t in range(E // per_tile):
            tgt = lo_v = None
            for u in range(per_tile):
                e = t * per_tile + u
                lo = starts[e] + p * win
                ws = pl.multiple_of(jnp.minimum(lo, cap - win), SLOT_ALIGN)
                ys.append(y_ref[e, 0, pl.ds(ws, win), :])
                t_u = ws + lane - u * win
                if u == 0:
                    tgt, lo_v = t_u, jnp.full((1, LANES), lo, jnp.int32)
                else:
                    here = lane >= u * win
                    tgt = jnp.where(here, t_u, tgt)
                    lo_v = jnp.where(here, lo, lo_v)
            tgts.append(tgt)
            los.append(lo_v)
        tgt = jnp.concatenate(tgts, axis=1)
        hit = rk == tgt.astype(F32)
        if not first:
            hit = jnp.logical_and(hit, tgt >= jnp.concatenate(los, axis=1))
        gates = jnp.where(hit, gt, 0.0).astype(BF16)
        return jnp.dot(gates, jnp.concatenate(ys, axis=0), preferred_element_type=F32)

    plans = []
    for sb in range(nsb):
        rows = slice(sb * tb, (sb + 1) * tb)
        starts, n_pass = _slot_windows(off_ref, (b * n_bounds + j * nsb + sb) * E, E, cap, win)
        plans.append((starts, n_pass))
        o_ref[0, rows, :] = DEEPNORM_ALPHA * h_ref[0, rows, :] + one_pass(sb, starts, 0, True)
    for sb in range(nsb):
        starts, n_pass = plans[sb]

        def extra(p, carry, sb=sb, starts=starts):
            o_ref[0, sb * tb:(sb + 1) * tb, :] += one_pass(sb, starts, p, False)
            return carry

        lax.fori_loop(1, n_pass, extra, 0)
    o_ref[0] = _layer_norm(o_ref[0], g2_ref[...], b2_ref[...])


def _combine(off_flat, rank, gate, y, h1, g2, b2, *, tb, nsb):
    B, E, S = rank.shape
    cap, D = y.shape[2], y.shape[3]
    win = min(SLOT_WINDOW, cap)
    ts = tb * nsb
    assert cap % SLOT_ALIGN == 0 and win % SLOT_ALIGN == 0 and LANES % win == 0 and E % (LANES // win) == 0
    assert S % ts == 0
    full = lambda a: pl.BlockSpec(a.shape, lambda b, j, off: (0,) * a.ndim)
    return pl.pallas_call(
        functools.partial(_combine_kernel, cap=cap, win=win, tb=tb),
        grid_spec=pltpu.PrefetchScalarGridSpec(
            num_scalar_prefetch=1,
            grid=(B, S // ts),
            in_specs=[
                pl.BlockSpec((1, E, ts), lambda b, j, off: (b, 0, j)),
                pl.BlockSpec((1, E, ts), lambda b, j, off: (b, 0, j)),
                pl.BlockSpec((E, 1, cap, D), lambda b, j, off: (0, b, 0, 0)),
                pl.BlockSpec((1, ts, D), lambda b, j, off: (b, j, 0)),
                full(g2), full(b2),
            ],
            out_specs=pl.BlockSpec((1, ts, D), lambda b, j, off: (b, j, 0)),
        ),
        out_shape=jax.ShapeDtypeStruct((B, S, D), F32),
        compiler_params=pltpu.CompilerParams(
            dimension_semantics=("arbitrary", "arbitrary"), vmem_limit_bytes=VMEM_LIMIT_BYTES),
        name="combine",
    )(off_flat, rank, gate, y, h1, g2, b2)


def _tile(n, target):
    t = min(n, target)
    assert n % t == 0, (n, t)
    return t


def kernel(x, positions, emb_ln_g, emb_ln_b, w_in, q_norm_g, w_qb, kv_norm_g, w_kvb, conv_w, conv_b,
           conv_ln_g, conv_ln_b, w_o, ln1_g, ln1_b, w_router, w_gate, w_up, w_down, ln2_g, ln2_b):
    B, S, D = x.shape
    T = B * S
    H = MLA_HEADS
    q_rank = q_norm_g.shape[-1]
    kv_rank = kv_norm_g.shape[-1]
    conv_ch = conv_w.shape[-1]
    qk_dim = QK_NOPE_DIM + QK_ROPE_DIM
    cap = CAPACITY_FACTOR * S // N_EXPERTS
    assert w_in.shape[0] == DEPTH == 1
    row = lambda a: a.reshape(1, -1)

    wi = w_in[0]
    c1, c2, c3 = q_rank, q_rank + kv_rank, q_rank + kv_rank + QK_ROPE_DIM
    tail = LANES - QK_NOPE_DIM - QK_ROPE_DIM
    kr_cols = jnp.pad(wi[:, c2:c3], ((0, 0), (QK_NOPE_DIM, tail)))
    w_in_r = jnp.concatenate(
        [wi[:, :c2], wi[:, c3:c3 + conv_ch], wi[:, c3 + conv_ch:], kr_cols], axis=1).astype(BF16)
    w_qb_pad = jnp.pad(w_qb[0].reshape(q_rank, H, qk_dim),
                       ((0, 0), (0, 0), (0, HEAD_PAD - qk_dim))).reshape(q_rank, H * HEAD_PAD).astype(BF16)
    wkv = w_kvb[0].reshape(kv_rank, H, QK_NOPE_DIM + V_HEAD_DIM)
    wk_pad = jnp.pad(wkv[:, :, :QK_NOPE_DIM], ((0, 0), (0, 0), (0, HEAD_PAD - QK_NOPE_DIM)))
    w_kv_r = jnp.concatenate([wk_pad.reshape(kv_rank, H * HEAD_PAD),
                              wkv[:, :, QK_NOPE_DIM:].reshape(kv_rank, H * V_HEAD_DIM)], axis=1).astype(BF16)
    half = QK_ROPE_DIM // 2
    inv_freq = (ROPE_THETA ** (-jnp.arange(half, dtype=F32) / half)).reshape(half, 1)

    x2d = x.reshape(T, D)
    q, k, v, hc = _in_proj(x2d, positions.reshape(1, T), inv_freq, row(emb_ln_g), row(emb_ln_b), w_in_r,
                           row(q_norm_g[0]), w_qb_pad, row(kv_norm_g[0]), w_kv_r, tm=_tile(S, IN_PROJ_TOKEN_TILE))
    n_ct = conv_ch // LANES
    cv = _conv(hc.reshape(B, S, conv_ch), jnp.swapaxes(conv_w[0].reshape(CONV_WIDTH, n_ct, LANES), 0, 1),
               conv_b[0].reshape(n_ct, 1, LANES), row(conv_ln_g[0]), row(conv_ln_b[0]))
    att = _attn(q.reshape(B, S, H * HEAD_PAD), k.reshape(B, S, H * HEAD_PAD),
                v.reshape(B, S, H * V_HEAD_DIM), tq=_tile(S, ATTN_QUERY_TILE))
    wr_hi = w_router[0].astype(BF16)
    w_r2 = jnp.concatenate([wr_hi, (w_router[0] - wr_hi.astype(F32)).astype(BF16)], axis=1)
    w_r2 = jnp.pad(w_r2, ((0, 0), (0, LANES - 2 * N_EXPERTS)))
    h1, lt2 = _out_proj(x2d, att.reshape(T, H * V_HEAD_DIM), cv.reshape(T, conv_ch), row(emb_ln_g),
                        row(emb_ln_b), w_o[0].astype(BF16), row(ln1_g[0]), row(ln1_b[0]),
                        w_r2, tm=_tile(S, OUT_PROJ_TOKEN_TILE), seq=S, n_lt=2 * N_EXPERTS)
    h1 = h1.reshape(B, S, D)
    tb = _tile(S, ROUTE_TOKEN_BLOCK)
    rank, gate, off = _route(lt2, cap=cap, tb=tb)
    off_flat = jnp.swapaxes(off[:, :, :S // tb + 1], 1, 2).reshape(-1)
    nsb = _tile(S // tb, ROUTE_BLOCKS_PER_STEP)
    xg = _dispatch(off_flat, rank, h1, cap=cap, tb=tb, nsb=nsb)
    y = _experts(xg.reshape(N_EXPERTS, B * cap, D), w_gate[0], w_up[0], w_down[0],
                 tf=_tile(w_gate.shape[-1], EXPERT_FF_CHUNK)).reshape(N_EXPERTS, B, cap, D)
    return _combine(off_flat, rank, gate, y, h1, row(ln2_g[0]), row(ln2_b[0]), tb=tb, nsb=nsb)
```

```python
import functools
import math

import jax
import jax.numpy as jnp
from jax import lax
from jax.experimental import pallas as pl
from jax.experimental.pallas import tpu as pltpu

F32 = jnp.float32
BF16 = jnp.bfloat16

MLA_HEADS = 8
QK_NOPE_DIM = 64
QK_ROPE_DIM = 32
V_HEAD_DIM = 64
CONV_WIDTH = 31
ROPE_THETA = 10000.0
N_EXPERTS = 16
CAPACITY_FACTOR = 2
DEPTH = 1
DEEPNORM_ALPHA = (2.0 * DEPTH) ** 0.25
LN_EPS = 1e-5
RMS_EPS = 1e-6
LOG2_E = math.log2(math.e)

LANES = 128
HEAD_PAD = LANES
VMEM_LIMIT_BYTES = 56 * 1024 * 1024
TOKEN_SUB_TILE = 256

IN_PROJ_TOKEN_TILE = 512
OUT_PROJ_TOKEN_TILE = 2048
ATTN_QUERY_TILE = 512
EXPERT_FF_CHUNK = 1024
EXPERT_VMEM_LIMIT_BYTES = 63 * 1024 * 1024

TOPK_SEARCH_STEPS = 36
TOPK_SEARCH_FLOOR = 1e-30


def _layer_norm(x, g, b):
    mu = jnp.mean(x, axis=-1, keepdims=True)
    xc = x - mu
    var = jnp.mean(xc * xc, axis=-1, keepdims=True)
    return xc * lax.rsqrt(var + LN_EPS) * g + b


def _rms_norm(x, g):
    return x * lax.rsqrt(jnp.mean(x * x, axis=-1, keepdims=True) + RMS_EPS) * g


def _rope_tile(x, cos_t, sin_t, lane):
    half = QK_ROPE_DIM // 2
    fwd = pltpu.roll(x, LANES - half, 1)
    bwd = pltpu.roll(x, half, 1)
    partner = jnp.where(lane < QK_NOPE_DIM + half, fwd, bwd)
    return x * cos_t + partner * sin_t


def _in_proj_kernel(x_ref, pos_ref, invf_ref, lng_ref, lnb_ref, win_ref, qg_ref, wqb_ref,
                    kvg_ref, wkv_ref, q_ref, k_ref, v_ref, hc_ref, *, q_rank, kv_rank, conv_ch):
    tm = x_ref.shape[0]
    sub = min(tm, TOKEN_SUB_TILE)
    c1 = q_rank
    c2 = c1 + kv_rank
    c3 = c2 + conv_ch
    c4 = c3 + conv_ch
    tail = LANES - QK_NOPE_DIM - QK_ROPE_DIM
    nk = MLA_HEADS * HEAD_PAD
    scale = (QK_NOPE_DIM + QK_ROPE_DIM) ** -0.5 * LOG2_E
    ones = jnp.ones((QK_NOPE_DIM, sub), F32)
    zeros = jnp.zeros((QK_NOPE_DIM, sub), F32)
    lane = lax.broadcasted_iota(jnp.int32, (sub, LANES), 1)
    for r0 in range(0, tm, sub):
        rows = slice(r0, r0 + sub)
        h = _layer_norm(x_ref[rows, :], lng_ref[...], lnb_ref[...])
        proj = jnp.dot(h.astype(BF16), win_ref[...], preferred_element_type=F32)
        cq = proj[:, :c1]
        ckv = proj[:, c1:c2]
        a = proj[:, c2:c3]
        g = proj[:, c3:c4]
        kr = proj[:, c4:c4 + LANES]

        ang = pos_ref[:, rows].astype(F32) * invf_ref[...]
        cos = jnp.cos(ang)
        sin = jnp.sin(ang)
        cos_t = jnp.concatenate([ones, cos, cos, ones[:tail]], axis=0).T
        sin_t = jnp.concatenate([zeros, -sin, sin, zeros[:tail]], axis=0).T

        cqn = _rms_norm(cq, qg_ref[...])
        q = jnp.dot(cqn.astype(BF16), wqb_ref[...], preferred_element_type=F32) * scale
        ckvn = _rms_norm(ckv, kvg_ref[...])
        kv = jnp.dot(ckvn.astype(BF16), wkv_ref[...], preferred_element_type=F32)
        k_pe = _rope_tile(kr, cos_t, sin_t, lane)
        for hd in range(MLA_HEADS):
            sl = slice(hd * HEAD_PAD, (hd + 1) * HEAD_PAD)
            q_ref[rows, sl] = _rope_tile(q[:, sl], cos_t, sin_t, lane).astype(BF16)
            k_ref[rows, sl] = (kv[:, sl] + k_pe).astype(BF16)
        v_ref[rows, :] = kv[:, nk:].astype(BF16)
        hc_ref[rows, :] = a * (1.0 / (1.0 + jnp.exp(-g)))


def _in_proj(x2d, pos_row, invf_col, lng, lnb, w_in_r, qg, w_qb_pad, kvg, w_kv_r, *, tm):
    T, D = x2d.shape
    q_rank = w_qb_pad.shape[0]
    kv_rank = w_kv_r.shape[0]
    conv_ch = (w_in_r.shape[1] - q_rank - kv_rank - LANES) // 2
    nk = MLA_HEADS * HEAD_PAD
    nv = MLA_HEADS * V_HEAD_DIM
    full = lambda a: pl.BlockSpec(a.shape, lambda i: (0,) * a.ndim)
    return pl.pallas_call(
        functools.partial(_in_proj_kernel, q_rank=q_rank, kv_rank=kv_rank, conv_ch=conv_ch),
        grid=(T // tm,),
        in_specs=[
            pl.BlockSpec((tm, D), lambda i: (i, 0)),
            pl.BlockSpec((1, tm), lambda i: (0, i)),
            full(invf_col), full(lng), full(lnb), full(w_in_r), full(qg), full(w_qb_pad),
            full(kvg), full(w_kv_r),
        ],
        out_specs=[
            pl.BlockSpec((tm, nk), lambda i: (i, 0)),
            pl.BlockSpec((tm, nk), lambda i: (i, 0)),
            pl.BlockSpec((tm, nv), lambda i: (i, 0)),
            pl.BlockSpec((tm, conv_ch), lambda i: (i, 0)),
        ],
        out_shape=[
            jax.ShapeDtypeStruct((T, nk), BF16),
            jax.ShapeDtypeStruct((T, nk), BF16),
            jax.ShapeDtypeStruct((T, nv), BF16),
            jax.ShapeDtypeStruct((T, conv_ch), F32),
        ],
        compiler_params=pltpu.CompilerParams(
            dimension_semantics=("arbitrary",), vmem_limit_bytes=VMEM_LIMIT_BYTES),
        name="in_proj",
    )(x2d, pos_row, invf_col, lng, lnb, w_in_r, qg, w_qb_pad, kvg, w_kv_r)


CONV_PAD_ROWS = 16
CONV_ROW_CHUNK = 128
CONV_CHUNKS_PER_STEP = 4
CONV_WINDOW = CONV_ROW_CHUNK + 2 * CONV_PAD_ROWS


def _conv_kernel(hc_ref, cw_ref, cb_ref, g_ref, b_ref, o_ref, win_ref, y_ref):
    S, C = hc_ref.shape[1], hc_ref.shape[2]
    n_chunks = S // CONV_ROW_CHUNK
    for r in range(n_chunks):
        lo = r * CONV_ROW_CHUNK - CONV_PAD_ROWS
        hi = lo + CONV_WINDOW
        src_lo, src_hi = max(lo, 0), min(hi, S)
        for c in range(C // LANES):
            if lo < 0:
                win_ref[r, c, 0:-lo, :] = jnp.zeros((-lo, LANES), F32)
            if hi > S:
                win_ref[r, c, CONV_WINDOW - (hi - S):, :] = jnp.zeros((hi - S, LANES), F32)
            win_ref[r, c, src_lo - lo:src_hi - lo, :] = hc_ref[0, src_lo:src_hi, c * LANES:(c + 1) * LANES]
    first = CONV_PAD_ROWS - CONV_WIDTH // 2

    n_tiles = C // LANES
    group = min(CONV_CHUNKS_PER_STEP, n_chunks)

    def chunk_group(rg, carry):
        sums = []
        for u in range(group):
            r = rg * group + u

            def tile_conv(c, s1, r=r, u=u):
                acc = jnp.zeros((CONV_ROW_CHUNK, LANES), F32)
                for j in range(CONV_WIDTH):
                    acc = acc + win_ref[r, c, first + j:first + j + CONV_ROW_CHUNK, :] * cw_ref[c, j:j + 1, :]
                acc = acc + cb_ref[c]
                y_ref[u, c] = acc
                return s1 + acc

            sums.append(lax.fori_loop(0, n_tiles, tile_conv, jnp.zeros((CONV_ROW_CHUNK, LANES), F32)))
        for u in range(group):
            base = pl.multiple_of((rg * group + u) * CONV_ROW_CHUNK, CONV_ROW_CHUNK)
            mu = jnp.sum(sums[u], axis=-1, keepdims=True) * (1.0 / C)
            s2 = jnp.zeros((CONV_ROW_CHUNK, LANES), F32)
            for c in range(n_tiles):
                yc = y_ref[u, c] - mu
                s2 = s2 + yc * yc
            inv = lax.rsqrt(jnp.sum(s2, axis=-1, keepdims=True) * (1.0 / C) + LN_EPS)
            for c in range(n_tiles):
                lanes = slice(c * LANES, (c + 1) * LANES)
                y = (y_ref[u, c] - mu) * inv * g_ref[:, lanes] + b_ref[:, lanes]
                o_ref[0, pl.ds(base, CONV_ROW_CHUNK), lanes] = (y * (1.0 / (1.0 + jnp.exp(-y)))).astype(BF16)
        return carry

    lax.fori_loop(0, n_chunks // group, chunk_group, 0)


def _conv(hc, cw, cb, g, b):
    B, S, C = hc.shape
    full = lambda a: pl.BlockSpec(a.shape, lambda i: (0,) * a.ndim)
    return pl.pallas_call(
        _conv_kernel,
        grid=(B,),
        in_specs=[pl.BlockSpec((1, S, C), lambda i: (i, 0, 0)), full(cw), full(cb), full(g), full(b)],
        out_specs=pl.BlockSpec((1, S, C), lambda i: (i, 0, 0)),
        out_shape=jax.ShapeDtypeStruct((B, S, C), BF16),
        scratch_shapes=[pltpu.VMEM((S // CONV_ROW_CHUNK, C // LANES, CONV_WINDOW, LANES), F32),
                        pltpu.VMEM((CONV_CHUNKS_PER_STEP, C // LANES, CONV_ROW_CHUNK, LANES), F32)],
        compiler_params=pltpu.CompilerParams(
            dimension_semantics=("arbitrary",), vmem_limit_bytes=VMEM_LIMIT_BYTES),
        name="conv",
    )(hc, cw, cb, g, b)


HEADS_PER_TILE = LANES // V_HEAD_DIM
ATTN_HEADS_PER_STEP = 4


ATTN_SLOTS = 2


def _attn_kernel(q_ref, k_ref, v_ref, o_ref, s_ref, p_ref, *, tq):
    S = q_ref.shape[1]
    assert HEADS_PER_TILE == 2
    lane = lax.broadcasted_iota(jnp.int32, (tq, LANES), 1)
    key_lane = lax.broadcasted_iota(jnp.int32, (S, LANES), 1)
    item = 0
    for pr in range(v_ref.shape[2] // LANES):
        v = v_ref[0, :, pr * LANES:(pr + 1) * LANES]
        one = jnp.ones((S, LANES), BF16)
        v_ones = [jnp.where(key_lane < V_HEAD_DIM, v, one), jnp.where(key_lane < V_HEAD_DIM, one, v)]
        for qi in range(S // tq):
            rows = slice(qi * tq, (qi + 1) * tq)
            outs = []
            for hh in range(HEADS_PER_TILE):
                slot = item % ATTN_SLOTS
                item += 1
                c0 = (pr * HEADS_PER_TILE + hh) * HEAD_PAD
                s_ref[slot] = lax.dot_general(q_ref[0, rows, c0:c0 + HEAD_PAD], k_ref[0, :, c0:c0 + HEAD_PAD],
                                              (((1,), (1,)), ((), ())), preferred_element_type=F32)
                s = s_ref[slot]
                p_ref[slot] = jnp.exp2(s - jnp.max(s, axis=-1, keepdims=True)).astype(BF16)
                o = jnp.dot(p_ref[slot], v_ones[hh], preferred_element_type=F32)
                outs.append(o * pltpu.roll(1.0 / o, V_HEAD_DIM, 1))
            o_ref[0, rows, pr * LANES:(pr + 1) * LANES] = jnp.where(lane < V_HEAD_DIM, outs[0], outs[1]).astype(BF16)


def _attn(q, k, v, *, tq):
    B, S, _ = q.shape
    qw = ATTN_HEADS_PER_STEP * HEAD_PAD
    vw = ATTN_HEADS_PER_STEP * V_HEAD_DIM
    return pl.pallas_call(
        functools.partial(_attn_kernel, tq=tq),
        grid=(B, MLA_HEADS // ATTN_HEADS_PER_STEP),
        in_specs=[
            pl.BlockSpec((1, S, qw), lambda b, h: (b, 0, h)),
            pl.BlockSpec((1, S, qw), lambda b, h: (b, 0, h)),
            pl.BlockSpec((1, S, vw), lambda b, h: (b, 0, h)),
        ],
        out_specs=pl.BlockSpec((1, S, vw), lambda b, h: (b, 0, h)),
        out_shape=jax.ShapeDtypeStruct((B, S, MLA_HEADS * V_HEAD_DIM), BF16),
        scratch_shapes=[pltpu.VMEM((ATTN_SLOTS, tq, S), F32), pltpu.VMEM((ATTN_SLOTS, tq, S), BF16)],
        compiler_params=pltpu.CompilerParams(
            dimension_semantics=("arbitrary", "arbitrary"), vmem_limit_bytes=VMEM_LIMIT_BYTES),
        name="attn",
    )(q, k, v)


def _out_proj_kernel(x_ref, att_ref, cv_ref, lng_ref, lnb_ref, wo_ref, g1_ref, b1_ref, wr_ref,
                     h1_ref, lt_ref):
    na = att_ref.shape[1]
    tm = x_ref.shape[0]
    n_lt = lt_ref.shape[1]
    sub = min(tm, TOKEN_SUB_TILE)
    for r0 in range(0, tm, sub):
        rows = slice(r0, r0 + sub)
        h0 = _layer_norm(x_ref[rows, :], lng_ref[...], lnb_ref[...])
        mix = jnp.dot(att_ref[rows, :], wo_ref[:na, :], preferred_element_type=F32)
        mix = mix + jnp.dot(cv_ref[rows, :], wo_ref[na:, :], preferred_element_type=F32)
        h1 = _layer_norm(DEEPNORM_ALPHA * h0 + mix, g1_ref[...], b1_ref[...])
        h1_ref[rows, :] = h1
        lg = jnp.dot(h1.astype(BF16), wr_ref[...], preferred_element_type=F32)
        lt_ref[0, :, rows] = lg.T[:n_lt, :]


def _out_proj(x2d, att, cv, lng, lnb, w_o, g1, b1, w_r2, *, tm, seq, n_lt):
    T, D = x2d.shape
    per = seq // tm
    full = lambda a: pl.BlockSpec(a.shape, lambda i: (0,) * a.ndim)
    return pl.pallas_call(
        _out_proj_kernel,
        grid=(T // tm,),
        in_specs=[
            pl.BlockSpec((tm, D), lambda i: (i, 0)),
            pl.BlockSpec((tm, att.shape[1]), lambda i: (i, 0)),
            pl.BlockSpec((tm, cv.shape[1]), lambda i: (i, 0)),
            full(lng), full(lnb), full(w_o), full(g1), full(b1), full(w_r2),
        ],
        out_specs=[
            pl.BlockSpec((tm, D), lambda i: (i, 0)),
            pl.BlockSpec((1, n_lt, tm), lambda i: (i // per, 0, i % per)),
        ],
        out_shape=[
            jax.ShapeDtypeStruct((T, D), F32),
            jax.ShapeDtypeStruct((T // seq, n_lt, seq), F32),
        ],
        compiler_params=pltpu.CompilerParams(
            dimension_semantics=("arbitrary",), vmem_limit_bytes=VMEM_LIMIT_BYTES),
        name="out_proj",
    )(x2d, att, cv, lng, lnb, w_o, g1, b1, w_r2)


def _route_kernel(lt_ref, rank_ref, gate_ref, off_ref, *, cap, tb):
    B, E, S = rank_ref.shape
    lt = lt_ref[:, :E, :] + lt_ref[:, E:, :]
    ex = jnp.exp(lt - jnp.max(lt, axis=1, keepdims=True))
    aff = ex / jnp.sum(ex, axis=1, keepdims=True)
    v = aff.reshape(B * E, S)
    rows = B * E
    kf = float(cap)

    def step(_, carry):
        lo, hi = carry
        mid = jnp.sqrt(jnp.maximum(lo, TOPK_SEARCH_FLOOR)) * jnp.sqrt(hi)
        cnt = jnp.sum(jnp.where(v >= mid, 1.0, 0.0), axis=1, keepdims=True)
        ge = cnt >= kf
        return jnp.where(ge, mid, lo), jnp.where(ge, hi, mid)

    lo, hi = lax.fori_loop(0, TOPK_SEARCH_STEPS, step,
                           (jnp.zeros((rows, 1), F32), jnp.full((rows, 1), 2.0, F32)))
    above = v >= hi
    tie = jnp.logical_and(v >= lo, jnp.logical_not(above))
    stacked = jnp.concatenate([jnp.where(above, 1.0, 0.0), jnp.where(tie, 1.0, 0.0)], axis=0)
    upper = jnp.where(lax.broadcasted_iota(jnp.int32, (S, S), 0) <= lax.broadcasted_iota(jnp.int32, (S, S), 1),
                      1.0, 0.0).astype(BF16)
    pc = jnp.dot(stacked.astype(BF16), upper, preferred_element_type=F32)
    pa = pc[:rows]
    pt = pc[rows:]
    need = kf - jnp.sum(jnp.where(above, 1.0, 0.0), axis=1, keepdims=True)
    sel = jnp.logical_or(above, jnp.logical_and(tie, pt <= need))
    taken = pa + jnp.minimum(pt, need)
    rank_ref[...] = jnp.where(sel, taken - 1.0, -1.0).astype(jnp.int32).reshape(B, E, S)
    gate_ref[...] = jnp.where(sel, v, 0.0).reshape(B, E, S)
    pick = jnp.where(lax.broadcasted_iota(jnp.int32, (S, LANES), 0) + 1
                     == lax.broadcasted_iota(jnp.int32, (S, LANES), 1) * tb, 1.0, 0.0).astype(BF16)
    off = jnp.dot(taken.astype(BF16), pick, preferred_element_type=F32)
    off_ref[...] = off.astype(jnp.int32).reshape(B, E, LANES)


def _route(lt2, *, cap, tb):
    B, E2, S = lt2.shape
    E = E2 // 2
    assert cap <= 256 and S // tb < LANES
    spec = pl.BlockSpec((B, E, S), lambda i: (0, 0, 0))
    ospec = pl.BlockSpec((B, E, LANES), lambda i: (0, 0, 0))
    return pl.pallas_call(
        functools.partial(_route_kernel, cap=cap, tb=tb),
        grid=(1,),
        in_specs=[pl.BlockSpec((B, E2, S), lambda i: (0, 0, 0))],
        out_specs=[spec, spec, ospec],
        out_shape=[jax.ShapeDtypeStruct((B, E, S), jnp.int32), jax.ShapeDtypeStruct((B, E, S), F32),
                   jax.ShapeDtypeStruct((B, E, LANES), jnp.int32)],
        compiler_params=pltpu.CompilerParams(
            dimension_semantics=("arbitrary",), vmem_limit_bytes=VMEM_LIMIT_BYTES),
        name="route",
    )(lt2)


ROUTE_TOKEN_BLOCK = 256
ROUTE_BLOCKS_PER_STEP = 4
SLOT_WINDOW = 64
SLOT_ALIGN = 16


def _slot_windows(off_ref, base, n_experts, cap, win):
    starts = []
    n_pass = jnp.int32(1)
    for e in range(n_experts):
        off = off_ref[base + e]
        end = off_ref[base + n_experts + e]
        start = jnp.minimum((off // SLOT_ALIGN) * SLOT_ALIGN, cap - win)
        starts.append(start)
        n_pass = jnp.maximum(n_pass, (end - start + (win - 1)) // win)
    return starts, n_pass


def _dispatch_kernel(off_ref, rank_ref, h_ref, xg_ref, *, cap, win, tb):
    E = rank_ref.shape[1]
    nsb = rank_ref.shape[2] // tb
    b, j = pl.program_id(0), pl.program_id(1)
    n_bounds = pl.num_programs(1) * nsb + 1

    @pl.when(j == 0)
    def _():
        xg_ref[...] = jnp.zeros(xg_ref.shape, BF16)

    sub = lax.broadcasted_iota(jnp.int32, (win, tb), 0)

    def one_pass(sb, starts, p, first):
        cols = slice(sb * tb, (sb + 1) * tb)
        pieces, rows = [], []
        for e in range(E):
            lo = starts[e] + p * win
            ws = pl.multiple_of(jnp.minimum(lo, cap - win), SLOT_ALIGN)
            tgt = ws + sub
            hit = rank_ref[0, e:e + 1, cols] == tgt
            if not first:
                hit = jnp.logical_and(hit, tgt >= lo)
            pieces.append(jnp.where(hit, 1.0, 0.0).astype(BF16))
            rows.append(ws)
        onehot = jnp.concatenate(pieces, axis=0)
        hb = h_ref[0, cols, :].astype(BF16)
        got = jnp.dot(onehot, hb, preferred_element_type=F32).astype(BF16)
        for e in range(E):
            xg_ref[e, 0, pl.ds(rows[e], win), :] += got[e * win:(e + 1) * win, :]

    plans = []
    for sb in range(nsb):
        starts, n_pass = _slot_windows(off_ref, (b * n_bounds + j * nsb + sb) * E, E, cap, win)
        plans.append((starts, n_pass))
        one_pass(sb, starts, 0, True)
    for sb in range(nsb):
        starts, n_pass = plans[sb]

        def extra(p, carry, sb=sb, starts=starts):
            one_pass(sb, starts, p, False)
            return carry

        lax.fori_loop(1, n_pass, extra, 0)


def _dispatch(off_flat, rank, h1b, *, cap, tb, nsb):
    B, E, S = rank.shape
    D = h1b.shape[-1]
    win = min(SLOT_WINDOW, cap)
    ts = tb * nsb
    assert cap % SLOT_ALIGN == 0 and win % SLOT_ALIGN == 0 and S % ts == 0
    return pl.pallas_call(
        functools.partial(_dispatch_kernel, cap=cap, win=win, tb=tb),
        grid_spec=pltpu.PrefetchScalarGridSpec(
            num_scalar_prefetch=1,
            grid=(B, S // ts),
            in_specs=[pl.BlockSpec((1, E, ts), lambda b, j, off: (b, 0, j)),
                      pl.BlockSpec((1, ts, D), lambda b, j, off: (b, j, 0))],
            out_specs=pl.BlockSpec((E, 1, cap, D), lambda b, j, off: (0, b, 0, 0)),
        ),
        out_shape=jax.ShapeDtypeStruct((E, B, cap, D), BF16),
        compiler_params=pltpu.CompilerParams(
            dimension_semantics=("arbitrary", "arbitrary"), vmem_limit_bytes=VMEM_LIMIT_BYTES),
        name="dispatch",
    )(off_flat, rank, h1b)


EXPERT_ROW_BLOCK = 1024


def _expert_kernel(x_ref, wg_ref, wu_ref, wd_ref, y_ref, acc_ref, wgb_ref, wub_ref, wdb_ref, *, n_chunks):
    f = pl.program_id(1)
    rows = x_ref.shape[1]
    rb = min(rows, EXPERT_ROW_BLOCK)
    last = n_chunks - 1

    def chunk(first, final):
        wgb_ref[...] = wg_ref[0].astype(BF16)
        wub_ref[...] = wu_ref[0].astype(BF16)
        wdb_ref[...] = wd_ref[0].astype(BF16)
        for r0 in range(0, rows, rb):
            x = x_ref[0, r0:r0 + rb, :]
            a = jnp.dot(x, wgb_ref[...], preferred_element_type=F32)
            u = jnp.dot(x, wub_ref[...], preferred_element_type=F32)
            hmid = (a * (1.0 / (1.0 + jnp.exp(-a))) * u).astype(BF16)
            part = jnp.dot(hmid, wdb_ref[...], preferred_element_type=F32)
            if not first:
                part = acc_ref[r0:r0 + rb, :] + part
            if final:
                y_ref[0, r0:r0 + rb, :] = part.astype(BF16)
            else:
                acc_ref[r0:r0 + rb, :] = part

    if n_chunks == 1:
        chunk(True, True)
    else:
        pl.when(f == 0)(lambda: chunk(True, False))
        if n_chunks > 2:
            pl.when(jnp.logical_and(f > 0, f < last))(lambda: chunk(False, False))
        pl.when(f == last)(lambda: chunk(False, True))


def _experts(xg, w_gate, w_up, w_down, *, tf):
    E, rows, D = xg.shape
    F = w_gate.shape[-1]
    return pl.pallas_call(
        functools.partial(_expert_kernel, n_chunks=F // tf),
        grid=(E, F // tf),
        in_specs=[
            pl.BlockSpec((1, rows, D), lambda e, f: (e, 0, 0)),
            pl.BlockSpec((1, D, tf), lambda e, f: (e, 0, f)),
            pl.BlockSpec((1, D, tf), lambda e, f: (e, 0, f)),
            pl.BlockSpec((1, tf, D), lambda e, f: (e, f, 0)),
        ],
        out_specs=pl.BlockSpec((1, rows, D), lambda e, f: (e, 0, 0)),
        out_shape=jax.ShapeDtypeStruct((E, rows, D), BF16),
        scratch_shapes=[pltpu.VMEM((rows, D), F32), pltpu.VMEM((D, tf), BF16), pltpu.VMEM((D, tf), BF16),
                        pltpu.VMEM((tf, D), BF16)],
        compiler_params=pltpu.CompilerParams(
            dimension_semantics=("arbitrary", "arbitrary"), vmem_limit_bytes=EXPERT_VMEM_LIMIT_BYTES),
        name="experts",
    )(xg, w_gate, w_up, w_down)


def _combine_kernel(off_ref, rank_ref, gate_ref, y_ref, h_ref, g2_ref, b2_ref, o_ref, *, cap, win, tb):
    E = rank_ref.shape[1]
    nsb = rank_ref.shape[2] // tb
    b, j = pl.program_id(0), pl.program_id(1)
    n_bounds = pl.num_programs(1) * nsb + 1
    per_tile = LANES // win
    lane = lax.broadcasted_iota(jnp.int32, (1, LANES), 1)
    first_lane = lax.broadcasted_iota(jnp.int32, (LANES, E * win), 0) * win
    col = lax.broadcasted_iota(jnp.int32, (LANES, E * win), 1)
    spread = jnp.where(jnp.logical_and(col >= first_lane, col < first_lane + win), 1.0, 0.0).astype(BF16)
    pad_rows = jnp.zeros((LANES - E, tb), F32)

    def token_major(x_et):
        return jnp.concatenate([x_et, pad_rows], axis=0).T.astype(BF16)

    def one_pass(sb, starts, p, first):
        cols = slice(sb * tb, (sb + 1) * tb)
        rk = jnp.dot(token_major(rank_ref[0, :, cols].astype(F32)), spread, preferred_element_type=F32)
        gt = jnp.dot(token_major(gate_ref[0, :, cols]), spread, preferred_element_type=F32)
        tgts, los, ys = [], [], []
        for t in range(E // per_tile):
            tgt = lo_v = None
            for u in range(per_tile):
                e = t * per_tile + u
                lo = starts[e] + p * win
                ws = pl.multiple_of(jnp.minimum(lo, cap - win), SLOT_ALIGN)
                ys.append(y_ref[e, 0, pl.ds(ws, win), :])
                t_u = ws + lane - u * win
                if u == 0:
                    tgt, lo_v = t_u, jnp.full((1, LANES), lo, jnp.int32)
                else:
                    here = lane >= u * win
                    tgt = jnp.where(here, t_u, tgt)
                    lo_v = jnp.where(here, lo, lo_v)
            tgts.append(tgt)
            los.append(lo_v)
        tgt = jnp.concatenate(tgts, axis=1)
        hit = rk == tgt.astype(F32)
        if not first:
            hit = jnp.logical_and(hit, tgt >= jnp.concatenate(los, axis=1))
        gates = jnp.where(hit, gt, 0.0).astype(BF16)
        return jnp.dot(gates, jnp.concatenate(ys, axis=0), preferred_element_type=F32)

    plans = []
    for sb in range(nsb):
        rows = slice(sb * tb, (sb + 1) * tb)
        starts, n_pass = _slot_windows(off_ref, (b * n_bounds + j * nsb + sb) * E, E, cap, win)
        plans.append((starts, n_pass))
        o_ref[0, rows, :] = DEEPNORM_ALPHA * h_ref[0, rows, :] + one_pass(sb, starts, 0, True)
    for sb in range(nsb):
        starts, n_pass = plans[sb]

        def extra(p, carry, sb=sb, starts=starts):
            o_ref[0, sb * tb:(sb + 1) * tb, :] += one_pass(sb, starts, p, False)
            return carry

        lax.fori_loop(1, n_pass, extra, 0)
    o_ref[0] = _layer_norm(o_ref[0], g2_ref[...], b2_ref[...])


def _combine(off_flat, rank, gate, y, h1, g2, b2, *, tb, nsb):
    B, E, S = rank.shape
    cap, D = y.shape[2], y.shape[3]
    win = min(SLOT_WINDOW, cap)
    ts = tb * nsb
    assert cap % SLOT_ALIGN == 0 and win % SLOT_ALIGN == 0 and LANES % win == 0 and E % (LANES // win) == 0
    assert S % ts == 0
    full = lambda a: pl.BlockSpec(a.shape, lambda b, j, off: (0,) * a.ndim)
    return pl.pallas_call(
        functools.partial(_combine_kernel, cap=cap, win=win, tb=tb),
        grid_spec=pltpu.PrefetchScalarGridSpec(
            num_scalar_prefetch=1,
            grid=(B, S // ts),
            in_specs=[
                pl.BlockSpec((1, E, ts), lambda b, j, off: (b, 0, j)),
                pl.BlockSpec((1, E, ts), lambda b, j, off: (b, 0, j)),
                pl.BlockSpec((E, 1, cap, D), lambda b, j, off: (0, b, 0, 0)),
                pl.BlockSpec((1, ts, D), lambda b, j, off: (b, j, 0)),
                full(g2), full(b2),
            ],
            out_specs=pl.BlockSpec((1, ts, D), lambda b, j, off: (b, j, 0)),
        ),
        out_shape=jax.ShapeDtypeStruct((B, S, D), F32),
        compiler_params=pltpu.CompilerParams(
            dimension_semantics=("arbitrary", "arbitrary"), vmem_limit_bytes=VMEM_LIMIT_BYTES),
        name="combine",
    )(off_flat, rank, gate, y, h1, g2, b2)


def _tile(n, target):
    t = min(n, target)
    assert n % t == 0, (n, t)
    return t


def kernel(x, positions, emb_ln_g, emb_ln_b, w_in, q_norm_g, w_qb, kv_norm_g, w_kvb, conv_w, conv_b,
           conv_ln_g, conv_ln_b, w_o, ln1_g, ln1_b, w_router, w_gate, w_up, w_down, ln2_g, ln2_b):
    B, S, D = x.shape
    T = B * S
    H = MLA_HEADS
    q_rank = q_norm_g.shape[-1]
    kv_rank = kv_norm_g.shape[-1]
    conv_ch = conv_w.shape[-1]
    qk_dim = QK_NOPE_DIM + QK_ROPE_DIM
    cap = CAPACITY_FACTOR * S // N_EXPERTS
    assert w_in.shape[0] == DEPTH == 1
    row = lambda a: a.reshape(1, -1)

    wi = w_in[0]
    c1, c2, c3 = q_rank, q_rank + kv_rank, q_rank + kv_rank + QK_ROPE_DIM
    tail = LANES - QK_NOPE_DIM - QK_ROPE_DIM
    kr_cols = jnp.pad(wi[:, c2:c3], ((0, 0), (QK_NOPE_DIM, tail)))
    w_in_r = jnp.concatenate(
        [wi[:, :c2], wi[:, c3:c3 + conv_ch], wi[:, c3 + conv_ch:], kr_cols], axis=1).astype(BF16)
    w_qb_pad = jnp.pad(w_qb[0].reshape(q_rank, H, qk_dim),
                       ((0, 0), (0, 0), (0, HEAD_PAD - qk_dim))).reshape(q_rank, H * HEAD_PAD).astype(BF16)
    wkv = w_kvb[0].reshape(kv_rank, H, QK_NOPE_DIM + V_HEAD_DIM)
    wk_pad = jnp.pad(wkv[:, :, :QK_NOPE_DIM], ((0, 0), (0, 0), (0, HEAD_PAD - QK_NOPE_DIM)))
    w_kv_r = jnp.concatenate([wk_pad.reshape(kv_rank, H * HEAD_PAD),
                              wkv[:, :, QK_NOPE_DIM:].reshape(kv_rank, H * V_HEAD_DIM)], axis=1).astype(BF16)
    half = QK_ROPE_DIM // 2
    inv_freq = (ROPE_THETA ** (-jnp.arange(half, dtype=F32) / half)).reshape(half, 1)

    x2d = x.reshape(T, D)
    q, k, v, hc = _in_proj(x2d, positions.reshape(1, T), inv_freq, row(emb_ln_g), row(emb_ln_b), w_in_r,
                           row(q_norm_g[0]), w_qb_pad, row(kv_norm_g[0]), w_kv_r, tm=_tile(S, IN_PROJ_TOKEN_TILE))
    n_ct = conv_ch // LANES
    cv = _conv(hc.reshape(B, S, conv_ch), jnp.swapaxes(conv_w[0].reshape(CONV_WIDTH, n_ct, LANES), 0, 1),
               conv_b[0].reshape(n_ct, 1, LANES), row(conv_ln_g[0]), row(conv_ln_b[0]))
    att = _attn(q.reshape(B, S, H * HEAD_PAD), k.reshape(B, S, H * HEAD_PAD),
                v.reshape(B, S, H * V_HEAD_DIM), tq=_tile(S, ATTN_QUERY_TILE))
    wr_hi = w_router[0].astype(BF16)
    w_r2 = jnp.concatenate([wr_hi, (w_router[0] - wr_hi.astype(F32)).astype(BF16)], axis=1)
    w_r2 = jnp.pad(w_r2, ((0, 0), (0, LANES - 2 * N_EXPERTS)))
    h1, lt2 = _out_proj(x2d, att.reshape(T, H * V_HEAD_DIM), cv.reshape(T, conv_ch), row(emb_ln_g),
                        row(emb_ln_b), w_o[0].astype(BF16), row(ln1_g[0]), row(ln1_b[0]),
                        w_r2, tm=_tile(S, OUT_PROJ_TOKEN_TILE), seq=S, n_lt=2 * N_EXPERTS)
    h1 = h1.reshape(B, S, D)
    tb = _tile(S, ROUTE_TOKEN_BLOCK)
    rank, gate, off = _route(lt2, cap=cap, tb=tb)
    off_flat = jnp.swapaxes(off[:, :, :S // tb + 1], 1, 2).reshape(-1)
    nsb = _tile(S // tb, ROUTE_BLOCKS_PER_STEP)
    xg = _dispatch(off_flat, rank, h1, cap=cap, tb=tb, nsb=nsb)
    y = _experts(xg.reshape(N_EXPERTS, B * cap, D), w_gate[0], w_up[0], w_down[0],
                 tf=_tile(w_gate.shape[-1], EXPERT_FF_CHUNK)).reshape(N_EXPERTS, B, cap, D)
    return _combine(off_flat, rank, gate, y, h1, row(ln2_g[0]), row(ln2_b[0]), tb=tb, nsb=nsb)
```

```python
import functools
import math

import jax
import jax.numpy as jnp
from jax import lax
from jax.experimental import pallas as pl
from jax.experimental.pallas import tpu as pltpu

F32 = jnp.float32
BF16 = jnp.bfloat16

MLA_HEADS = 8
QK_NOPE_DIM = 64
QK_ROPE_DIM = 32
V_HEAD_DIM = 64
CONV_WIDTH = 31
ROPE_THETA = 10000.0
N_EXPERTS = 16
CAPACITY_FACTOR = 2
DEPTH = 1
DEEPNORM_ALPHA = (2.0 * DEPTH) ** 0.25
LN_EPS = 1e-5
RMS_EPS = 1e-6
LOG2_E = math.log2(math.e)

LANES = 128
HEAD_PAD = LANES
VMEM_LIMIT_BYTES = 56 * 1024 * 1024
TOKEN_SUB_TILE = 256

IN_PROJ_TOKEN_TILE = 512
OUT_PROJ_TOKEN_TILE = 2048
ATTN_QUERY_TILE = 512
EXPERT_FF_CHUNK = 1024
EXPERT_VMEM_LIMIT_BYTES = 63 * 1024 * 1024

TOPK_SEARCH_STEPS = 36
TOPK_SEARCH_FLOOR = 1e-30


def _layer_norm(x, g, b):
    mu = jnp.mean(x, axis=-1, keepdims=True)
    xc = x - mu
    var = jnp.mean(xc * xc, axis=-1, keepdims=True)
    return xc * lax.rsqrt(var + LN_EPS) * g + b


def _rms_norm(x, g):
    return x * lax.rsqrt(jnp.mean(x * x, axis=-1, keepdims=True) + RMS_EPS) * g


def _rope_tile(x, cos_t, sin_t, lane):
    half = QK_ROPE_DIM // 2
    fwd = pltpu.roll(x, LANES - half, 1)
    bwd = pltpu.roll(x, half, 1)
    partner = jnp.where(lane < QK_NOPE_DIM + half, fwd, bwd)
    return x * cos_t + partner * sin_t


def _in_proj_kernel(x_ref, pos_ref, invf_ref, lng_ref, lnb_ref, win_ref, qg_ref, wqb_ref,
                    kvg_ref, wkv_ref, q_ref, k_ref, v_ref, hc_ref, *, q_rank, kv_rank, conv_ch):
    tm = x_ref.shape[0]
    sub = min(tm, TOKEN_SUB_TILE)
    c1 = q_rank
    c2 = c1 + kv_rank
    c3 = c2 + conv_ch
    c4 = c3 + conv_ch
    tail = LANES - QK_NOPE_DIM - QK_ROPE_DIM
    nk = MLA_HEADS * HEAD_PAD
    scale = (QK_NOPE_DIM + QK_ROPE_DIM) ** -0.5 * LOG2_E
    ones = jnp.ones((QK_NOPE_DIM, sub), F32)
    zeros = jnp.zeros((QK_NOPE_DIM, sub), F32)
    lane = lax.broadcasted_iota(jnp.int32, (sub, LANES), 1)
    for r0 in range(0, tm, sub):
        rows = slice(r0, r0 + sub)
        h = _layer_norm(x_ref[rows, :], lng_ref[...], lnb_ref[...])
        proj = jnp.dot(h.astype(BF16), win_ref[...], preferred_element_type=F32)
        cq = proj[:, :c1]
        ckv = proj[:, c1:c2]
        a = proj[:, c2:c3]
        g = proj[:, c3:c4]
        kr = proj[:, c4:c4 + LANES]

        ang = pos_ref[:, rows].astype(F32) * invf_ref[...]
        cos = jnp.cos(ang)
        sin = jnp.sin(ang)
        cos_t = jnp.concatenate([ones, cos, cos, ones[:tail]], axis=0).T
        sin_t = jnp.concatenate([zeros, -sin, sin, zeros[:tail]], axis=0).T

        cqn = _rms_norm(cq, qg_ref[...])
        q = jnp.dot(cqn.astype(BF16), wqb_ref[...], preferred_element_type=F32) * scale
        ckvn = _rms_norm(ckv, kvg_ref[...])
        kv = jnp.dot(ckvn.astype(BF16), wkv_ref[...], preferred_element_type=F32)
        k_pe = _rope_tile(kr, cos_t, sin_t, lane)
        for hd in range(MLA_HEADS):
            sl = slice(hd * HEAD_PAD, (hd + 1) * HEAD_PAD)
            q_ref[rows, sl] = _rope_tile(q[:, sl], cos_t, sin_t, lane).astype(BF16)
            k_ref[rows, sl] = (kv[:, sl] + k_pe).astype(BF16)
        v_ref[rows, :] = kv[:, nk:].astype(BF16)
        hc_ref[rows, :] = a * (1.0 / (1.0 + jnp.exp(-g)))


def _in_proj(x2d, pos_row, invf_col, lng, lnb, w_in_r, qg, w_qb_pad, kvg, w_kv_r, *, tm):
    T, D = x2d.shape
    q_rank = w_qb_pad.shape[0]
    kv_rank = w_kv_r.shape[0]
    conv_ch = (w_in_r.shape[1] - q_rank - kv_rank - LANES) // 2
    nk = MLA_HEADS * HEAD_PAD
    nv = MLA_HEADS * V_HEAD_DIM
    full = lambda a: pl.BlockSpec(a.shape, lambda i: (0,) * a.ndim)
    return pl.pallas_call(
        functools.partial(_in_proj_kernel, q_rank=q_rank, kv_rank=kv_rank, conv_ch=conv_ch),
        grid=(T // tm,),
        in_specs=[
            pl.BlockSpec((tm, D), lambda i: (i, 0)),
            pl.BlockSpec((1, tm), lambda i: (0, i)),
            full(invf_col), full(lng), full(lnb), full(w_in_r), full(qg), full(w_qb_pad),
            full(kvg), full(w_kv_r),
        ],
        out_specs=[
            pl.BlockSpec((tm, nk), lambda i: (i, 0)),
            pl.BlockSpec((tm, nk), lambda i: (i, 0)),
            pl.BlockSpec((tm, nv), lambda i: (i, 0)),
            pl.BlockSpec((tm, conv_ch), lambda i: (i, 0)),
        ],
        out_shape=[
            jax.ShapeDtypeStruct((T, nk), BF16),
            jax.ShapeDtypeStruct((T, nk), BF16),
            jax.ShapeDtypeStruct((T, nv), BF16),
            jax.ShapeDtypeStruct((T, conv_ch), F32),
        ],
        compiler_params=pltpu.CompilerParams(
            dimension_semantics=("arbitrary",), vmem_limit_bytes=VMEM_LIMIT_BYTES),
        name="in_proj",
    )(x2d, pos_row, invf_col, lng, lnb, w_in_r, qg, w_qb_pad, kvg, w_kv_r)


CONV_PAD_ROWS = 16
CONV_ROW_CHUNK = 128
CONV_CHUNKS_PER_STEP = 4
CONV_WINDOW = CONV_ROW_CHUNK + 2 * CONV_PAD_ROWS


def _conv_kernel(hc_ref, cw_ref, cb_ref, g_ref, b_ref, o_ref, win_ref, y_ref):
    S, C = hc_ref.shape[1], hc_ref.shape[2]
    n_chunks = S // CONV_ROW_CHUNK
    for r in range(n_chunks):
        lo = r * CONV_ROW_CHUNK - CONV_PAD_ROWS
        hi = lo + CONV_WINDOW
        src_lo, src_hi = max(lo, 0), min(hi, S)
        for c in range(C // LANES):
            if lo < 0:
                win_ref[r, c, 0:-lo, :] = jnp.zeros((-lo, LANES), F32)
            if hi > S:
                win_ref[r, c, CONV_WINDOW - (hi - S):, :] = jnp.zeros((hi - S, LANES), F32)
            win_ref[r, c, src_lo - lo:src_hi - lo, :] = hc_ref[0, src_lo:src_hi, c * LANES:(c + 1) * LANES]
    first = CONV_PAD_ROWS - CONV_WIDTH // 2

    n_tiles = C // LANES
    group = min(CONV_CHUNKS_PER_STEP, n_chunks)

    def chunk_group(rg, carry):
        sums = []
        for u in range(group):
            r = rg * group + u

            def tile_conv(c, s1, r=r, u=u):
                acc = jnp.zeros((CONV_ROW_CHUNK, LANES), F32)
                for j in range(CONV_WIDTH):
                    acc = acc + win_ref[r, c, first + j:first + j + CONV_ROW_CHUNK, :] * cw_ref[c, j:j + 1, :]
                acc = acc + cb_ref[c]
                y_ref[u, c] = acc
                return s1 + acc

            sums.append(lax.fori_loop(0, n_tiles, tile_conv, jnp.zeros((CONV_ROW_CHUNK, LANES), F32)))
        for u in range(group):
            base = pl.multiple_of((rg * group + u) * CONV_ROW_CHUNK, CONV_ROW_CHUNK)
            mu = jnp.sum(sums[u], axis=-1, keepdims=True) * (1.0 / C)
            s2 = jnp.zeros((CONV_ROW_CHUNK, LANES), F32)
            for c in range(n_tiles):
                yc = y_ref[u, c] - mu
                s2 = s2 + yc * yc
            inv = lax.rsqrt(jnp.sum(s2, axis=-1, keepdims=True) * (1.0 / C) + LN_EPS)
            for c in range(n_tiles):
                lanes = slice(c * LANES, (c + 1) * LANES)
                y = (y_ref[u, c] - mu) * inv * g_ref[:, lanes] + b_ref[:, lanes]
                o_ref[0, pl.ds(base, CONV_ROW_CHUNK), lanes] = (y * (1.0 / (1.0 + jnp.exp(-y)))).astype(BF16)
        return carry

    lax.fori_loop(0, n_chunks // group, chunk_group, 0)


def _conv(hc, cw, cb, g, b):
    B, S, C = hc.shape
    full = lambda a: pl.BlockSpec(a.shape, lambda i: (0,) * a.ndim)
    return pl.pallas_call(
        _conv_kernel,
        grid=(B,),
        in_specs=[pl.BlockSpec((1, S, C), lambda i: (i, 0, 0)), full(cw), full(cb), full(g), full(b)],
        out_specs=pl.BlockSpec((1, S, C), lambda i: (i, 0, 0)),
        out_shape=jax.ShapeDtypeStruct((B, S, C), BF16),
        scratch_shapes=[pltpu.VMEM((S // CONV_ROW_CHUNK, C // LANES, CONV_WINDOW, LANES), F32),
                        pltpu.VMEM((CONV_CHUNKS_PER_STEP, C // LANES, CONV_ROW_CHUNK, LANES), F32)],
        compiler_params=pltpu.CompilerParams(
            dimension_semantics=("arbitrary",), vmem_limit_bytes=VMEM_LIMIT_BYTES),
        name="conv",
    )(hc, cw, cb, g, b)


HEADS_PER_TILE = LANES // V_HEAD_DIM
ATTN_HEADS_PER_STEP = 4


ATTN_SLOTS = 2


def _attn_kernel(q_ref, k_ref, v_ref, o_ref, s_ref, p_ref, *, tq):
    S = q_ref.shape[1]
    assert HEADS_PER_TILE == 2
    lane = lax.broadcasted_iota(jnp.int32, (tq, LANES), 1)
    key_lane = lax.broadcasted_iota(jnp.int32, (S, LANES), 1)
    item = 0
    for pr in range(v_ref.shape[2] // LANES):
        v = v_ref[0, :, pr * LANES:(pr + 1) * LANES]
        one = jnp.ones((S, LANES), BF16)
        v_ones = [jnp.where(key_lane < V_HEAD_DIM, v, one), jnp.where(key_lane < V_HEAD_DIM, one, v)]
        for qi in range(S // tq):
            rows = slice(qi * tq, (qi + 1) * tq)
            outs = []
            for hh in range(HEADS_PER_TILE):
                slot = item % ATTN_SLOTS
                item += 1
                c0 = (pr * HEADS_PER_TILE + hh) * HEAD_PAD
                s_ref[slot] = lax.dot_general(q_ref[0, rows, c0:c0 + HEAD_PAD], k_ref[0, :, c0:c0 + HEAD_PAD],
                                              (((1,), (1,)), ((), ())), preferred_element_type=F32)
                s = s_ref[slot]
                p_ref[slot] = jnp.exp2(s - jnp.max(s, axis=-1, keepdims=True)).astype(BF16)
                o = jnp.dot(p_ref[slot], v_ones[hh], preferred_element_type=F32)
                outs.append(o * pltpu.roll(1.0 / o, V_HEAD_DIM, 1))
            o_ref[0, rows, pr * LANES:(pr + 1) * LANES] = jnp.where(lane < V_HEAD_DIM, outs[0], outs[1]).astype(BF16)


def _attn(q, k, v, *, tq):
    B, S, _ = q.shape
    qw = ATTN_HEADS_PER_STEP * HEAD_PAD
    vw = ATTN_HEADS_PER_STEP * V_HEAD_DIM
    return pl.pallas_call(
        functools.partial(_attn_kernel, tq=tq),
        grid=(B, MLA_HEADS // ATTN_HEADS_PER_STEP),
        in_specs=[
            pl.BlockSpec((1, S, qw), lambda b, h: (b, 0, h)),
            pl.BlockSpec((1, S, qw), lambda b, h: (b, 0, h)),
            pl.BlockSpec((1, S, vw), lambda b, h: (b, 0, h)),
        ],
        out_specs=pl.BlockSpec((1, S, vw), lambda b, h: (b, 0, h)),
        out_shape=jax.ShapeDtypeStruct((B, S, MLA_HEADS * V_HEAD_DIM), BF16),
        scratch_shapes=[pltpu.VMEM((ATTN_SLOTS, tq, S), F32), pltpu.VMEM((ATTN_SLOTS, tq, S), BF16)],
        compiler_params=pltpu.CompilerParams(
            dimension_semantics=("arbitrary", "arbitrary"), vmem_limit_bytes=VMEM_LIMIT_BYTES),
        name="attn",
    )(q, k, v)


def _out_proj_kernel(x_ref, att_ref, cv_ref, lng_ref, lnb_ref, wo_ref, g1_ref, b1_ref, wr_ref,
                     h1_ref, lt_ref):
    na = att_ref.shape[1]
    tm = x_ref.shape[0]
    n_lt = lt_ref.shape[1]
    sub = min(tm, TOKEN_SUB_TILE)
    for r0 in range(0, tm, sub):
        rows = slice(r0, r0 + sub)
        h0 = _layer_norm(x_ref[rows, :], lng_ref[...], lnb_ref[...])
        mix = jnp.dot(att_ref[rows, :], wo_ref[:na, :], preferred_element_type=F32)
        mix = mix + jnp.dot(cv_ref[rows, :], wo_ref[na:, :], preferred_element_type=F32)
        h1 = _layer_norm(DEEPNORM_ALPHA * h0 + mix, g1_ref[...], b1_ref[...])
        h1_ref[rows, :] = h1
        lg = jnp.dot(h1.astype(BF16), wr_ref[...], preferred_element_type=F32)
        lt_ref[0, :, rows] = lg.T[:n_lt, :]


def _out_proj(x2d, att, cv, lng, lnb, w_o, g1, b1, w_r2, *, tm, seq, n_lt):
    T, D = x2d.shape
    per = seq // tm
    full = lambda a: pl.BlockSpec(a.shape, lambda i: (0,) * a.ndim)
    return pl.pallas_call(
        _out_proj_kernel,
        grid=(T // tm,),
        in_specs=[
            pl.BlockSpec((tm, D), lambda i: (i, 0)),
            pl.BlockSpec((tm, att.shape[1]), lambda i: (i, 0)),
            pl.BlockSpec((tm, cv.shape[1]), lambda i: (i, 0)),
            full(lng), full(lnb), full(w_o), full(g1), full(b1), full(w_r2),
        ],
        out_specs=[
            pl.BlockSpec((tm, D), lambda i: (i, 0)),
            pl.BlockSpec((1, n_lt, tm), lambda i: (i // per, 0, i % per)),
        ],
        out_shape=[
            jax.ShapeDtypeStruct((T, D), F32),
            jax.ShapeDtypeStruct((T // seq, n_lt, seq), F32),
        ],
        compiler_params=pltpu.CompilerParams(
            dimension_semantics=("arbitrary",), vmem_limit_bytes=VMEM_LIMIT_BYTES),
        name="out_proj",
    )(x2d, att, cv, lng, lnb, w_o, g1, b1, w_r2)


def _route_kernel(lt_ref, rank_ref, gate_ref, off_ref, *, cap, tb):
    B, E, S = rank_ref.shape
    lt = lt_ref[:, :E, :] + lt_ref[:, E:, :]
    ex = jnp.exp(lt - jnp.max(lt, axis=1, keepdims=True))
    aff = ex / jnp.sum(ex, axis=1, keepdims=True)
    v = aff.reshape(B * E, S)
    rows = B * E
    kf = float(cap)

    def step(_, carry):
        lo, hi = carry
        mid = jnp.sqrt(jnp.maximum(lo, TOPK_SEARCH_FLOOR)) * jnp.sqrt(hi)
        cnt = jnp.sum(jnp.where(v >= mid, 1.0, 0.0), axis=1, keepdims=True)
        ge = cnt >= kf
        return jnp.where(ge, mid, lo), jnp.where(ge, hi, mid)

    lo, hi = lax.fori_loop(0, TOPK_SEARCH_STEPS, step,
                           (jnp.zeros((rows, 1), F32), jnp.full((rows, 1), 2.0, F32)))
    above = v >= hi
    tie = jnp.logical_and(v >= lo, jnp.logical_not(above))
    stacked = jnp.concatenate([jnp.where(above, 1.0, 0.0), jnp.where(tie, 1.0, 0.0)], axis=0)
    upper = jnp.where(lax.broadcasted_iota(jnp.int32, (S, S), 0) <= lax.broadcasted_iota(jnp.int32, (S, S), 1),
                      1.0, 0.0).astype(BF16)
    pc = jnp.dot(stacked.astype(BF16), upper, preferred_element_type=F32)
    pa = pc[:rows]
    pt = pc[rows:]
    need = kf - jnp.sum(jnp.where(above, 1.0, 0.0), axis=1, keepdims=True)
    sel = jnp.logical_or(above, jnp.logical_and(tie, pt <= need))
    taken = pa + jnp.minimum(pt, need)
    rank_ref[...] = jnp.where(sel, taken - 1.0, -1.0).astype(jnp.int32).reshape(B, E, S)
    gate_ref[...] = jnp.where(sel, v, 0.0).reshape(B, E, S)
    pick = jnp.where(lax.broadcasted_iota(jnp.int32, (S, LANES), 0) + 1
                     == lax.broadcasted_iota(jnp.int32, (S, LANES), 1) * tb, 1.0, 0.0).astype(BF16)
    off = jnp.dot(taken.astype(BF16), pick, preferred_element_type=F32)
    off_ref[...] = off.astype(jnp.int32).reshape(B, E, LANES)


def _route(lt2, *, cap, tb):
    B, E2, S = lt2.shape
    E = E2 // 2
    assert cap <= 256 and S // tb < LANES
    spec = pl.BlockSpec((B, E, S), lambda i: (0, 0, 0))
    ospec = pl.BlockSpec((B, E, LANES), lambda i: (0, 0, 0))
    return pl.pallas_call(
        functools.partial(_route_kernel, cap=cap, tb=tb),
        grid=(1,),
        in_specs=[pl.BlockSpec((B, E2, S), lambda i: (0, 0, 0))],
        out_specs=[spec, spec, ospec],
        out_shape=[jax.ShapeDtypeStruct((B, E, S), jnp.int32), jax.ShapeDtypeStruct((B, E, S), F32),
                   jax.ShapeDtypeStruct((B, E, LANES), jnp.int32)],
        compiler_params=pltpu.CompilerParams(
            dimension_semantics=("arbitrary",), vmem_limit_bytes=VMEM_LIMIT_BYTES),
        name="route",
    )(lt2)


ROUTE_TOKEN_BLOCK = 256
DISPATCH_BLOCKS_PER_STEP = 8
COMBINE_BLOCKS_PER_STEP = 4
SLOT_WINDOW = 64
SLOT_ALIGN = 16


def _slot_windows(off_ref, base, n_experts, cap, win):
    starts = []
    n_pass = jnp.int32(1)
    for e in range(n_experts):
        off = off_ref[base + e]
        end = off_ref[base + n_experts + e]
        start = jnp.minimum((off // SLOT_ALIGN) * SLOT_ALIGN, cap - win)
        starts.append(start)
        n_pass = jnp.maximum(n_pass, (end - start + (win - 1)) // win)
    return starts, n_pass


def _dispatch_kernel(off_ref, rank_ref, h_ref, xg_ref, *, cap, win, tb):
    E = rank_ref.shape[1]
    nsb = rank_ref.shape[2] // tb
    b, j = pl.program_id(0), pl.program_id(1)
    n_bounds = pl.num_programs(1) * nsb + 1

    @pl.when(j == 0)
    def _():
        xg_ref[...] = jnp.zeros(xg_ref.shape, BF16)

    sub = lax.broadcasted_iota(jnp.int32, (win, tb), 0)

    def one_pass(sb, starts, p, first):
        cols = slice(sb * tb, (sb + 1) * tb)
        pieces, rows = [], []
        for e in range(E):
            lo = starts[e] + p * win
            ws = pl.multiple_of(jnp.minimum(lo, cap - win), SLOT_ALIGN)
            tgt = ws + sub
            hit = rank_ref[0, e:e + 1, cols] == tgt
            if not first:
                hit = jnp.logical_and(hit, tgt >= lo)
            pieces.append(jnp.where(hit, 1.0, 0.0).astype(BF16))
            rows.append(ws)
        onehot = jnp.concatenate(pieces, axis=0)
        hb = h_ref[0, cols, :].astype(BF16)
        got = jnp.dot(onehot, hb, preferred_element_type=F32).astype(BF16)
        for e in range(E):
            xg_ref[e, 0, pl.ds(rows[e], win), :] += got[e * win:(e + 1) * win, :]

    plans = []
    for sb in range(nsb):
        starts, n_pass = _slot_windows(off_ref, (b * n_bounds + j * nsb + sb) * E, E, cap, win)
        plans.append((starts, n_pass))
        one_pass(sb, starts, 0, True)
    for sb in range(nsb):
        starts, n_pass = plans[sb]

        def extra(p, carry, sb=sb, starts=starts):
            one_pass(sb, starts, p, False)
            return carry

        lax.fori_loop(1, n_pass, extra, 0)


def _dispatch(off_flat, rank, h1b, *, cap, tb, nsb):
    B, E, S = rank.shape
    D = h1b.shape[-1]
    win = min(SLOT_WINDOW, cap)
    ts = tb * nsb
    assert cap % SLOT_ALIGN == 0 and win % SLOT_ALIGN == 0 and S % ts == 0
    return pl.pallas_call(
        functools.partial(_dispatch_kernel, cap=cap, win=win, tb=tb),
        grid_spec=pltpu.PrefetchScalarGridSpec(
            num_scalar_prefetch=1,
            grid=(B, S // ts),
            in_specs=[pl.BlockSpec((1, E, ts), lambda b, j, off: (b, 0, j)),
                      pl.BlockSpec((1, ts, D), lambda b, j, off: (b, j, 0))],
            out_specs=pl.BlockSpec((E, 1, cap, D), lambda b, j, off: (0, b, 0, 0)),
        ),
        out_shape=jax.ShapeDtypeStruct((E, B, cap, D), BF16),
        compiler_params=pltpu.CompilerParams(
            dimension_semantics=("arbitrary", "arbitrary"), vmem_limit_bytes=VMEM_LIMIT_BYTES),
        name="dispatch",
    )(off_flat, rank, h1b)


EXPERT_ROW_BLOCK = 1024


def _expert_kernel(x_ref, wg_ref, wu_ref, wd_ref, y_ref, acc_ref, wgb_ref, wub_ref, wdb_ref, *, n_chunks):
    f = pl.program_id(1)
    rows = x_ref.shape[1]
    rb = min(rows, EXPERT_ROW_BLOCK)
    last = n_chunks - 1

    def chunk(first, final):
        wgb_ref[...] = wg_ref[0].astype(BF16)
        wub_ref[...] = wu_ref[0].astype(BF16)
        wdb_ref[...] = wd_ref[0].astype(BF16)
        for r0 in range(0, rows, rb):
            x = x_ref[0, r0:r0 + rb, :]
            a = jnp.dot(x, wgb_ref[...], preferred_element_type=F32)
            u = jnp.dot(x, wub_ref[...], preferred_element_type=F32)
            hmid = (a * (1.0 / (1.0 + jnp.exp(-a))) * u).astype(BF16)
            part = jnp.dot(hmid, wdb_ref[...], preferred_element_type=F32)
            if not first:
                part = acc_ref[r0:r0 + rb, :] + part
            if final:
                y_ref[0, r0:r0 + rb, :] = part.astype(BF16)
            else:
                acc_ref[r0:r0 + rb, :] = part

    if n_chunks == 1:
        chunk(True, True)
    else:
        pl.when(f == 0)(lambda: chunk(True, False))
        if n_chunks > 2:
            pl.when(jnp.logical_and(f > 0, f < last))(lambda: chunk(False, False))
        pl.when(f == last)(lambda: chunk(False, True))


def _experts(xg, w_gate, w_up, w_down, *, tf):
    E, rows, D = xg.shape
    F = w_gate.shape[-1]
    return pl.pallas_call(
        functools.partial(_expert_kernel, n_chunks=F // tf),
        grid=(E, F // tf),
        in_specs=[
            pl.BlockSpec((1, rows, D), lambda e, f: (e, 0, 0)),
            pl.BlockSpec((1, D, tf), lambda e, f: (e, 0, f)),
            pl.BlockSpec((1, D, tf), lambda e, f: (e, 0, f)),
            pl.BlockSpec((1, tf, D), lambda e, f: (e, f, 0)),
        ],
        out_specs=pl.BlockSpec((1, rows, D), lambda e, f: (e, 0, 0)),
        out_shape=jax.ShapeDtypeStruct((E, rows, D), BF16),
        scratch_shapes=[pltpu.VMEM((rows, D), F32), pltpu.VMEM((D, tf), BF16), pltpu.VMEM((D, tf), BF16),
                        pltpu.VMEM((tf, D), BF16)],
        compiler_params=pltpu.CompilerParams(
            dimension_semantics=("arbitrary", "arbitrary"), vmem_limit_bytes=EXPERT_VMEM_LIMIT_BYTES),
        name="experts",
    )(xg, w_gate, w_up, w_down)


def _combine_kernel(off_ref, rank_ref, gate_ref, y_ref, h_ref, g2_ref, b2_ref, o_ref, *, cap, win, tb):
    E = rank_ref.shape[1]
    nsb = rank_ref.shape[2] // tb
    b, j = pl.program_id(0), pl.program_id(1)
    n_bounds = pl.num_programs(1) * nsb + 1
    per_tile = LANES // win
    lane = lax.broadcasted_iota(jnp.int32, (1, LANES), 1)
    first_lane = lax.broadcasted_iota(jnp.int32, (LANES, E * win), 0) * win
    col = lax.broadcasted_iota(jnp.int32, (LANES, E * win), 1)
    spread = jnp.where(jnp.logical_and(col >= first_lane, col < first_lane + win), 1.0, 0.0).astype(BF16)
    pad_rows = jnp.zeros((LANES - E, tb), F32)

    def token_major(x_et):
        return jnp.concatenate([x_et, pad_rows], axis=0).T.astype(BF16)

    def one_pass(sb, starts, p, first):
        cols = slice(sb * tb, (sb + 1) * tb)
        rk = jnp.dot(token_major(rank_ref[0, :, cols].astype(F32)), spread, preferred_element_type=F32)
        gt = jnp.dot(token_major(gate_ref[0, :, cols]), spread, preferred_element_type=F32)
        tgts, los, ys = [], [], []
        for t in range(E // per_tile):
            tgt = lo_v = None
            for u in range(per_tile):
                e = t * per_tile + u
                lo = starts[e] + p * win
                ws = pl.multiple_of(jnp.minimum(lo, cap - win), SLOT_ALIGN)
                ys.append(y_ref[e, 0, pl.ds(ws, win), :])
                t_u = ws + lane - u * win
                if u == 0:
                    tgt, lo_v = t_u, jnp.full((1, LANES), lo, jnp.int32)
                else:
                    here = lane >= u * win
                    tgt = jnp.where(here, t_u, tgt)
                    lo_v = jnp.where(here, lo, lo_v)
            tgts.append(tgt)
            los.append(lo_v)
        tgt = jnp.concatenate(tgts, axis=1)
        hit = rk == tgt.astype(F32)
        if not first:
            hit = jnp.logical_and(hit, tgt >= jnp.concatenate(los, axis=1))
        gates = jnp.where(hit, gt, 0.0).astype(BF16)
        return jnp.dot(gates, jnp.concatenate(ys, axis=0), preferred_element_type=F32)

    plans = []
    for sb in range(nsb):
        rows = slice(sb * tb, (sb + 1) * tb)
        starts, n_pass = _slot_windows(off_ref, (b * n_bounds + j * nsb + sb) * E, E, cap, win)
        plans.append((starts, n_pass))
        o_ref[0, rows, :] = DEEPNORM_ALPHA * h_ref[0, rows, :] + one_pass(sb, starts, 0, True)
    for sb in range(nsb):
        starts, n_pass = plans[sb]

        def extra(p, carry, sb=sb, starts=starts):
            o_ref[0, sb * tb:(sb + 1) * tb, :] += one_pass(sb, starts, p, False)
            return carry

        lax.fori_loop(1, n_pass, extra, 0)
    o_ref[0] = _layer_norm(o_ref[0], g2_ref[...], b2_ref[...])


def _combine(off_flat, rank, gate, y, h1, g2, b2, *, tb, nsb):
    B, E, S = rank.shape
    cap, D = y.shape[2], y.shape[3]
    win = min(SLOT_WINDOW, cap)
    ts = tb * nsb
    assert cap % SLOT_ALIGN == 0 and win % SLOT_ALIGN == 0 and LANES % win == 0 and E % (LANES // win) == 0
    assert S % ts == 0
    full = lambda a: pl.BlockSpec(a.shape, lambda b, j, off: (0,) * a.ndim)
    return pl.pallas_call(
        functools.partial(_combine_kernel, cap=cap, win=win, tb=tb),
        grid_spec=pltpu.PrefetchScalarGridSpec(
            num_scalar_prefetch=1,
            grid=(B, S // ts),
            in_specs=[
                pl.BlockSpec((1, E, ts), lambda b, j, off: (b, 0, j)),
                pl.BlockSpec((1, E, ts), lambda b, j, off: (b, 0, j)),
                pl.BlockSpec((E, 1, cap, D), lambda b, j, off: (0, b, 0, 0)),
                pl.BlockSpec((1, ts, D), lambda b, j, off: (b, j, 0)),
                full(g2), full(b2),
            ],
            out_specs=pl.BlockSpec((1, ts, D), lambda b, j, off: (b, j, 0)),
        ),
        out_shape=jax.ShapeDtypeStruct((B, S, D), F32),
        compiler_params=pltpu.CompilerParams(
            dimension_semantics=("arbitrary", "arbitrary"), vmem_limit_bytes=VMEM_LIMIT_BYTES),
        name="combine",
    )(off_flat, rank, gate, y, h1, g2, b2)


def _tile(n, target):
    t = min(n, target)
    assert n % t == 0, (n, t)
    return t


def kernel(x, positions, emb_ln_g, emb_ln_b, w_in, q_norm_g, w_qb, kv_norm_g, w_kvb, conv_w, conv_b,
           conv_ln_g, conv_ln_b, w_o, ln1_g, ln1_b, w_router, w_gate, w_up, w_down, ln2_g, ln2_b):
    B, S, D = x.shape
    T = B * S
    H = MLA_HEADS
    q_rank = q_norm_g.shape[-1]
    kv_rank = kv_norm_g.shape[-1]
    conv_ch = conv_w.shape[-1]
    qk_dim = QK_NOPE_DIM + QK_ROPE_DIM
    cap = CAPACITY_FACTOR * S // N_EXPERTS
    assert w_in.shape[0] == DEPTH == 1
    row = lambda a: a.reshape(1, -1)

    wi = w_in[0]
    c1, c2, c3 = q_rank, q_rank + kv_rank, q_rank + kv_rank + QK_ROPE_DIM
    tail = LANES - QK_NOPE_DIM - QK_ROPE_DIM
    kr_cols = jnp.pad(wi[:, c2:c3], ((0, 0), (QK_NOPE_DIM, tail)))
    w_in_r = jnp.concatenate(
        [wi[:, :c2], wi[:, c3:c3 + conv_ch], wi[:, c3 + conv_ch:], kr_cols], axis=1).astype(BF16)
    w_qb_pad = jnp.pad(w_qb[0].reshape(q_rank, H, qk_dim),
                       ((0, 0), (0, 0), (0, HEAD_PAD - qk_dim))).reshape(q_rank, H * HEAD_PAD).astype(BF16)
    wkv = w_kvb[0].reshape(kv_rank, H, QK_NOPE_DIM + V_HEAD_DIM)
    wk_pad = jnp.pad(wkv[:, :, :QK_NOPE_DIM], ((0, 0), (0, 0), (0, HEAD_PAD - QK_NOPE_DIM)))
    w_kv_r = jnp.concatenate([wk_pad.reshape(kv_rank, H * HEAD_PAD),
                              wkv[:, :, QK_NOPE_DIM:].reshape(kv_rank, H * V_HEAD_DIM)], axis=1).astype(BF16)
    half = QK_ROPE_DIM // 2
    inv_freq = (ROPE_THETA ** (-jnp.arange(half, dtype=F32) / half)).reshape(half, 1)

    x2d = x.reshape(T, D)
    q, k, v, hc = _in_proj(x2d, positions.reshape(1, T), inv_freq, row(emb_ln_g), row(emb_ln_b), w_in_r,
                           row(q_norm_g[0]), w_qb_pad, row(kv_norm_g[0]), w_kv_r, tm=_tile(S, IN_PROJ_TOKEN_TILE))
    n_ct = conv_ch // LANES
    cv = _conv(hc.reshape(B, S, conv_ch), jnp.swapaxes(conv_w[0].reshape(CONV_WIDTH, n_ct, LANES), 0, 1),
               conv_b[0].reshape(n_ct, 1, LANES), row(conv_ln_g[0]), row(conv_ln_b[0]))
    att = _attn(q.reshape(B, S, H * HEAD_PAD), k.reshape(B, S, H * HEAD_PAD),
                v.reshape(B, S, H * V_HEAD_DIM), tq=_tile(S, ATTN_QUERY_TILE))
    wr_hi = w_router[0].astype(BF16)
    w_r2 = jnp.concatenate([wr_hi, (w_router[0] - wr_hi.astype(F32)).astype(BF16)], axis=1)
    w_r2 = jnp.pad(w_r2, ((0, 0), (0, LANES - 2 * N_EXPERTS)))
    h1, lt2 = _out_proj(x2d, att.reshape(T, H * V_HEAD_DIM), cv.reshape(T, conv_ch), row(emb_ln_g),
                        row(emb_ln_b), w_o[0].astype(BF16), row(ln1_g[0]), row(ln1_b[0]),
                        w_r2, tm=_tile(S, OUT_PROJ_TOKEN_TILE), seq=S, n_lt=2 * N_EXPERTS)
    h1 = h1.reshape(B, S, D)
    tb = _tile(S, ROUTE_TOKEN_BLOCK)
    rank, gate, off = _route(lt2, cap=cap, tb=tb)
    off_flat = jnp.swapaxes(off[:, :, :S // tb + 1], 1, 2).reshape(-1)
    xg = _dispatch(off_flat, rank, h1, cap=cap, tb=tb, nsb=_tile(S // tb, DISPATCH_BLOCKS_PER_STEP))
    y = _experts(xg.reshape(N_EXPERTS, B * cap, D), w_gate[0], w_up[0], w_down[0],
                 tf=_tile(w_gate.shape[-1], EXPERT_FF_CHUNK)).reshape(N_EXPERTS, B, cap, D)
    return _combine(off_flat, rank, gate, y, h1, row(ln2_g[0]), row(ln2_b[0]), tb=tb,
                    nsb=_tile(S // tb, COMBINE_BLOCKS_PER_STEP))
```

```python
import functools
import math

import jax
import jax.numpy as jnp
from jax import lax
from jax.experimental import pallas as pl
from jax.experimental.pallas import tpu as pltpu

F32 = jnp.float32
BF16 = jnp.bfloat16

MLA_HEADS = 8
QK_NOPE_DIM = 64
QK_ROPE_DIM = 32
V_HEAD_DIM = 64
CONV_WIDTH = 31
ROPE_THETA = 10000.0
N_EXPERTS = 16
CAPACITY_FACTOR = 2
DEPTH = 1
DEEPNORM_ALPHA = (2.0 * DEPTH) ** 0.25
LN_EPS = 1e-5
RMS_EPS = 1e-6
LOG2_E = math.log2(math.e)

LANES = 128
HEAD_PAD = LANES
VMEM_LIMIT_BYTES = 56 * 1024 * 1024
TOKEN_SUB_TILE = 256

IN_PROJ_TOKEN_TILE = 512
OUT_PROJ_TOKEN_TILE = 2048
ATTN_QUERY_TILE = 512
EXPERT_FF_CHUNK = 1024
EXPERT_VMEM_LIMIT_BYTES = 63 * 1024 * 1024

TOPK_SEARCH_STEPS = 36
TOPK_SEARCH_FLOOR = 1e-30


def _layer_norm(x, g, b):
    mu = jnp.mean(x, axis=-1, keepdims=True)
    xc = x - mu
    var = jnp.mean(xc * xc, axis=-1, keepdims=True)
    return xc * lax.rsqrt(var + LN_EPS) * g + b


def _rms_norm(x, g):
    return x * lax.rsqrt(jnp.mean(x * x, axis=-1, keepdims=True) + RMS_EPS) * g


def _rope_tile(x, cos_t, sin_t, lane):
    half = QK_ROPE_DIM // 2
    fwd = pltpu.roll(x, LANES - half, 1)
    bwd = pltpu.roll(x, half, 1)
    partner = jnp.where(lane < QK_NOPE_DIM + half, fwd, bwd)
    return x * cos_t + partner * sin_t


def _in_proj_kernel(x_ref, pos_ref, invf_ref, lng_ref, lnb_ref, win_ref, qg_ref, wqb_ref,
                    kvg_ref, wkv_ref, q_ref, k_ref, v_ref, hc_ref, *, q_rank, kv_rank, conv_ch):
    tm = x_ref.shape[0]
    sub = min(tm, TOKEN_SUB_TILE)
    c1 = q_rank
    c2 = c1 + kv_rank
    c3 = c2 + conv_ch
    c4 = c3 + conv_ch
    tail = LANES - QK_NOPE_DIM - QK_ROPE_DIM
    nk = MLA_HEADS * HEAD_PAD
    scale = (QK_NOPE_DIM + QK_ROPE_DIM) ** -0.5 * LOG2_E
    ones = jnp.ones((QK_NOPE_DIM, sub), F32)
    zeros = jnp.zeros((QK_NOPE_DIM, sub), F32)
    lane = lax.broadcasted_iota(jnp.int32, (sub, LANES), 1)
    for r0 in range(0, tm, sub):
        rows = slice(r0, r0 + sub)
        h = _layer_norm(x_ref[rows, :], lng_ref[...], lnb_ref[...])
        proj = jnp.dot(h.astype(BF16), win_ref[...], preferred_element_type=F32)
        cq = proj[:, :c1]
        ckv = proj[:, c1:c2]
        a = proj[:, c2:c3]
        g = proj[:, c3:c4]
        kr = proj[:, c4:c4 + LANES]

        ang = pos_ref[:, rows].astype(F32) * invf_ref[...]
        cos = jnp.cos(ang)
        sin = jnp.sin(ang)
        cos_t = jnp.concatenate([ones, cos, cos, ones[:tail]], axis=0).T
        sin_t = jnp.concatenate([zeros, -sin, sin, zeros[:tail]], axis=0).T

        cqn = _rms_norm(cq, qg_ref[...])
        q = jnp.dot(cqn.astype(BF16), wqb_ref[...], preferred_element_type=F32) * scale
        ckvn = _rms_norm(ckv, kvg_ref[...])
        kv = jnp.dot(ckvn.astype(BF16), wkv_ref[...], preferred_element_type=F32)
        k_pe = _rope_tile(kr, cos_t, sin_t, lane)
        for hd in range(MLA_HEADS):
            sl = slice(hd * HEAD_PAD, (hd + 1) * HEAD_PAD)
            q_ref[rows, sl] = _rope_tile(q[:, sl], cos_t, sin_t, lane).astype(BF16)
            k_ref[rows, sl] = (kv[:, sl] + k_pe).astype(BF16)
        v_ref[rows, :] = kv[:, nk:].astype(BF16)
        hc_ref[rows, :] = a * (1.0 / (1.0 + jnp.exp(-g)))


def _in_proj(x2d, pos_row, invf_col, lng, lnb, w_in_r, qg, w_qb_pad, kvg, w_kv_r, *, tm):
    T, D = x2d.shape
    q_rank = w_qb_pad.shape[0]
    kv_rank = w_kv_r.shape[0]
    conv_ch = (w_in_r.shape[1] - q_rank - kv_rank - LANES) // 2
    nk = MLA_HEADS * HEAD_PAD
    nv = MLA_HEADS * V_HEAD_DIM
    full = lambda a: pl.BlockSpec(a.shape, lambda i: (0,) * a.ndim)
    return pl.pallas_call(
        functools.partial(_in_proj_kernel, q_rank=q_rank, kv_rank=kv_rank, conv_ch=conv_ch),
        grid=(T // tm,),
        in_specs=[
            pl.BlockSpec((tm, D), lambda i: (i, 0)),
            pl.BlockSpec((1, tm), lambda i: (0, i)),
            full(invf_col), full(lng), full(lnb), full(w_in_r), full(qg), full(w_qb_pad),
            full(kvg), full(w_kv_r),
        ],
        out_specs=[
            pl.BlockSpec((tm, nk), lambda i: (i, 0)),
            pl.BlockSpec((tm, nk), lambda i: (i, 0)),
            pl.BlockSpec((tm, nv), lambda i: (i, 0)),
            pl.BlockSpec((tm, conv_ch), lambda i: (i, 0)),
        ],
        out_shape=[
            jax.ShapeDtypeStruct((T, nk), BF16),
            jax.ShapeDtypeStruct((T, nk), BF16),
            jax.ShapeDtypeStruct((T, nv), BF16),
            jax.ShapeDtypeStruct((T, conv_ch), F32),
        ],
        compiler_params=pltpu.CompilerParams(
            dimension_semantics=("arbitrary",), vmem_limit_bytes=VMEM_LIMIT_BYTES),
        name="in_proj",
    )(x2d, pos_row, invf_col, lng, lnb, w_in_r, qg, w_qb_pad, kvg, w_kv_r)


CONV_PAD_ROWS = 16
CONV_ROW_CHUNK = 128
CONV_CHUNKS_PER_STEP = 4
CONV_WINDOW = CONV_ROW_CHUNK + 2 * CONV_PAD_ROWS


def _conv_kernel(hc_ref, cw_ref, cb_ref, g_ref, b_ref, o_ref, win_ref, y_ref):
    S, C = hc_ref.shape[1], hc_ref.shape[2]
    n_chunks = S // CONV_ROW_CHUNK
    for r in range(n_chunks):
        lo = r * CONV_ROW_CHUNK - CONV_PAD_ROWS
        hi = lo + CONV_WINDOW
        src_lo, src_hi = max(lo, 0), min(hi, S)
        for c in range(C // LANES):
            if lo < 0:
                win_ref[r, c, 0:-lo, :] = jnp.zeros((-lo, LANES), F32)
            if hi > S:
                win_ref[r, c, CONV_WINDOW - (hi - S):, :] = jnp.zeros((hi - S, LANES), F32)
            win_ref[r, c, src_lo - lo:src_hi - lo, :] = hc_ref[0, src_lo:src_hi, c * LANES:(c + 1) * LANES]
    first = CONV_PAD_ROWS - CONV_WIDTH // 2

    n_tiles = C // LANES
    group = min(CONV_CHUNKS_PER_STEP, n_chunks)

    def chunk_group(rg, carry):
        sums = []
        for u in range(group):
            r = rg * group + u

            def tile_conv(c, s1, r=r, u=u):
                acc = jnp.zeros((CONV_ROW_CHUNK, LANES), F32)
                for j in range(CONV_WIDTH):
                    acc = acc + win_ref[r, c, first + j:first + j + CONV_ROW_CHUNK, :] * cw_ref[c, j:j + 1, :]
                acc = acc + cb_ref[c]
                y_ref[u, c] = acc
                return s1 + acc

            sums.append(lax.fori_loop(0, n_tiles, tile_conv, jnp.zeros((CONV_ROW_CHUNK, LANES), F32)))
        for u in range(group):
            base = pl.multiple_of((rg * group + u) * CONV_ROW_CHUNK, CONV_ROW_CHUNK)
            mu = jnp.sum(sums[u], axis=-1, keepdims=True) * (1.0 / C)
            s2 = jnp.zeros((CONV_ROW_CHUNK, LANES), F32)
            for c in range(n_tiles):
                yc = y_ref[u, c] - mu
                s2 = s2 + yc * yc
            inv = lax.rsqrt(jnp.sum(s2, axis=-1, keepdims=True) * (1.0 / C) + LN_EPS)
            for c in range(n_tiles):
                lanes = slice(c * LANES, (c + 1) * LANES)
                y = (y_ref[u, c] - mu) * inv * g_ref[:, lanes] + b_ref[:, lanes]
                o_ref[0, pl.ds(base, CONV_ROW_CHUNK), lanes] = (y * (1.0 / (1.0 + jnp.exp(-y)))).astype(BF16)
        return carry

    lax.fori_loop(0, n_chunks // group, chunk_group, 0)


def _conv(hc, cw, cb, g, b):
    B, S, C = hc.shape
    full = lambda a: pl.BlockSpec(a.shape, lambda i: (0,) * a.ndim)
    return pl.pallas_call(
        _conv_kernel,
        grid=(B,),
        in_specs=[pl.BlockSpec((1, S, C), lambda i: (i, 0, 0)), full(cw), full(cb), full(g), full(b)],
        out_specs=pl.BlockSpec((1, S, C), lambda i: (i, 0, 0)),
        out_shape=jax.ShapeDtypeStruct((B, S, C), BF16),
        scratch_shapes=[pltpu.VMEM((S // CONV_ROW_CHUNK, C // LANES, CONV_WINDOW, LANES), F32),
                        pltpu.VMEM((CONV_CHUNKS_PER_STEP, C // LANES, CONV_ROW_CHUNK, LANES), F32)],
        compiler_params=pltpu.CompilerParams(
            dimension_semantics=("arbitrary",), vmem_limit_bytes=VMEM_LIMIT_BYTES),
        name="conv",
    )(hc, cw, cb, g, b)


HEADS_PER_TILE = LANES // V_HEAD_DIM
ATTN_HEADS_PER_STEP = 4


ATTN_SLOTS = 2


def _attn_kernel(q_ref, k_ref, v_ref, o_ref, s_ref, p_ref, *, tq):
    S = q_ref.shape[1]
    assert HEADS_PER_TILE == 2
    lane = lax.broadcasted_iota(jnp.int32, (tq, LANES), 1)
    key_lane = lax.broadcasted_iota(jnp.int32, (S, LANES), 1)
    item = 0
    for pr in range(v_ref.shape[2] // LANES):
        v = v_ref[0, :, pr * LANES:(pr + 1) * LANES]
        one = jnp.ones((S, LANES), BF16)
        v_ones = [jnp.where(key_lane < V_HEAD_DIM, v, one), jnp.where(key_lane < V_HEAD_DIM, one, v)]
        for qi in range(S // tq):
            rows = slice(qi * tq, (qi + 1) * tq)
            outs = []
            for hh in range(HEADS_PER_TILE):
                slot = item % ATTN_SLOTS
                item += 1
                c0 = (pr * HEADS_PER_TILE + hh) * HEAD_PAD
                s_ref[slot] = lax.dot_general(q_ref[0, rows, c0:c0 + HEAD_PAD], k_ref[0, :, c0:c0 + HEAD_PAD],
                                              (((1,), (1,)), ((), ())), preferred_element_type=F32)
                s = s_ref[slot]
                p_ref[slot] = jnp.exp2(s - jnp.max(s, axis=-1, keepdims=True)).astype(BF16)
                o = jnp.dot(p_ref[slot], v_ones[hh], preferred_element_type=F32)
                outs.append(o * pltpu.roll(1.0 / o, V_HEAD_DIM, 1))
            o_ref[0, rows, pr * LANES:(pr + 1) * LANES] = jnp.where(lane < V_HEAD_DIM, outs[0], outs[1]).astype(BF16)


def _attn(q, k, v, *, tq):
    B, S, _ = q.shape
    qw = ATTN_HEADS_PER_STEP * HEAD_PAD
    vw = ATTN_HEADS_PER_STEP * V_HEAD_DIM
    return pl.pallas_call(
        functools.partial(_attn_kernel, tq=tq),
        grid=(B, MLA_HEADS // ATTN_HEADS_PER_STEP),
        in_specs=[
            pl.BlockSpec((1, S, qw), lambda b, h: (b, 0, h)),
            pl.BlockSpec((1, S, qw), lambda b, h: (b, 0, h)),
            pl.BlockSpec((1, S, vw), lambda b, h: (b, 0, h)),
        ],
        out_specs=pl.BlockSpec((1, S, vw), lambda b, h: (b, 0, h)),
        out_shape=jax.ShapeDtypeStruct((B, S, MLA_HEADS * V_HEAD_DIM), BF16),
        scratch_shapes=[pltpu.VMEM((ATTN_SLOTS, tq, S), F32), pltpu.VMEM((ATTN_SLOTS, tq, S), BF16)],
        compiler_params=pltpu.CompilerParams(
            dimension_semantics=("arbitrary", "arbitrary"), vmem_limit_bytes=VMEM_LIMIT_BYTES),
        name="attn",
    )(q, k, v)


def _out_proj_kernel(x_ref, att_ref, cv_ref, lng_ref, lnb_ref, wo_ref, g1_ref, b1_ref, wr_ref,
                     h1_ref, lt_ref):
    na = att_ref.shape[1]
    tm = x_ref.shape[0]
    n_lt = lt_ref.shape[1]
    sub = min(tm, TOKEN_SUB_TILE)
    for r0 in range(0, tm, sub):
        rows = slice(r0, r0 + sub)
        h0 = _layer_norm(x_ref[rows, :], lng_ref[...], lnb_ref[...])
        mix = jnp.dot(att_ref[rows, :], wo_ref[:na, :], preferred_element_type=F32)
        mix = mix + jnp.dot(cv_ref[rows, :], wo_ref[na:, :], preferred_element_type=F32)
        h1 = _layer_norm(DEEPNORM_ALPHA * h0 + mix, g1_ref[...], b1_ref[...])
        h1_ref[rows, :] = h1
        lg = jnp.dot(h1.astype(BF16), wr_ref[...], preferred_element_type=F32)
        lt_ref[0, :, rows] = lg.T[:n_lt, :]


def _out_proj(x2d, att, cv, lng, lnb, w_o, g1, b1, w_r2, *, tm, seq, n_lt):
    T, D = x2d.shape
    per = seq // tm
    full = lambda a: pl.BlockSpec(a.shape, lambda i: (0,) * a.ndim)
    return pl.pallas_call(
        _out_proj_kernel,
        grid=(T // tm,),
        in_specs=[
            pl.BlockSpec((tm, D), lambda i: (i, 0)),
            pl.BlockSpec((tm, att.shape[1]), lambda i: (i, 0)),
            pl.BlockSpec((tm, cv.shape[1]), lambda i: (i, 0)),
            full(lng), full(lnb), full(w_o), full(g1), full(b1), full(w_r2),
        ],
        out_specs=[
            pl.BlockSpec((tm, D), lambda i: (i, 0)),
            pl.BlockSpec((1, n_lt, tm), lambda i: (i // per, 0, i % per)),
        ],
        out_shape=[
            jax.ShapeDtypeStruct((T, D), F32),
            jax.ShapeDtypeStruct((T // seq, n_lt, seq), F32),
        ],
        compiler_params=pltpu.CompilerParams(
            dimension_semantics=("arbitrary",), vmem_limit_bytes=VMEM_LIMIT_BYTES),
        name="out_proj",
    )(x2d, att, cv, lng, lnb, w_o, g1, b1, w_r2)


def _route_kernel(lt_ref, rank_ref, gate_ref, off_ref, *, cap, tb):
    B, E, S = rank_ref.shape
    lt = lt_ref[:, :E, :] + lt_ref[:, E:, :]
    ex = jnp.exp(lt - jnp.max(lt, axis=1, keepdims=True))
    aff = ex / jnp.sum(ex, axis=1, keepdims=True)
    v = aff.reshape(B * E, S)
    rows = B * E
    kf = float(cap)

    def step(_, carry):
        lo, hi = carry
        mid = jnp.sqrt(jnp.maximum(lo, TOPK_SEARCH_FLOOR)) * jnp.sqrt(hi)
        cnt = jnp.sum(jnp.where(v >= mid, 1.0, 0.0), axis=1, keepdims=True)
        ge = cnt >= kf
        return jnp.where(ge, mid, lo), jnp.where(ge, hi, mid)

    lo, hi = lax.fori_loop(0, TOPK_SEARCH_STEPS, step,
                           (jnp.zeros((rows, 1), F32), jnp.full((rows, 1), 2.0, F32)))
    above = v >= hi
    tie = jnp.logical_and(v >= lo, jnp.logical_not(above))
    stacked = jnp.concatenate([jnp.where(above, 1.0, 0.0), jnp.where(tie, 1.0, 0.0)], axis=0)
    upper = jnp.where(lax.broadcasted_iota(jnp.int32, (S, S), 0) <= lax.broadcasted_iota(jnp.int32, (S, S), 1),
                      1.0, 0.0).astype(BF16)
    pc = jnp.dot(stacked.astype(BF16), upper, preferred_element_type=F32)
    pa = pc[:rows]
    pt = pc[rows:]
    need = kf - jnp.sum(jnp.where(above, 1.0, 0.0), axis=1, keepdims=True)
    sel = jnp.logical_or(above, jnp.logical_and(tie, pt <= need))
    taken = pa + jnp.minimum(pt, need)
    rank_ref[...] = jnp.where(sel, taken - 1.0, -1.0).astype(jnp.int32).reshape(B, E, S)
    gate_ref[...] = jnp.where(sel, v, 0.0).reshape(B, E, S)
    pick = jnp.where(lax.broadcasted_iota(jnp.int32, (S, LANES), 0) + 1
                     == lax.broadcasted_iota(jnp.int32, (S, LANES), 1) * tb, 1.0, 0.0).astype(BF16)
    off = jnp.dot(taken.astype(BF16), pick, preferred_element_type=F32)
    off_ref[...] = off.astype(jnp.int32).reshape(B, E, LANES)


def _route(lt2, *, cap, tb):
    B, E2, S = lt2.shape
    E = E2 // 2
    assert cap <= 256 and S // tb < LANES
    spec = pl.BlockSpec((B, E, S), lambda i: (0, 0, 0))
    ospec = pl.BlockSpec((B, E, LANES), lambda i: (0, 0, 0))
    return pl.pallas_call(
        functools.partial(_route_kernel, cap=cap, tb=tb),
        grid=(1,),
        in_specs=[pl.BlockSpec((B, E2, S), lambda i: (0, 0, 0))],
        out_specs=[spec, spec, ospec],
        out_shape=[jax.ShapeDtypeStruct((B, E, S), jnp.int32), jax.ShapeDtypeStruct((B, E, S), F32),
                   jax.ShapeDtypeStruct((B, E, LANES), jnp.int32)],
        compiler_params=pltpu.CompilerParams(
            dimension_semantics=("arbitrary",), vmem_limit_bytes=VMEM_LIMIT_BYTES),
        name="route",
    )(lt2)


ROUTE_TOKEN_BLOCK = 256
DISPATCH_BLOCKS_PER_STEP = 8
COMBINE_BLOCKS_PER_STEP = 8
SLOT_WINDOW = 64
SLOT_ALIGN = 16


def _slot_windows(off_ref, base, n_experts, cap, win):
    starts = []
    n_pass = jnp.int32(1)
    for e in range(n_experts):
        off = off_ref[base + e]
        end = off_ref[base + n_experts + e]
        start = jnp.minimum((off // SLOT_ALIGN) * SLOT_ALIGN, cap - win)
        starts.append(start)
        n_pass = jnp.maximum(n_pass, (end - start + (win - 1)) // win)
    return starts, n_pass


def _dispatch_kernel(off_ref, rank_ref, h_ref, xg_ref, *, cap, win, tb):
    E = rank_ref.shape[1]
    nsb = rank_ref.shape[2] // tb
    b, j = pl.program_id(0), pl.program_id(1)
    n_bounds = pl.num_programs(1) * nsb + 1

    @pl.when(j == 0)
    def _():
        xg_ref[...] = jnp.zeros(xg_ref.shape, BF16)

    sub = lax.broadcasted_iota(jnp.int32, (win, tb), 0)

    def one_pass(sb, starts, p, first):
        cols = slice(sb * tb, (sb + 1) * tb)
        pieces, rows = [], []
        for e in range(E):
            lo = starts[e] + p * win
            ws = pl.multiple_of(jnp.minimum(lo, cap - win), SLOT_ALIGN)
            tgt = ws + sub
            hit = rank_ref[0, e:e + 1, cols] == tgt
            if not first:
                hit = jnp.logical_and(hit, tgt >= lo)
            pieces.append(jnp.where(hit, 1.0, 0.0).astype(BF16))
            rows.append(ws)
        onehot = jnp.concatenate(pieces, axis=0)
        hb = h_ref[0, cols, :].astype(BF16)
        got = jnp.dot(onehot, hb, preferred_element_type=F32).astype(BF16)
        for e in range(E):
            xg_ref[e, 0, pl.ds(rows[e], win), :] += got[e * win:(e + 1) * win, :]

    plans = []
    for sb in range(nsb):
        starts, n_pass = _slot_windows(off_ref, (b * n_bounds + j * nsb + sb) * E, E, cap, win)
        plans.append((starts, n_pass))
        one_pass(sb, starts, 0, True)
    for sb in range(nsb):
        starts, n_pass = plans[sb]

        def extra(p, carry, sb=sb, starts=starts):
            one_pass(sb, starts, p, False)
            return carry

        lax.fori_loop(1, n_pass, extra, 0)


def _dispatch(off_flat, rank, h1b, *, cap, tb, nsb):
    B, E, S = rank.shape
    D = h1b.shape[-1]
    win = min(SLOT_WINDOW, cap)
    ts = tb * nsb
    assert cap % SLOT_ALIGN == 0 and win % SLOT_ALIGN == 0 and S % ts == 0
    return pl.pallas_call(
        functools.partial(_dispatch_kernel, cap=cap, win=win, tb=tb),
        grid_spec=pltpu.PrefetchScalarGridSpec(
            num_scalar_prefetch=1,
            grid=(B, S // ts),
            in_specs=[pl.BlockSpec((1, E, ts), lambda b, j, off: (b, 0, j)),
                      pl.BlockSpec((1, ts, D), lambda b, j, off: (b, j, 0))],
            out_specs=pl.BlockSpec((E, 1, cap, D), lambda b, j, off: (0, b, 0, 0)),
        ),
        out_shape=jax.ShapeDtypeStruct((E, B, cap, D), BF16),
        compiler_params=pltpu.CompilerParams(
            dimension_semantics=("arbitrary", "arbitrary"), vmem_limit_bytes=VMEM_LIMIT_BYTES),
        name="dispatch",
    )(off_flat, rank, h1b)


EXPERT_ROW_BLOCK = 1024


def _expert_kernel(x_ref, wg_ref, wu_ref, wd_ref, y_ref, acc_ref, wgb_ref, wub_ref, wdb_ref, *, n_chunks):
    f = pl.program_id(1)
    rows = x_ref.shape[1]
    rb = min(rows, EXPERT_ROW_BLOCK)
    last = n_chunks - 1

    def chunk(first, final):
        wgb_ref[...] = wg_ref[0].astype(BF16)
        wub_ref[...] = wu_ref[0].astype(BF16)
        wdb_ref[...] = wd_ref[0].astype(BF16)
        for r0 in range(0, rows, rb):
            x = x_ref[0, r0:r0 + rb, :]
            a = jnp.dot(x, wgb_ref[...], preferred_element_type=F32)
            u = jnp.dot(x, wub_ref[...], preferred_element_type=F32)
            hmid = (a * (1.0 / (1.0 + jnp.exp(-a))) * u).astype(BF16)
            part = jnp.dot(hmid, wdb_ref[...], preferred_element_type=F32)
            if not first:
                part = acc_ref[r0:r0 + rb, :] + part
            if final:
                y_ref[0, r0:r0 + rb, :] = part.astype(BF16)
            else:
                acc_ref[r0:r0 + rb, :] = part

    if n_chunks == 1:
        chunk(True, True)
    else:
        pl.when(f == 0)(lambda: chunk(True, False))
        if n_chunks > 2:
            pl.when(jnp.logical_and(f > 0, f < last))(lambda: chunk(False, False))
        pl.when(f == last)(lambda: chunk(False, True))


def _experts(xg, w_gate, w_up, w_down, *, tf):
    E, rows, D = xg.shape
    F = w_gate.shape[-1]
    return pl.pallas_call(
        functools.partial(_expert_kernel, n_chunks=F // tf),
        grid=(E, F // tf),
        in_specs=[
            pl.BlockSpec((1, rows, D), lambda e, f: (e, 0, 0)),
            pl.BlockSpec((1, D, tf), lambda e, f: (e, 0, f)),
            pl.BlockSpec((1, D, tf), lambda e, f: (e, 0, f)),
            pl.BlockSpec((1, tf, D), lambda e, f: (e, f, 0)),
        ],
        out_specs=pl.BlockSpec((1, rows, D), lambda e, f: (e, 0, 0)),
        out_shape=jax.ShapeDtypeStruct((E, rows, D), BF16),
        scratch_shapes=[pltpu.VMEM((rows, D), F32), pltpu.VMEM((D, tf), BF16), pltpu.VMEM((D, tf), BF16),
                        pltpu.VMEM((tf, D), BF16)],
        compiler_params=pltpu.CompilerParams(
            dimension_semantics=("arbitrary", "arbitrary"), vmem_limit_bytes=EXPERT_VMEM_LIMIT_BYTES),
        name="experts",
    )(xg, w_gate, w_up, w_down)


def _combine_kernel(off_ref, rank_ref, gate_ref, y_ref, h_ref, g2_ref, b2_ref, o_ref, *, cap, win, tb):
    E = rank_ref.shape[1]
    nsb = rank_ref.shape[2] // tb
    b, j = pl.program_id(0), pl.program_id(1)
    n_bounds = pl.num_programs(1) * nsb + 1
    per_tile = LANES // win
    lane = lax.broadcasted_iota(jnp.int32, (1, LANES), 1)
    first_lane = lax.broadcasted_iota(jnp.int32, (LANES, E * win), 0) * win
    col = lax.broadcasted_iota(jnp.int32, (LANES, E * win), 1)
    spread = jnp.where(jnp.logical_and(col >= first_lane, col < first_lane + win), 1.0, 0.0).astype(BF16)
    pad_rows = jnp.zeros((LANES - E, tb), F32)

    def token_major(x_et):
        return jnp.concatenate([x_et, pad_rows], axis=0).T.astype(BF16)

    def one_pass(sb, starts, p, first):
        cols = slice(sb * tb, (sb + 1) * tb)
        rk = jnp.dot(token_major(rank_ref[0, :, cols].astype(F32)), spread, preferred_element_type=F32)
        gt = jnp.dot(token_major(gate_ref[0, :, cols]), spread, preferred_element_type=F32)
        tgts, los, ys = [], [], []
        for t in range(E // per_tile):
            tgt = lo_v = None
            for u in range(per_tile):
                e = t * per_tile + u
                lo = starts[e] + p * win
                ws = pl.multiple_of(jnp.minimum(lo, cap - win), SLOT_ALIGN)
                ys.append(y_ref[e, 0, pl.ds(ws, win), :])
                t_u = ws + lane - u * win
                if u == 0:
                    tgt, lo_v = t_u, jnp.full((1, LANES), lo, jnp.int32)
                else:
                    here = lane >= u * win
                    tgt = jnp.where(here, t_u, tgt)
                    lo_v = jnp.where(here, lo, lo_v)
            tgts.append(tgt)
            los.append(lo_v)
        tgt = jnp.concatenate(tgts, axis=1)
        hit = rk == tgt.astype(F32)
        if not first:
            hit = jnp.logical_and(hit, tgt >= jnp.concatenate(los, axis=1))
        gates = jnp.where(hit, gt, 0.0).astype(BF16)
        return jnp.dot(gates, jnp.concatenate(ys, axis=0), preferred_element_type=F32)

    plans = []
    for sb in range(nsb):
        rows = slice(sb * tb, (sb + 1) * tb)
        starts, n_pass = _slot_windows(off_ref, (b * n_bounds + j * nsb + sb) * E, E, cap, win)
        plans.append((starts, n_pass))
        o_ref[0, rows, :] = DEEPNORM_ALPHA * h_ref[0, rows, :] + one_pass(sb, starts, 0, True)
    for sb in range(nsb):
        starts, n_pass = plans[sb]

        def extra(p, carry, sb=sb, starts=starts):
            o_ref[0, sb * tb:(sb + 1) * tb, :] += one_pass(sb, starts, p, False)
            return carry

        lax.fori_loop(1, n_pass, extra, 0)
    o_ref[0] = _layer_norm(o_ref[0], g2_ref[...], b2_ref[...])


def _combine(off_flat, rank, gate, y, h1, g2, b2, *, tb, nsb):
    B, E, S = rank.shape
    cap, D = y.shape[2], y.shape[3]
    win = min(SLOT_WINDOW, cap)
    ts = tb * nsb
    assert cap % SLOT_ALIGN == 0 and win % SLOT_ALIGN == 0 and LANES % win == 0 and E % (LANES // win) == 0
    assert S % ts == 0
    full = lambda a: pl.BlockSpec(a.shape, lambda b, j, off: (0,) * a.ndim)
    return pl.pallas_call(
        functools.partial(_combine_kernel, cap=cap, win=win, tb=tb),
        grid_spec=pltpu.PrefetchScalarGridSpec(
            num_scalar_prefetch=1,
            grid=(B, S // ts),
            in_specs=[
                pl.BlockSpec((1, E, ts), lambda b, j, off: (b, 0, j)),
                pl.BlockSpec((1, E, ts), lambda b, j, off: (b, 0, j)),
                pl.BlockSpec((E, 1, cap, D), lambda b, j, off: (0, b, 0, 0)),
                pl.BlockSpec((1, ts, D), lambda b, j, off: (b, j, 0)),
                full(g2), full(b2),
            ],
            out_specs=pl.BlockSpec((1, ts, D), lambda b, j, off: (b, j, 0)),
        ),
        out_shape=jax.ShapeDtypeStruct((B, S, D), F32),
        compiler_params=pltpu.CompilerParams(
            dimension_semantics=("arbitrary", "arbitrary"), vmem_limit_bytes=VMEM_LIMIT_BYTES),
        name="combine",
    )(off_flat, rank, gate, y, h1, g2, b2)


def _tile(n, target):
    t = min(n, target)
    assert n % t == 0, (n, t)
    return t


def kernel(x, positions, emb_ln_g, emb_ln_b, w_in, q_norm_g, w_qb, kv_norm_g, w_kvb, conv_w, conv_b,
           conv_ln_g, conv_ln_b, w_o, ln1_g, ln1_b, w_router, w_gate, w_up, w_down, ln2_g, ln2_b):
    B, S, D = x.shape
    T = B * S
    H = MLA_HEADS
    q_rank = q_norm_g.shape[-1]
    kv_rank = kv_norm_g.shape[-1]
    conv_ch = conv_w.shape[-1]
    qk_dim = QK_NOPE_DIM + QK_ROPE_DIM
    cap = CAPACITY_FACTOR * S // N_EXPERTS
    assert w_in.shape[0] == DEPTH == 1
    row = lambda a: a.reshape(1, -1)

    wi = w_in[0]
    c1, c2, c3 = q_rank, q_rank + kv_rank, q_rank + kv_rank + QK_ROPE_DIM
    tail = LANES - QK_NOPE_DIM - QK_ROPE_DIM
    kr_cols = jnp.pad(wi[:, c2:c3], ((0, 0), (QK_NOPE_DIM, tail)))
    w_in_r = jnp.concatenate(
        [wi[:, :c2], wi[:, c3:c3 + conv_ch], wi[:, c3 + conv_ch:], kr_cols], axis=1).astype(BF16)
    w_qb_pad = jnp.pad(w_qb[0].reshape(q_rank, H, qk_dim),
                       ((0, 0), (0, 0), (0, HEAD_PAD - qk_dim))).reshape(q_rank, H * HEAD_PAD).astype(BF16)
    wkv = w_kvb[0].reshape(kv_rank, H, QK_NOPE_DIM + V_HEAD_DIM)
    wk_pad = jnp.pad(wkv[:, :, :QK_NOPE_DIM], ((0, 0), (0, 0), (0, HEAD_PAD - QK_NOPE_DIM)))
    w_kv_r = jnp.concatenate([wk_pad.reshape(kv_rank, H * HEAD_PAD),
                              wkv[:, :, QK_NOPE_DIM:].reshape(kv_rank, H * V_HEAD_DIM)], axis=1).astype(BF16)
    half = QK_ROPE_DIM // 2
    inv_freq = (ROPE_THETA ** (-jnp.arange(half, dtype=F32) / half)).reshape(half, 1)

    x2d = x.reshape(T, D)
    q, k, v, hc = _in_proj(x2d, positions.reshape(1, T), inv_freq, row(emb_ln_g), row(emb_ln_b), w_in_r,
                           row(q_norm_g[0]), w_qb_pad, row(kv_norm_g[0]), w_kv_r, tm=_tile(S, IN_PROJ_TOKEN_TILE))
    n_ct = conv_ch // LANES
    cv = _conv(hc.reshape(B, S, conv_ch), jnp.swapaxes(conv_w[0].reshape(CONV_WIDTH, n_ct, LANES), 0, 1),
               conv_b[0].reshape(n_ct, 1, LANES), row(conv_ln_g[0]), row(conv_ln_b[0]))
    att = _attn(q.reshape(B, S, H * HEAD_PAD), k.reshape(B, S, H * HEAD_PAD),
                v.reshape(B, S, H * V_HEAD_DIM), tq=_tile(S, ATTN_QUERY_TILE))
    wr_hi = w_router[0].astype(BF16)
    w_r2 = jnp.concatenate([wr_hi, (w_router[0] - wr_hi.astype(F32)).astype(BF16)], axis=1)
    w_r2 = jnp.pad(w_r2, ((0, 0), (0, LANES - 2 * N_EXPERTS)))
    h1, lt2 = _out_proj(x2d, att.reshape(T, H * V_HEAD_DIM), cv.reshape(T, conv_ch), row(emb_ln_g),
                        row(emb_ln_b), w_o[0].astype(BF16), row(ln1_g[0]), row(ln1_b[0]),
                        w_r2, tm=_tile(S, OUT_PROJ_TOKEN_TILE), seq=S, n_lt=2 * N_EXPERTS)
    h1 = h1.reshape(B, S, D)
    tb = _tile(S, ROUTE_TOKEN_BLOCK)
    rank, gate, off = _route(lt2, cap=cap, tb=tb)
    off_flat = jnp.swapaxes(off[:, :, :S // tb + 1], 1, 2).reshape(-1)
    xg = _dispatch(off_flat, rank, h1, cap=cap, tb=tb, nsb=_tile(S // tb, DISPATCH_BLOCKS_PER_STEP))
    y = _experts(xg.reshape(N_EXPERTS, B * cap, D), w_gate[0], w_up[0], w_down[0],
                 tf=_tile(w_gate.shape[-1], EXPERT_FF_CHUNK)).reshape(N_EXPERTS, B, cap, D)
    return _combine(off_flat, rank, gate, y, h1, row(ln2_g[0]), row(ln2_b[0]), tb=tb,
                    nsb=_tile(S // tb, COMBINE_BLOCKS_PER_STEP))
```

```python
import functools
import math

import jax
import jax.numpy as jnp
from jax import lax
from jax.experimental import pallas as pl
from jax.experimental.pallas import tpu as pltpu

F32 = jnp.float32
BF16 = jnp.bfloat16

MLA_HEADS = 8
QK_NOPE_DIM = 64
QK_ROPE_DIM = 32
V_HEAD_DIM = 64
CONV_WIDTH = 31
ROPE_THETA = 10000.0
N_EXPERTS = 16
CAPACITY_FACTOR = 2
DEPTH = 1
DEEPNORM_ALPHA = (2.0 * DEPTH) ** 0.25
LN_EPS = 1e-5
RMS_EPS = 1e-6
LOG2_E = math.log2(math.e)

LANES = 128
HEAD_PAD = LANES
VMEM_LIMIT_BYTES = 56 * 1024 * 1024
TOKEN_SUB_TILE = 256

IN_PROJ_TOKEN_TILE = 512
OUT_PROJ_TOKEN_TILE = 2048
ATTN_QUERY_TILE = 512
EXPERT_FF_CHUNK = 1024
EXPERT_VMEM_LIMIT_BYTES = 63 * 1024 * 1024

TOPK_SEARCH_STEPS = 36
TOPK_SEARCH_FLOOR = 1e-30


def _layer_norm(x, g, b):
    mu = jnp.mean(x, axis=-1, keepdims=True)
    xc = x - mu
    var = jnp.mean(xc * xc, axis=-1, keepdims=True)
    return xc * lax.rsqrt(var + LN_EPS) * g + b


def _rms_norm(x, g):
    return x * lax.rsqrt(jnp.mean(x * x, axis=-1, keepdims=True) + RMS_EPS) * g


def _rope_tile(x, cos_t, sin_t, lane):
    half = QK_ROPE_DIM // 2
    fwd = pltpu.roll(x, LANES - half, 1)
    bwd = pltpu.roll(x, half, 1)
    partner = jnp.where(lane < QK_NOPE_DIM + half, fwd, bwd)
    return x * cos_t + partner * sin_t


def _in_proj_kernel(x_ref, pos_ref, invf_ref, lng_ref, lnb_ref, win_ref, qg_ref, wqb_ref,
                    kvg_ref, wkv_ref, q_ref, k_ref, v_ref, hc_ref, *, q_rank, kv_rank, conv_ch):
    tm = x_ref.shape[0]
    sub = min(tm, TOKEN_SUB_TILE)
    c1 = q_rank
    c2 = c1 + kv_rank
    c3 = c2 + conv_ch
    c4 = c3 + conv_ch
    tail = LANES - QK_NOPE_DIM - QK_ROPE_DIM
    nk = MLA_HEADS * HEAD_PAD
    scale = (QK_NOPE_DIM + QK_ROPE_DIM) ** -0.5 * LOG2_E
    ones = jnp.ones((QK_NOPE_DIM, sub), F32)
    zeros = jnp.zeros((QK_NOPE_DIM, sub), F32)
    lane = lax.broadcasted_iota(jnp.int32, (sub, LANES), 1)
    for r0 in range(0, tm, sub):
        rows = slice(r0, r0 + sub)
        h = _layer_norm(x_ref[rows, :], lng_ref[...], lnb_ref[...])
        proj = jnp.dot(h.astype(BF16), win_ref[...], preferred_element_type=F32)
        cq = proj[:, :c1]
        ckv = proj[:, c1:c2]
        a = proj[:, c2:c3]
        g = proj[:, c3:c4]
        kr = proj[:, c4:c4 + LANES]

        ang = pos_ref[:, rows].astype(F32) * invf_ref[...]
        cos = jnp.cos(ang)
        sin = jnp.sin(ang)
        cos_t = jnp.concatenate([ones, cos, cos, ones[:tail]], axis=0).T
        sin_t = jnp.concatenate([zeros, -sin, sin, zeros[:tail]], axis=0).T

        cqn = _rms_norm(cq, qg_ref[...])
        q = jnp.dot(cqn.astype(BF16), wqb_ref[...], preferred_element_type=F32) * scale
        ckvn = _rms_norm(ckv, kvg_ref[...])
        kv = jnp.dot(ckvn.astype(BF16), wkv_ref[...], preferred_element_type=F32)
        k_pe = _rope_tile(kr, cos_t, sin_t, lane)
        for hd in range(MLA_HEADS):
            sl = slice(hd * HEAD_PAD, (hd + 1) * HEAD_PAD)
            q_ref[rows, sl] = _rope_tile(q[:, sl], cos_t, sin_t, lane).astype(BF16)
            k_ref[rows, sl] = (kv[:, sl] + k_pe).astype(BF16)
        v_ref[rows, :] = kv[:, nk:].astype(BF16)
        hc_ref[rows, :] = a * (1.0 / (1.0 + jnp.exp(-g)))


def _in_proj(x2d, pos_row, invf_col, lng, lnb, w_in_r, qg, w_qb_pad, kvg, w_kv_r, *, tm):
    T, D = x2d.shape
    q_rank = w_qb_pad.shape[0]
    kv_rank = w_kv_r.shape[0]
    conv_ch = (w_in_r.shape[1] - q_rank - kv_rank - LANES) // 2
    nk = MLA_HEADS * HEAD_PAD
    nv = MLA_HEADS * V_HEAD_DIM
    full = lambda a: pl.BlockSpec(a.shape, lambda i: (0,) * a.ndim)
    return pl.pallas_call(
        functools.partial(_in_proj_kernel, q_rank=q_rank, kv_rank=kv_rank, conv_ch=conv_ch),
        grid=(T // tm,),
        in_specs=[
            pl.BlockSpec((tm, D), lambda i: (i, 0)),
            pl.BlockSpec((1, tm), lambda i: (0, i)),
            full(invf_col), full(lng), full(lnb), full(w_in_r), full(qg), full(w_qb_pad),
            full(kvg), full(w_kv_r),
        ],
        out_specs=[
            pl.BlockSpec((tm, nk), lambda i: (i, 0)),
            pl.BlockSpec((tm, nk), lambda i: (i, 0)),
            pl.BlockSpec((tm, nv), lambda i: (i, 0)),
            pl.BlockSpec((tm, conv_ch), lambda i: (i, 0)),
        ],
        out_shape=[
            jax.ShapeDtypeStruct((T, nk), BF16),
            jax.ShapeDtypeStruct((T, nk), BF16),
            jax.ShapeDtypeStruct((T, nv), BF16),
            jax.ShapeDtypeStruct((T, conv_ch), F32),
        ],
        compiler_params=pltpu.CompilerParams(
            dimension_semantics=("arbitrary",), vmem_limit_bytes=VMEM_LIMIT_BYTES),
        name="in_proj",
    )(x2d, pos_row, invf_col, lng, lnb, w_in_r, qg, w_qb_pad, kvg, w_kv_r)


CONV_PAD_ROWS = 16
CONV_ROW_CHUNK = 128
CONV_CHUNKS_PER_STEP = 4
CONV_WINDOW = CONV_ROW_CHUNK + 2 * CONV_PAD_ROWS


def _conv_kernel(hc_ref, cw_ref, cb_ref, g_ref, b_ref, o_ref, win_ref, y_ref):
    S, C = hc_ref.shape[1], hc_ref.shape[2]
    n_chunks = S // CONV_ROW_CHUNK
    for r in range(n_chunks):
        lo = r * CONV_ROW_CHUNK - CONV_PAD_ROWS
        hi = lo + CONV_WINDOW
        src_lo, src_hi = max(lo, 0), min(hi, S)
        for c in range(C // LANES):
            if lo < 0:
                win_ref[r, c, 0:-lo, :] = jnp.zeros((-lo, LANES), F32)
            if hi > S:
                win_ref[r, c, CONV_WINDOW - (hi - S):, :] = jnp.zeros((hi - S, LANES), F32)
            win_ref[r, c, src_lo - lo:src_hi - lo, :] = hc_ref[0, src_lo:src_hi, c * LANES:(c + 1) * LANES]
    first = CONV_PAD_ROWS - CONV_WIDTH // 2

    n_tiles = C // LANES
    group = min(CONV_CHUNKS_PER_STEP, n_chunks)

    def chunk_group(rg, carry):
        sums = []
        for u in range(group):
            r = rg * group + u

            def tile_conv(c, s1, r=r, u=u):
                acc = jnp.zeros((CONV_ROW_CHUNK, LANES), F32)
                for j in range(CONV_WIDTH):
                    acc = acc + win_ref[r, c, first + j:first + j + CONV_ROW_CHUNK, :] * cw_ref[c, j:j + 1, :]
                acc = acc + cb_ref[c]
                y_ref[u, c] = acc
                return s1 + acc

            sums.append(lax.fori_loop(0, n_tiles, tile_conv, jnp.zeros((CONV_ROW_CHUNK, LANES), F32)))
        for u in range(group):
            base = pl.multiple_of((rg * group + u) * CONV_ROW_CHUNK, CONV_ROW_CHUNK)
            mu = jnp.sum(sums[u], axis=-1, keepdims=True) * (1.0 / C)
            s2 = jnp.zeros((CONV_ROW_CHUNK, LANES), F32)
            for c in range(n_tiles):
                yc = y_ref[u, c] - mu
                s2 = s2 + yc * yc
            inv = lax.rsqrt(jnp.sum(s2, axis=-1, keepdims=True) * (1.0 / C) + LN_EPS)
            for c in range(n_tiles):
                lanes = slice(c * LANES, (c + 1) * LANES)
                y = (y_ref[u, c] - mu) * inv * g_ref[:, lanes] + b_ref[:, lanes]
                o_ref[0, pl.ds(base, CONV_ROW_CHUNK), lanes] = (y * (1.0 / (1.0 + jnp.exp(-y)))).astype(BF16)
        return carry

    lax.fori_loop(0, n_chunks // group, chunk_group, 0)


def _conv(hc, cw, cb, g, b):
    B, S, C = hc.shape
    full = lambda a: pl.BlockSpec(a.shape, lambda i: (0,) * a.ndim)
    return pl.pallas_call(
        _conv_kernel,
        grid=(B,),
        in_specs=[pl.BlockSpec((1, S, C), lambda i: (i, 0, 0)), full(cw), full(cb), full(g), full(b)],
        out_specs=pl.BlockSpec((1, S, C), lambda i: (i, 0, 0)),
        out_shape=jax.ShapeDtypeStruct((B, S, C), BF16),
        scratch_shapes=[pltpu.VMEM((S // CONV_ROW_CHUNK, C // LANES, CONV_WINDOW, LANES), F32),
                        pltpu.VMEM((CONV_CHUNKS_PER_STEP, C // LANES, CONV_ROW_CHUNK, LANES), F32)],
        compiler_params=pltpu.CompilerParams(
            dimension_semantics=("arbitrary",), vmem_limit_bytes=VMEM_LIMIT_BYTES),
        name="conv",
    )(hc, cw, cb, g, b)


HEADS_PER_TILE = LANES // V_HEAD_DIM
ATTN_HEADS_PER_STEP = 4


ATTN_SLOTS = 2


def _attn_kernel(q_ref, k_ref, v_ref, o_ref, s_ref, p_ref, *, tq):
    S = q_ref.shape[1]
    assert HEADS_PER_TILE == 2
    lane = lax.broadcasted_iota(jnp.int32, (tq, LANES), 1)
    key_lane = lax.broadcasted_iota(jnp.int32, (S, LANES), 1)
    item = 0
    for pr in range(v_ref.shape[2] // LANES):
        v = v_ref[0, :, pr * LANES:(pr + 1) * LANES]
        one = jnp.ones((S, LANES), BF16)
        v_ones = [jnp.where(key_lane < V_HEAD_DIM, v, one), jnp.where(key_lane < V_HEAD_DIM, one, v)]
        for qi in range(S // tq):
            rows = slice(qi * tq, (qi + 1) * tq)
            outs = []
            for hh in range(HEADS_PER_TILE):
                slot = item % ATTN_SLOTS
                item += 1
                c0 = (pr * HEADS_PER_TILE + hh) * HEAD_PAD
                s_ref[slot] = lax.dot_general(q_ref[0, rows, c0:c0 + HEAD_PAD], k_ref[0, :, c0:c0 + HEAD_PAD],
                                              (((1,), (1,)), ((), ())), preferred_element_type=F32)
                s = s_ref[slot]
                p_ref[slot] = jnp.exp2(s - jnp.max(s, axis=-1, keepdims=True)).astype(BF16)
                o = jnp.dot(p_ref[slot], v_ones[hh], preferred_element_type=F32)
                outs.append(o * pltpu.roll(1.0 / o, V_HEAD_DIM, 1))
            o_ref[0, rows, pr * LANES:(pr + 1) * LANES] = jnp.where(lane < V_HEAD_DIM, outs[0], outs[1]).astype(BF16)


def _attn(q, k, v, *, tq):
    B, S, _ = q.shape
    qw = ATTN_HEADS_PER_STEP * HEAD_PAD
    vw = ATTN_HEADS_PER_STEP * V_HEAD_DIM
    return pl.pallas_call(
        functools.partial(_attn_kernel, tq=tq),
        grid=(B, MLA_HEADS // ATTN_HEADS_PER_STEP),
        in_specs=[
            pl.BlockSpec((1, S, qw), lambda b, h: (b, 0, h)),
            pl.BlockSpec((1, S, qw), lambda b, h: (b, 0, h)),
            pl.BlockSpec((1, S, vw), lambda b, h: (b, 0, h)),
        ],
        out_specs=pl.BlockSpec((1, S, vw), lambda b, h: (b, 0, h)),
        out_shape=jax.ShapeDtypeStruct((B, S, MLA_HEADS * V_HEAD_DIM), BF16),
        scratch_shapes=[pltpu.VMEM((ATTN_SLOTS, tq, S), F32), pltpu.VMEM((ATTN_SLOTS, tq, S), BF16)],
        compiler_params=pltpu.CompilerParams(
            dimension_semantics=("arbitrary", "arbitrary"), vmem_limit_bytes=VMEM_LIMIT_BYTES),
        name="attn",
    )(q, k, v)


def _out_proj_kernel(x_ref, att_ref, cv_ref, lng_ref, lnb_ref, wo_ref, g1_ref, b1_ref, wr_ref,
                     h1_ref, lt_ref, wob_ref):
    na = att_ref.shape[1]
    tm = x_ref.shape[0]
    n_lt = lt_ref.shape[1]
    sub = min(tm, TOKEN_SUB_TILE)
    wob_ref[...] = wo_ref[...].astype(BF16)
    for r0 in range(0, tm, sub):
        rows = slice(r0, r0 + sub)
        h0 = _layer_norm(x_ref[rows, :], lng_ref[...], lnb_ref[...])
        mix = jnp.dot(att_ref[rows, :], wob_ref[:na, :], preferred_element_type=F32)
        mix = mix + jnp.dot(cv_ref[rows, :], wob_ref[na:, :], preferred_element_type=F32)
        h1 = _layer_norm(DEEPNORM_ALPHA * h0 + mix, g1_ref[...], b1_ref[...])
        h1_ref[rows, :] = h1
        lg = jnp.dot(h1.astype(BF16), wr_ref[...], preferred_element_type=F32)
        lt_ref[0, :, rows] = lg.T[:n_lt, :]


def _out_proj(x2d, att, cv, lng, lnb, w_o, g1, b1, w_r2, *, tm, seq, n_lt):
    T, D = x2d.shape
    per = seq // tm
    full = lambda a: pl.BlockSpec(a.shape, lambda i: (0,) * a.ndim)
    return pl.pallas_call(
        _out_proj_kernel,
        grid=(T // tm,),
        in_specs=[
            pl.BlockSpec((tm, D), lambda i: (i, 0)),
            pl.BlockSpec((tm, att.shape[1]), lambda i: (i, 0)),
            pl.BlockSpec((tm, cv.shape[1]), lambda i: (i, 0)),
            full(lng), full(lnb), full(w_o), full(g1), full(b1), full(w_r2),
        ],
        out_specs=[
            pl.BlockSpec((tm, D), lambda i: (i, 0)),
            pl.BlockSpec((1, n_lt, tm), lambda i: (i // per, 0, i % per)),
        ],
        out_shape=[
            jax.ShapeDtypeStruct((T, D), F32),
            jax.ShapeDtypeStruct((T // seq, n_lt, seq), F32),
        ],
        scratch_shapes=[pltpu.VMEM(w_o.shape, BF16)],
        compiler_params=pltpu.CompilerParams(
            dimension_semantics=("arbitrary",), vmem_limit_bytes=VMEM_LIMIT_BYTES),
        name="out_proj",
    )(x2d, att, cv, lng, lnb, w_o, g1, b1, w_r2)


def _route_kernel(lt_ref, rank_ref, gate_ref, off_ref, *, cap, tb):
    B, E, S = rank_ref.shape
    lt = lt_ref[:, :E, :] + lt_ref[:, E:, :]
    ex = jnp.exp(lt - jnp.max(lt, axis=1, keepdims=True))
    aff = ex / jnp.sum(ex, axis=1, keepdims=True)
    v = aff.reshape(B * E, S)
    rows = B * E
    kf = float(cap)

    def step(_, carry):
        lo, hi = carry
        mid = jnp.sqrt(jnp.maximum(lo, TOPK_SEARCH_FLOOR)) * jnp.sqrt(hi)
        cnt = jnp.sum(jnp.where(v >= mid, 1.0, 0.0), axis=1, keepdims=True)
        ge = cnt >= kf
        return jnp.where(ge, mid, lo), jnp.where(ge, hi, mid)

    lo, hi = lax.fori_loop(0, TOPK_SEARCH_STEPS, step,
                           (jnp.zeros((rows, 1), F32), jnp.full((rows, 1), 2.0, F32)))
    above = v >= hi
    tie = jnp.logical_and(v >= lo, jnp.logical_not(above))
    stacked = jnp.concatenate([jnp.where(above, 1.0, 0.0), jnp.where(tie, 1.0, 0.0)], axis=0)
    upper = jnp.where(lax.broadcasted_iota(jnp.int32, (S, S), 0) <= lax.broadcasted_iota(jnp.int32, (S, S), 1),
                      1.0, 0.0).astype(BF16)
    pc = jnp.dot(stacked.astype(BF16), upper, preferred_element_type=F32)
    pa = pc[:rows]
    pt = pc[rows:]
    need = kf - jnp.sum(jnp.where(above, 1.0, 0.0), axis=1, keepdims=True)
    sel = jnp.logical_or(above, jnp.logical_and(tie, pt <= need))
    taken = pa + jnp.minimum(pt, need)
    rank_ref[...] = jnp.where(sel, taken - 1.0, -1.0).astype(jnp.int32).reshape(B, E, S)
    gate_ref[...] = jnp.where(sel, v, 0.0).reshape(B, E, S)
    pick = jnp.where(lax.broadcasted_iota(jnp.int32, (S, LANES), 0) + 1
                     == lax.broadcasted_iota(jnp.int32, (S, LANES), 1) * tb, 1.0, 0.0).astype(BF16)
    off = jnp.dot(taken.astype(BF16), pick, preferred_element_type=F32)
    off_ref[...] = off.astype(jnp.int32).reshape(B, E, LANES)


def _route(lt2, *, cap, tb):
    B, E2, S = lt2.shape
    E = E2 // 2
    assert cap <= 256 and S // tb < LANES
    spec = pl.BlockSpec((B, E, S), lambda i: (0, 0, 0))
    ospec = pl.BlockSpec((B, E, LANES), lambda i: (0, 0, 0))
    return pl.pallas_call(
        functools.partial(_route_kernel, cap=cap, tb=tb),
        grid=(1,),
        in_specs=[pl.BlockSpec((B, E2, S), lambda i: (0, 0, 0))],
        out_specs=[spec, spec, ospec],
        out_shape=[jax.ShapeDtypeStruct((B, E, S), jnp.int32), jax.ShapeDtypeStruct((B, E, S), F32),
                   jax.ShapeDtypeStruct((B, E, LANES), jnp.int32)],
        compiler_params=pltpu.CompilerParams(
            dimension_semantics=("arbitrary",), vmem_limit_bytes=VMEM_LIMIT_BYTES),
        name="route",
    )(lt2)


ROUTE_TOKEN_BLOCK = 256
DISPATCH_BLOCKS_PER_STEP = 8
COMBINE_BLOCKS_PER_STEP = 8
SLOT_WINDOW = 64
SLOT_ALIGN = 16


def _slot_windows(off_ref, base, n_experts, cap, win):
    starts = []
    n_pass = jnp.int32(1)
    for e in range(n_experts):
        off = off_ref[base + e]
        end = off_ref[base + n_experts + e]
        start = jnp.minimum((off // SLOT_ALIGN) * SLOT_ALIGN, cap - win)
        starts.append(start)
        n_pass = jnp.maximum(n_pass, (end - start + (win - 1)) // win)
    return starts, n_pass


def _dispatch_kernel(off_ref, rank_ref, h_ref, xg_ref, *, cap, win, tb):
    E = rank_ref.shape[1]
    nsb = rank_ref.shape[2] // tb
    b, j = pl.program_id(0), pl.program_id(1)
    n_bounds = pl.num_programs(1) * nsb + 1

    @pl.when(j == 0)
    def _():
        xg_ref[...] = jnp.zeros(xg_ref.shape, BF16)

    sub = lax.broadcasted_iota(jnp.int32, (win, tb), 0)

    def one_pass(sb, starts, p, first):
        cols = slice(sb * tb, (sb + 1) * tb)
        pieces, rows = [], []
        for e in range(E):
            lo = starts[e] + p * win
            ws = pl.multiple_of(jnp.minimum(lo, cap - win), SLOT_ALIGN)
            tgt = ws + sub
            hit = rank_ref[0, e:e + 1, cols] == tgt
            if not first:
                hit = jnp.logical_and(hit, tgt >= lo)
            pieces.append(jnp.where(hit, 1.0, 0.0).astype(BF16))
            rows.append(ws)
        onehot = jnp.concatenate(pieces, axis=0)
        hb = h_ref[0, cols, :].astype(BF16)
        got = jnp.dot(onehot, hb, preferred_element_type=F32).astype(BF16)
        for e in range(E):
            xg_ref[e, 0, pl.ds(rows[e], win), :] += got[e * win:(e + 1) * win, :]

    plans = []
    for sb in range(nsb):
        starts, n_pass = _slot_windows(off_ref, (b * n_bounds + j * nsb + sb) * E, E, cap, win)
        plans.append((starts, n_pass))
        one_pass(sb, starts, 0, True)
    for sb in range(nsb):
        starts, n_pass = plans[sb]

        def extra(p, carry, sb=sb, starts=starts):
            one_pass(sb, starts, p, False)
            return carry

        lax.fori_loop(1, n_pass, extra, 0)


def _dispatch(off_flat, rank, h1b, *, cap, tb, nsb):
    B, E, S = rank.shape
    D = h1b.shape[-1]
    win = min(SLOT_WINDOW, cap)
    ts = tb * nsb
    assert cap % SLOT_ALIGN == 0 and win % SLOT_ALIGN == 0 and S % ts == 0
    return pl.pallas_call(
        functools.partial(_dispatch_kernel, cap=cap, win=win, tb=tb),
        grid_spec=pltpu.PrefetchScalarGridSpec(
            num_scalar_prefetch=1,
            grid=(B, S // ts),
            in_specs=[pl.BlockSpec((1, E, ts), lambda b, j, off: (b, 0, j)),
                      pl.BlockSpec((1, ts, D), lambda b, j, off: (b, j, 0))],
            out_specs=pl.BlockSpec((E, 1, cap, D), lambda b, j, off: (0, b, 0, 0)),
        ),
        out_shape=jax.ShapeDtypeStruct((E, B, cap, D), BF16),
        compiler_params=pltpu.CompilerParams(
            dimension_semantics=("arbitrary", "arbitrary"), vmem_limit_bytes=VMEM_LIMIT_BYTES),
        name="dispatch",
    )(off_flat, rank, h1b)


EXPERT_ROW_BLOCK = 1024


def _expert_kernel(x_ref, wg_ref, wu_ref, wd_ref, y_ref, acc_ref, wgb_ref, wub_ref, wdb_ref, *, n_chunks):
    f = pl.program_id(1)
    rows = x_ref.shape[1]
    rb = min(rows, EXPERT_ROW_BLOCK)
    last = n_chunks - 1

    def chunk(first, final):
        wgb_ref[...] = wg_ref[0].astype(BF16)
        wub_ref[...] = wu_ref[0].astype(BF16)
        wdb_ref[...] = wd_ref[0].astype(BF16)
        for r0 in range(0, rows, rb):
            x = x_ref[0, r0:r0 + rb, :]
            a = jnp.dot(x, wgb_ref[...], preferred_element_type=F32)
            u = jnp.dot(x, wub_ref[...], preferred_element_type=F32)
            hmid = (a * (1.0 / (1.0 + jnp.exp(-a))) * u).astype(BF16)
            part = jnp.dot(hmid, wdb_ref[...], preferred_element_type=F32)
            if not first:
                part = acc_ref[r0:r0 + rb, :] + part
            if final:
                y_ref[0, r0:r0 + rb, :] = part.astype(BF16)
            else:
                acc_ref[r0:r0 + rb, :] = part

    if n_chunks == 1:
        chunk(True, True)
    else:
        pl.when(f == 0)(lambda: chunk(True, False))
        if n_chunks > 2:
            pl.when(jnp.logical_and(f > 0, f < last))(lambda: chunk(False, False))
        pl.when(f == last)(lambda: chunk(False, True))


def _experts(xg, w_gate, w_up, w_down, *, tf):
    E, rows, D = xg.shape
    F = w_gate.shape[-1]
    return pl.pallas_call(
        functools.partial(_expert_kernel, n_chunks=F // tf),
        grid=(E, F // tf),
        in_specs=[
            pl.BlockSpec((1, rows, D), lambda e, f: (e, 0, 0)),
            pl.BlockSpec((1, D, tf), lambda e, f: (e, 0, f)),
            pl.BlockSpec((1, D, tf), lambda e, f: (e, 0, f)),
            pl.BlockSpec((1, tf, D), lambda e, f: (e, f, 0)),
        ],
        out_specs=pl.BlockSpec((1, rows, D), lambda e, f: (e, 0, 0)),
        out_shape=jax.ShapeDtypeStruct((E, rows, D), BF16),
        scratch_shapes=[pltpu.VMEM((rows, D), F32), pltpu.VMEM((D, tf), BF16), pltpu.VMEM((D, tf), BF16),
                        pltpu.VMEM((tf, D), BF16)],
        compiler_params=pltpu.CompilerParams(
            dimension_semantics=("arbitrary", "arbitrary"), vmem_limit_bytes=EXPERT_VMEM_LIMIT_BYTES),
        name="experts",
    )(xg, w_gate, w_up, w_down)


def _combine_kernel(off_ref, rank_ref, gate_ref, y_ref, h_ref, g2_ref, b2_ref, o_ref, *, cap, win, tb):
    E = rank_ref.shape[1]
    nsb = rank_ref.shape[2] // tb
    b, j = pl.program_id(0), pl.program_id(1)
    n_bounds = pl.num_programs(1) * nsb + 1
    per_tile = LANES // win
    lane = lax.broadcasted_iota(jnp.int32, (1, LANES), 1)
    first_lane = lax.broadcasted_iota(jnp.int32, (LANES, E * win), 0) * win
    col = lax.broadcasted_iota(jnp.int32, (LANES, E * win), 1)
    spread = jnp.where(jnp.logical_and(col >= first_lane, col < first_lane + win), 1.0, 0.0).astype(BF16)
    pad_rows = jnp.zeros((LANES - E, tb), F32)

    def token_major(x_et):
        return jnp.concatenate([x_et, pad_rows], axis=0).T.astype(BF16)

    def one_pass(sb, starts, p, first):
        cols = slice(sb * tb, (sb + 1) * tb)
        rk = jnp.dot(token_major(rank_ref[0, :, cols].astype(F32)), spread, preferred_element_type=F32)
        gt = jnp.dot(token_major(gate_ref[0, :, cols]), spread, preferred_element_type=F32)
        tgts, los, ys = [], [], []
        for t in range(E // per_tile):
            tgt = lo_v = None
            for u in range(per_tile):
                e = t * per_tile + u
                lo = starts[e] + p * win
                ws = pl.multiple_of(jnp.minimum(lo, cap - win), SLOT_ALIGN)
                ys.append(y_ref[e, 0, pl.ds(ws, win), :])
                t_u = ws + lane - u * win
                if u == 0:
                    tgt, lo_v = t_u, jnp.full((1, LANES), lo, jnp.int32)
                else:
                    here = lane >= u * win
                    tgt = jnp.where(here, t_u, tgt)
                    lo_v = jnp.where(here, lo, lo_v)
            tgts.append(tgt)
            los.append(lo_v)
        tgt = jnp.concatenate(tgts, axis=1)
        hit = rk == tgt.astype(F32)
        if not first:
            hit = jnp.logical_and(hit, tgt >= jnp.concatenate(los, axis=1))
        gates = jnp.where(hit, gt, 0.0).astype(BF16)
        return jnp.dot(gates, jnp.concatenate(ys, axis=0), preferred_element_type=F32)

    plans = []
    for sb in range(nsb):
        rows = slice(sb * tb, (sb + 1) * tb)
        starts, n_pass = _slot_windows(off_ref, (b * n_bounds + j * nsb + sb) * E, E, cap, win)
        plans.append((starts, n_pass))
        o_ref[0, rows, :] = DEEPNORM_ALPHA * h_ref[0, rows, :] + one_pass(sb, starts, 0, True)
    for sb in range(nsb):
        starts, n_pass = plans[sb]

        def extra(p, carry, sb=sb, starts=starts):
            o_ref[0, sb * tb:(sb + 1) * tb, :] += one_pass(sb, starts, p, False)
            return carry

        lax.fori_loop(1, n_pass, extra, 0)
    o_ref[0] = _layer_norm(o_ref[0], g2_ref[...], b2_ref[...])


def _combine(off_flat, rank, gate, y, h1, g2, b2, *, tb, nsb):
    B, E, S = rank.shape
    cap, D = y.shape[2], y.shape[3]
    win = min(SLOT_WINDOW, cap)
    ts = tb * nsb
    assert cap % SLOT_ALIGN == 0 and win % SLOT_ALIGN == 0 and LANES % win == 0 and E % (LANES // win) == 0
    assert S % ts == 0
    full = lambda a: pl.BlockSpec(a.shape, lambda b, j, off: (0,) * a.ndim)
    return pl.pallas_call(
        functools.partial(_combine_kernel, cap=cap, win=win, tb=tb),
        grid_spec=pltpu.PrefetchScalarGridSpec(
            num_scalar_prefetch=1,
            grid=(B, S // ts),
            in_specs=[
                pl.BlockSpec((1, E, ts), lambda b, j, off: (b, 0, j)),
                pl.BlockSpec((1, E, ts), lambda b, j, off: (b, 0, j)),
                pl.BlockSpec((E, 1, cap, D), lambda b, j, off: (0, b, 0, 0)),
                pl.BlockSpec((1, ts, D), lambda b, j, off: (b, j, 0)),
                full(g2), full(b2),
            ],
            out_specs=pl.BlockSpec((1, ts, D), lambda b, j, off: (b, j, 0)),
        ),
        out_shape=jax.ShapeDtypeStruct((B, S, D), F32),
        compiler_params=pltpu.CompilerParams(
            dimension_semantics=("arbitrary", "arbitrary"), vmem_limit_bytes=VMEM_LIMIT_BYTES),
        name="combine",
    )(off_flat, rank, gate, y, h1, g2, b2)


def _tile(n, target):
    t = min(n, target)
    assert n % t == 0, (n, t)
    return t


def kernel(x, positions, emb_ln_g, emb_ln_b, w_in, q_norm_g, w_qb, kv_norm_g, w_kvb, conv_w, conv_b,
           conv_ln_g, conv_ln_b, w_o, ln1_g, ln1_b, w_router, w_gate, w_up, w_down, ln2_g, ln2_b):
    B, S, D = x.shape
    T = B * S
    H = MLA_HEADS
    q_rank = q_norm_g.shape[-1]
    kv_rank = kv_norm_g.shape[-1]
    conv_ch = conv_w.shape[-1]
    qk_dim = QK_NOPE_DIM + QK_ROPE_DIM
    cap = CAPACITY_FACTOR * S // N_EXPERTS
    assert w_in.shape[0] == DEPTH == 1
    row = lambda a: a.reshape(1, -1)

    wi = w_in[0]
    c1, c2, c3 = q_rank, q_rank + kv_rank, q_rank + kv_rank + QK_ROPE_DIM
    tail = LANES - QK_NOPE_DIM - QK_ROPE_DIM
    kr_cols = jnp.pad(wi[:, c2:c3], ((0, 0), (QK_NOPE_DIM, tail)))
    w_in_r = jnp.concatenate(
        [wi[:, :c2], wi[:, c3:c3 + conv_ch], wi[:, c3 + conv_ch:], kr_cols], axis=1).astype(BF16)
    w_qb_pad = jnp.pad(w_qb[0].reshape(q_rank, H, qk_dim),
                       ((0, 0), (0, 0), (0, HEAD_PAD - qk_dim))).reshape(q_rank, H * HEAD_PAD).astype(BF16)
    wkv = w_kvb[0].reshape(kv_rank, H, QK_NOPE_DIM + V_HEAD_DIM)
    wk_pad = jnp.pad(wkv[:, :, :QK_NOPE_DIM], ((0, 0), (0, 0), (0, HEAD_PAD - QK_NOPE_DIM)))
    w_kv_r = jnp.concatenate([wk_pad.reshape(kv_rank, H * HEAD_PAD),
                              wkv[:, :, QK_NOPE_DIM:].reshape(kv_rank, H * V_HEAD_DIM)], axis=1).astype(BF16)
    half = QK_ROPE_DIM // 2
    inv_freq = (ROPE_THETA ** (-jnp.arange(half, dtype=F32) / half)).reshape(half, 1)

    x2d = x.reshape(T, D)
    q, k, v, hc = _in_proj(x2d, positions.reshape(1, T), inv_freq, row(emb_ln_g), row(emb_ln_b), w_in_r,
                           row(q_norm_g[0]), w_qb_pad, row(kv_norm_g[0]), w_kv_r, tm=_tile(S, IN_PROJ_TOKEN_TILE))
    n_ct = conv_ch // LANES
    cv = _conv(hc.reshape(B, S, conv_ch), jnp.swapaxes(conv_w[0].reshape(CONV_WIDTH, n_ct, LANES), 0, 1),
               conv_b[0].reshape(n_ct, 1, LANES), row(conv_ln_g[0]), row(conv_ln_b[0]))
    att = _attn(q.reshape(B, S, H * HEAD_PAD), k.reshape(B, S, H * HEAD_PAD),
                v.reshape(B, S, H * V_HEAD_DIM), tq=_tile(S, ATTN_QUERY_TILE))
    wr_hi = w_router[0].astype(BF16)
    w_r2 = jnp.concatenate([wr_hi, (w_router[0] - wr_hi.astype(F32)).astype(BF16)], axis=1)
    w_r2 = jnp.pad(w_r2, ((0, 0), (0, LANES - 2 * N_EXPERTS)))
    h1, lt2 = _out_proj(x2d, att.reshape(T, H * V_HEAD_DIM), cv.reshape(T, conv_ch), row(emb_ln_g),
                        row(emb_ln_b), w_o[0], row(ln1_g[0]), row(ln1_b[0]),
                        w_r2, tm=_tile(S, OUT_PROJ_TOKEN_TILE), seq=S, n_lt=2 * N_EXPERTS)
    h1 = h1.reshape(B, S, D)
    tb = _tile(S, ROUTE_TOKEN_BLOCK)
    rank, gate, off = _route(lt2, cap=cap, tb=tb)
    off_flat = jnp.swapaxes(off[:, :, :S // tb + 1], 1, 2).reshape(-1)
    xg = _dispatch(off_flat, rank, h1, cap=cap, tb=tb, nsb=_tile(S // tb, DISPATCH_BLOCKS_PER_STEP))
    y = _experts(xg.reshape(N_EXPERTS, B * cap, D), w_gate[0], w_up[0], w_down[0],
                 tf=_tile(w_gate.shape[-1], EXPERT_FF_CHUNK)).reshape(N_EXPERTS, B, cap, D)
    return _combine(off_flat, rank, gate, y, h1, row(ln2_g[0]), row(ln2_b[0]), tb=tb,
                    nsb=_tile(S // tb, COMBINE_BLOCKS_PER_STEP))
```
